```python
import jax, jax.numpy as jnp
from jax import lax
import numpy as np

D_MODEL = 2048
BATCH = 4
SEQ = 4096
DEPTH = 1
DEC_BATCH = 16
DEC_SEQ = 32
PAST_LEN = 1024

CHUNK = 64
Q_BLOCK = 128
HGRN_HEADS = 8
HGRN_DK = 128
HGRN_DV = 128
D_HGRN = HGRN_HEADS * HGRN_DK
MLA_HEADS = 8
Q_LORA = 512
KV_LORA = 256
QK_NOPE = 128
QK_ROPE = 64
V_HEAD = 128
D_MLA = MLA_HEADS * V_HEAD
D_MIX = D_HGRN + D_MLA
D_IN = 4 * D_HGRN + Q_LORA + KV_LORA + QK_ROPE
SPLITS = [D_HGRN, 2 * D_HGRN, 3 * D_HGRN, 4 * D_HGRN, 4 * D_HGRN + Q_LORA, 4 * D_HGRN + Q_LORA + KV_LORA]
D_FF = 5632
CONV_W = 3
ROPE_THETA = 10000.0
LN_EPS = 1e-5
RMS_EPS = 1e-6
ALPHA = (2.0 * DEPTH) ** 0.25
BETA = (8.0 * DEPTH) ** -0.25

kernel_name = "hymba_hgrn2_mla_convffn_stream_step"


def layer_norm(x, g, b):
    xf = x.astype(jnp.float32)
    mu = jnp.mean(xf, axis=-1, keepdims=True)
    var = jnp.mean(jnp.square(xf - mu), axis=-1, keepdims=True)
    return ((xf - mu) * lax.rsqrt(var + LN_EPS) * g + b).astype(x.dtype)


def rms_norm(x, g):
    xf = x.astype(jnp.float32)
    return (xf * lax.rsqrt(jnp.mean(xf * xf, axis=-1, keepdims=True) + RMS_EPS) * g).astype(x.dtype)


def rope_tables(pos):
    inv = ROPE_THETA ** (-jnp.arange(0, QK_ROPE, 2, dtype=jnp.float32) / QK_ROPE)
    ang = pos.astype(jnp.float32)[:, None] * inv[None, :]
    return jnp.cos(ang), jnp.sin(ang)


def apply_rope(x, cos, sin):
    xf = x.astype(jnp.float32)
    x1, x2 = jnp.split(xf, 2, axis=-1)
    return jnp.concatenate([x1 * cos - x2 * sin, x1 * sin + x2 * cos], axis=-1).astype(x.dtype)


def hgrn2_recurrence(q, k, log_f, v, s0, blk):
    B, T, H, _ = q.shape
    nb = T // blk

    def to_blocks(a):
        return jnp.moveaxis(a.astype(jnp.float32).reshape(B, nb, blk, H, a.shape[-1]), 1, 0)

    tri = jnp.tril(jnp.ones((blk, blk), dtype=bool))[None, :, :, None, None]

    def step(S, inp):
        qb, kb, gb, vb = inp
        b = jnp.cumsum(gb, axis=1)
        diff = b[:, :, None] - b[:, None, :]
        decay = jnp.exp(jnp.where(tri, diff, -jnp.inf))
        attn = jnp.einsum('bthk,btshk,bshk->bhts', qb, decay, kb)
        o_intra = jnp.einsum('bhts,bshv->bthv', attn, vb)
        o_inter = jnp.einsum('bthk,bhkv->bthv', qb * jnp.exp(b), S)
        b_last = b[:, -1]
        k_dec = kb * jnp.exp(b_last[:, None] - b)
        S_new = jnp.exp(b_last)[..., None] * S + jnp.einsum('bshk,bshv->bhkv', k_dec, vb)
        return S_new, o_intra + o_inter

    S_fin, o = lax.scan(step, s0.astype(jnp.float32),
                        (to_blocks(q), to_blocks(k), to_blocks(log_f), to_blocks(v)))
    o = jnp.moveaxis(o, 0, 1).reshape(B, T, H, v.shape[-1]).astype(v.dtype)
    return o, S_fin


def mla_attention(q_nope, q_rope, k_nope, k_rope, v, q_pos, k_pos):
    B, T, H, _ = q_nope.shape
    nqb = T // Q_BLOCK if T % Q_BLOCK == 0 else 1
    qb = T // nqb
    scale = (QK_NOPE + QK_ROPE) ** -0.5
    k_chunk = k_pos // CHUNK

    def blocks(a):
        return jnp.moveaxis(a.reshape(B, nqb, qb, *a.shape[2:]), 1, 0)

    def attend(args):
        qn, qr, qp = args
        s = (jnp.einsum('bqhd,bkhd->bhqk', qn, k_nope)
             + jnp.einsum('bqhd,bkd->bhqk', qr, k_rope)).astype(jnp.float32) * scale
        mask = k_chunk[None, :] <= (qp // CHUNK)[:, None]
        p = jax.nn.softmax(jnp.where(mask, s, -jnp.inf), axis=-1).astype(v.dtype)
        return jnp.einsum('bhqk,bkhd->bqhd', p, v)

    o = lax.map(attend, (blocks(q_nope), blocks(q_rope), q_pos.reshape(nqb, qb)))
    return jnp.moveaxis(o, 0, 1).reshape(B, T, H, V_HEAD)


def token_mixer(x, s0, lat_past, kr_past, blk, lb, w_in, hgrn_norm_g, q_a_g, w_q_b, kv_a_g, w_kv_b, w_out):
    B, T, _ = x.shape
    past_len = lat_past.shape[1]
    proj = x @ w_in
    hq, hf, hi, hg, cq, ckv, kr = jnp.split(proj, SPLITS, axis=-1)

    f = lb + (1.0 - lb) * jax.nn.sigmoid(hf.astype(jnp.float32))
    heads = lambda a: a.reshape(B, T, HGRN_HEADS, -1)
    o_h, s_new = hgrn2_recurrence(heads(hq), heads(1.0 - f), heads(jnp.log(f)), heads(hi), s0, blk)
    o_h = rms_norm(o_h, hgrn_norm_g.reshape(HGRN_HEADS, HGRN_DV)).reshape(B, T, D_HGRN) * jax.nn.silu(hg)

    q_pos = past_len + jnp.arange(T)
    cos, sin = rope_tables(q_pos)
    q = (rms_norm(cq, q_a_g) @ w_q_b).reshape(B, T, MLA_HEADS, QK_NOPE + QK_ROPE)
    q_nope = q[..., :QK_NOPE]
    q_rope = apply_rope(q[..., QK_NOPE:], cos[:, None, :], sin[:, None, :])
    lat_new = rms_norm(ckv, kv_a_g)
    kr_new = apply_rope(kr, cos, sin)
    lat = jnp.concatenate([lat_past.astype(lat_new.dtype), lat_new], axis=1)
    krs = jnp.concatenate([kr_past.astype(kr_new.dtype), kr_new], axis=1)
    kv = (lat @ w_kv_b).reshape(B, past_len + T, MLA_HEADS, QK_NOPE + V_HEAD)
    k_nope, v = kv[..., :QK_NOPE], kv[..., QK_NOPE:]
    o_a = mla_attention(q_nope, q_rope, k_nope, krs, v, q_pos, jnp.arange(past_len + T))

    out = jnp.concatenate([o_h, o_a.reshape(B, T, D_MLA)], axis=-1) @ w_out
    return out, s_new, lat_new, kr_new


def conv_ffn(x, conv_buf, w_up, w_gate, conv_w, conv_b, w_down):
    T = x.shape[1]
    u = x @ w_up
    ext = jnp.concatenate([conv_buf.astype(u.dtype), u], axis=1)
    a = conv_b + sum(ext[:, j:j + T] * conv_w[j] for j in range(CONV_W))
    h = jax.nn.silu(a) * (x @ w_gate)
    return h @ w_down, ext[:, -(CONV_W - 1):]


def encoder(x, hgrn_state, lat_cache, kr_cache, conv_cache, blk, lb_param, ln_in_g, ln_in_b,
            w_in, hgrn_norm_g, q_a_g, w_q_b, kv_a_g, w_kv_b, w_out, ln1_g, ln1_b,
            w_ffn_up, w_ffn_gate, conv_w, conv_b, w_ffn_down, ln2_g, ln2_b):
    lbs = jnp.cumsum(jax.nn.softmax(lb_param.astype(jnp.float32), axis=0), axis=0)
    h = layer_norm(x, ln_in_g, ln_in_b)
    s_out, lat_out, kr_out, conv_out = [], [], [], []
    for l in range(DEPTH):
        mix, s_new, lat_new, kr_new = token_mixer(h, hgrn_state[l], lat_cache[l], kr_cache[l], blk, lbs[l],
                                                  w_in[l], hgrn_norm_g[l], q_a_g[l], w_q_b[l],
                                                  kv_a_g[l], w_kv_b[l], w_out[l])
        h = layer_norm(ALPHA * h + mix, ln1_g[l], ln1_b[l])
        ff, conv_new = conv_ffn(h, conv_cache[l], w_ffn_up[l], w_ffn_gate[l], conv_w[l], conv_b[l], w_ffn_down[l])
        h = layer_norm(ALPHA * h + ff, ln2_g[l], ln2_b[l])
        s_out.append(s_new); lat_out.append(lat_new); kr_out.append(kr_new); conv_out.append(conv_new)
    return h, jnp.stack(s_out), jnp.stack(lat_out), jnp.stack(kr_out), jnp.stack(conv_out)


def setup_inputs(seed: int = 0) -> dict:
    key = jax.random.key(seed)
    ks = jax.random.split(key, 32)

    def nrm(k, shape, scale):
        return jax.random.normal(k, shape, dtype=jnp.float32) * scale

    return {
        "x_prompt": nrm(ks[0], (BATCH, SEQ, D_MODEL), 1.0),
        "x_sample": nrm(ks[1], (DEC_BATCH, DEC_SEQ, D_MODEL), 1.0),
        "cache_kv_latent": nrm(ks[2], (DEPTH, DEC_BATCH, PAST_LEN, KV_LORA), 1.0),
        "cache_k_rope": nrm(ks[3], (DEPTH, DEC_BATCH, PAST_LEN, QK_ROPE), 1.0),
        "state_hgrn": nrm(ks[4], (DEPTH, DEC_BATCH, HGRN_HEADS, HGRN_DK, HGRN_DV), 0.5),
        "cache_ffn_conv": nrm(ks[5], (DEPTH, DEC_BATCH, CONV_W - 1, D_FF), 1.0),
        "lb_param": nrm(ks[6], (DEPTH + 1, D_HGRN), 1.0),
        "ln_in_g": 1.0 + nrm(ks[7], (D_MODEL,), 0.02),
        "ln_in_b": nrm(ks[8], (D_MODEL,), 0.02),
        "w_in": nrm(ks[9], (DEPTH, D_MODEL, D_IN), D_MODEL ** -0.5),
        "hgrn_norm_g": 1.0 + nrm(ks[10], (DEPTH, D_HGRN), 0.02),
        "q_a_g": 1.0 + nrm(ks[11], (DEPTH, Q_LORA), 0.02),
        "w_q_b": nrm(ks[12], (DEPTH, Q_LORA, MLA_HEADS * (QK_NOPE + QK_ROPE)), Q_LORA ** -0.5),
        "kv_a_g": 1.0 + nrm(ks[13], (DEPTH, KV_LORA), 0.02),
        "w_kv_b": nrm(ks[14], (DEPTH, KV_LORA, MLA_HEADS * (QK_NOPE + V_HEAD)), KV_LORA ** -0.5),
        "w_out": nrm(ks[15], (DEPTH, D_MIX, D_MODEL), BETA * D_MIX ** -0.5),
        "ln1_g": 1.0 + nrm(ks[16], (DEPTH, D_MODEL), 0.02),
        "ln1_b": nrm(ks[17], (DEPTH, D_MODEL), 0.02),
        "w_ffn_up": nrm(ks[18], (DEPTH, D_MODEL, D_FF), D_MODEL ** -0.5),
        "w_ffn_gate": nrm(ks[19], (DEPTH, D_MODEL, D_FF), D_MODEL ** -0.5),
        "conv_w": nrm(ks[20], (DEPTH, CONV_W, D_FF), CONV_W ** -0.5),
        "conv_b": nrm(ks[21], (DEPTH, D_FF), 0.02),
        "w_ffn_down": nrm(ks[22], (DEPTH, D_FF, D_MODEL), BETA * D_FF ** -0.5),
        "ln2_g": 1.0 + nrm(ks[23], (DEPTH, D_MODEL), 0.02),
        "ln2_b": nrm(ks[24], (DEPTH, D_MODEL), 0.02),
    }


def reference(x_prompt, x_sample, cache_kv_latent, cache_k_rope, state_hgrn, cache_ffn_conv,
              lb_param, ln_in_g, ln_in_b, w_in, hgrn_norm_g, q_a_g, w_q_b, kv_a_g, w_kv_b, w_out,
              ln1_g, ln1_b, w_ffn_up, w_ffn_gate, conv_w, conv_b, w_ffn_down, ln2_g, ln2_b):
    B, T = x_prompt.shape[0], x_prompt.shape[1]
    p_s0 = jnp.zeros((DEPTH, B, HGRN_HEADS, HGRN_DK, HGRN_DV), jnp.float32)
    p_lat0 = jnp.zeros((DEPTH, B, 0, KV_LORA), x_prompt.dtype)
    p_kr0 = jnp.zeros((DEPTH, B, 0, QK_ROPE), x_prompt.dtype)
    p_conv0 = jnp.zeros((DEPTH, B, CONV_W - 1, D_FF), x_prompt.dtype)
    y_prompt, p_state_hgrn, p_kv_latent, p_k_rope, p_ffn_conv = encoder(
        x_prompt, p_s0, p_lat0, p_kr0, p_conv0, CHUNK, lb_param, ln_in_g, ln_in_b,
        w_in, hgrn_norm_g, q_a_g, w_q_b, kv_a_g, w_kv_b, w_out, ln1_g, ln1_b,
        w_ffn_up, w_ffn_gate, conv_w, conv_b, w_ffn_down, ln2_g, ln2_b)
    y_sample, s_state_hgrn, s_kv_latent, s_k_rope, s_ffn_conv = encoder(
        x_sample, state_hgrn, cache_kv_latent, cache_k_rope, cache_ffn_conv, x_sample.shape[1],
        lb_param, ln_in_g, ln_in_b,
        w_in, hgrn_norm_g, q_a_g, w_q_b, kv_a_g, w_kv_b, w_out, ln1_g, ln1_b,
        w_ffn_up, w_ffn_gate, conv_w, conv_b, w_ffn_down, ln2_g, ln2_b)
    return (y_prompt, y_sample, p_kv_latent, p_k_rope, p_state_hgrn, p_ffn_conv,
            s_kv_latent, s_k_rope, s_state_hgrn, s_ffn_conv)
```

```python
import functools

import jax
import jax.numpy as jnp
from jax import lax
from jax.experimental import pallas as pl
from jax.experimental.pallas import tpu as pltpu

F32 = jnp.float32
BF16 = jnp.bfloat16

D_MODEL = 2048
CHUNK = 64
HGRN_HEADS = 8
HGRN_DK = 128
HGRN_DV = 128
D_HGRN = HGRN_HEADS * HGRN_DK
MLA_HEADS = 8
Q_LORA = 512
KV_LORA = 256
QK_NOPE = 128
QK_ROPE = 64
V_HEAD = 128
D_MLA = MLA_HEADS * V_HEAD
D_FF = 5632
CONV_W = 3
ROPE_THETA = 10000.0
LN_EPS = 1e-5
RMS_EPS = 1e-6
DEPTH = 1
ALPHA = (2.0 * DEPTH) ** 0.25

QK_PAD = 256
ROPE_PAD = 128
SUB = 16
NEG_BIG = -1e30
V7X_VMEM_LIMIT = 56 * 1024 * 1024

NT_DIMS = (((1,), (1,)), ((), ()))
TN_DIMS = (((0,), (0,)), ((), ()))


def _cparams(sem):
    return pltpu.CompilerParams(dimension_semantics=sem, vmem_limit_bytes=V7X_VMEM_LIMIT)


def _layer_norm(x, g, b):
    mu = jnp.mean(x, axis=-1, keepdims=True)
    xc = x - mu
    var = jnp.mean(xc * xc, axis=-1, keepdims=True)
    return xc * lax.rsqrt(var + LN_EPS) * g + b


def _rms_norm(x, g):
    return x * lax.rsqrt(jnp.mean(x * x, axis=-1, keepdims=True) + RMS_EPS) * g


N_HG_TILES = 4
W_MLA_COLS = Q_LORA + KV_LORA + 2 * ROPE_PAD


def _in_proj_kernel(x_ref, g_ref, b_ref, wh_ref, wm_ref, qg_ref, kvg_ref, cos_ref, sin_ref,
                    hg_ref, cq_ref, lat_ref, kr_ref, hb_ref):
    j = pl.program_id(1)

    @pl.when(j == 0)
    def _():
        hb_ref[...] = _layer_norm(x_ref[...], g_ref[...], b_ref[...]).astype(BF16)

    @pl.when(j < N_HG_TILES)
    def _():
        hg_ref[...] = jnp.dot(hb_ref[...], wh_ref[...], preferred_element_type=F32)

    @pl.when(j == N_HG_TILES)
    def _():
        p = jnp.dot(hb_ref[...], wm_ref[...], preferred_element_type=F32)
        cq = p[:, :Q_LORA]
        ckv = p[:, Q_LORA:Q_LORA + KV_LORA]
        kr = p[:, Q_LORA + KV_LORA:Q_LORA + KV_LORA + ROPE_PAD]
        kr_sw = p[:, Q_LORA + KV_LORA + ROPE_PAD:]
        cq_ref[...] = _rms_norm(cq, qg_ref[...]).astype(BF16)
        lat_ref[...] = _rms_norm(ckv, kvg_ref[...])
        kr_ref[...] = kr * cos_ref[...] + kr_sw * sin_ref[...]


def _in_proj(x, ln_g, ln_b, w_h, w_m, q_a_g, kv_a_g, cos_k, sin_k, tm):
    m = x.shape[0]
    n_pos_tiles = cos_k.shape[0] // tm
    grid = (m // tm, N_HG_TILES + 1)
    row = lambda i, j: (i, 0)
    const = lambda i, j: (0, 0)
    pos = lambda i, j: (i % n_pos_tiles, 0)
    hg_col = lambda i, j: (i, jnp.minimum(j, N_HG_TILES - 1))
    return pl.pallas_call(
        _in_proj_kernel,
        grid=grid,
        in_specs=[
            pl.BlockSpec((tm, D_MODEL), row),
            pl.BlockSpec((1, D_MODEL), const),
            pl.BlockSpec((1, D_MODEL), const),
            pl.BlockSpec((D_MODEL, D_HGRN), lambda i, j: (0, jnp.minimum(j, N_HG_TILES - 1))),
            pl.BlockSpec((D_MODEL, W_MLA_COLS), const),
            pl.BlockSpec((1, Q_LORA), const),
            pl.BlockSpec((1, KV_LORA), const),
            pl.BlockSpec((tm, ROPE_PAD), pos),
            pl.BlockSpec((tm, ROPE_PAD), pos),
        ],
        out_specs=[
            pl.BlockSpec((tm, D_HGRN), hg_col),
            pl.BlockSpec((tm, Q_LORA), row),
            pl.BlockSpec((tm, KV_LORA), row),
            pl.BlockSpec((tm, ROPE_PAD), row),
        ],
        out_shape=[
            jax.ShapeDtypeStruct((m, N_HG_TILES * D_HGRN), F32),
            jax.ShapeDtypeStruct((m, Q_LORA), BF16),
            jax.ShapeDtypeStruct((m, KV_LORA), F32),
            jax.ShapeDtypeStruct((m, ROPE_PAD), F32),
        ],
        scratch_shapes=[pltpu.VMEM((tm, D_MODEL), BF16)],
        compiler_params=_cparams(("arbitrary", "arbitrary")),
        name="in_proj",
    )(x, ln_g, ln_b, w_h, w_m, q_a_g, kv_a_g, cos_k, sin_k)


def _hgrn_head_block(q, fpre, v, lb, st):
    c = q.shape[0]
    n_sub = c // SUB
    f = lb + (1.0 - lb) * jax.nn.sigmoid(fpre)
    g = jnp.log(f)
    k = 1.0 - f
    ri = lax.broadcasted_iota(jnp.int32, (c, c), 0)
    ci = lax.broadcasted_iota(jnp.int32, (c, c), 1)
    tril = (ci <= ri).astype(F32)
    b = jnp.dot(tril, g, preferred_element_type=F32, precision=lax.Precision.HIGHEST)

    row_idx = lax.broadcasted_iota(jnp.int32, (c, 1), 0)
    sub_rows = lax.broadcasted_iota(jnp.int32, (SUB, 1), 0)
    lane = lax.broadcasted_iota(jnp.int32, (SUB, c), 1)

    a_rows = []
    for i in range(n_sub):
        lo = i * SUB
        q_i = q[lo:lo + SUB]
        b_i = b[lo:lo + SUB]
        k_i = k[lo:lo + SUB]
        a_i = jnp.zeros((SUB, c), F32)
        for s in range(SUB):
            diff = b_i - b_i[s:s + 1]
            w = jnp.exp(jnp.where(sub_rows >= s, diff, -jnp.inf))
            col = jnp.sum(q_i * k_i[s:s + 1] * w, axis=-1, keepdims=True)
            a_i = jnp.where(lane == lo + s, col, a_i)
        if i > 0:
            b0 = b[lo - 1:lo]
            q_t = (q_i * jnp.exp(b_i - b0)).astype(BF16)
            k_t = (k * jnp.exp(jnp.where(row_idx < lo, b0 - b, -jnp.inf))).astype(BF16)
            a_i = a_i + lax.dot_general(q_t, k_t, NT_DIMS, preferred_element_type=F32)
        a_rows.append(a_i)
    a = a_rows[0] if n_sub == 1 else jnp.concatenate(a_rows, axis=0)

    vb = v.astype(BF16)
    o = jnp.dot(a.astype(BF16), vb, preferred_element_type=F32)
    o = o + lax.dot_general((q * jnp.exp(b)).astype(BF16), st.astype(BF16), NT_DIMS,
                            preferred_element_type=F32)
    b_last = b[c - 1:c]
    k_dec = (k * jnp.exp(b_last - b)).astype(BF16)
    st_new = jnp.exp(b_last) * st + lax.dot_general(vb, k_dec, TN_DIMS, preferred_element_type=F32)
    return o, st_new


def _hgrn_kernel(*refs, c, has_s0):
    if has_s0:
        q_ref, f_ref, i_ref, g_ref, lb_ref, ng_ref, s0_ref, o_ref, s_ref, st_ref = refs
    else:
        q_ref, f_ref, i_ref, g_ref, lb_ref, ng_ref, o_ref, s_ref, st_ref = refs
        s0_ref = None
    t = pl.program_id(1)

    @pl.when(t == 0)
    def _():
        for h in range(HGRN_HEADS):
            if has_s0:
                st_ref[h] = s0_ref[0, h].T
            else:
                st_ref[h] = jnp.zeros((HGRN_DV, HGRN_DK), F32)

    for h in range(HGRN_HEADS):
        cols = slice(h * HGRN_DK, (h + 1) * HGRN_DK)
        o, st_new = _hgrn_head_block(q_ref[:, cols], f_ref[:, cols], i_ref[:, cols],
                                     lb_ref[:, cols], st_ref[h])
        st_ref[h] = st_new
        o = _rms_norm(o, ng_ref[:, cols]) * jax.nn.silu(g_ref[:, cols])
        o_ref[:, cols] = o.astype(BF16)

    @pl.when(t == pl.num_programs(1) - 1)
    def _():
        for h in range(HGRN_HEADS):
            s_ref[0, h] = st_ref[h].T


def _hgrn(hg, lb, norm_g, s0, batch, seq, c):
    nt = seq // c
    has_s0 = s0 is not None
    col = lambda n: (lambda b, t: (b * nt + t, n))
    const = lambda b, t: (0, 0)
    state = lambda b, t: (b, 0, 0, 0)
    in_specs = [pl.BlockSpec((c, D_HGRN), col(n)) for n in range(4)]
    in_specs += [pl.BlockSpec((1, D_HGRN), const), pl.BlockSpec((1, D_HGRN), const)]
    args = [hg, hg, hg, hg, lb, norm_g]
    if has_s0:
        in_specs.append(pl.BlockSpec((1, HGRN_HEADS, HGRN_DK, HGRN_DV), state))
        args.append(s0)
    return pl.pallas_call(
        functools.partial(_hgrn_kernel, c=c, has_s0=has_s0),
        grid=(batch, nt),
        in_specs=in_specs,
        out_specs=[
            pl.BlockSpec((c, D_HGRN), lambda b, t: (b * nt + t, 0)),
            pl.BlockSpec((1, HGRN_HEADS, HGRN_DK, HGRN_DV), state),
        ],
        out_shape=[
            jax.ShapeDtypeStruct((batch * seq, D_HGRN), BF16),
            jax.ShapeDtypeStruct((batch, HGRN_HEADS, HGRN_DK, HGRN_DV), F32),
        ],
        scratch_shapes=[pltpu.VMEM((HGRN_HEADS, HGRN_DV, HGRN_DK), F32)],
        compiler_params=_cparams(("arbitrary", "arbitrary")),
        name="hgrn",
    )(*args)


def _q_proj_kernel(cq_ref, wq_ref, wsw_ref, cos_ref, sin_ref, q_ref, *, scale):
    cq = cq_ref[...]
    q_lin = jnp.dot(cq, wq_ref[...], preferred_element_type=F32)
    q_sw = jnp.dot(cq, wsw_ref[...], preferred_element_type=F32)
    cos = cos_ref[...]
    sin = sin_ref[...]
    for h in range(MLA_HEADS):
        base = h * QK_PAD
        q_ref[:, base:base + QK_NOPE] = (q_lin[:, base:base + QK_NOPE] * scale).astype(BF16)
        rot = (q_lin[:, base + QK_NOPE:base + QK_PAD] * cos
               + q_sw[:, h * ROPE_PAD:(h + 1) * ROPE_PAD] * sin)
        q_ref[:, base + QK_NOPE:base + QK_PAD] = (rot * scale).astype(BF16)


def _q_proj(cq, w_q, w_q_sw, cos_k, sin_k, tm):
    m = cq.shape[0]
    n_pos_tiles = cos_k.shape[0] // tm
    scale = (QK_NOPE + QK_ROPE) ** -0.5
    row = lambda i: (i, 0)
    const = lambda i: (0, 0)
    pos = lambda i: (i % n_pos_tiles, 0)
    return pl.pallas_call(
        functools.partial(_q_proj_kernel, scale=scale),
        grid=(m // tm,),
        in_specs=[
            pl.BlockSpec((tm, Q_LORA), row),
            pl.BlockSpec((Q_LORA, MLA_HEADS * QK_PAD), const),
            pl.BlockSpec((Q_LORA, MLA_HEADS * ROPE_PAD), const),
            pl.BlockSpec((tm, ROPE_PAD), pos),
            pl.BlockSpec((tm, ROPE_PAD), pos),
        ],
        out_specs=pl.BlockSpec((tm, MLA_HEADS * QK_PAD), row),
        out_shape=jax.ShapeDtypeStruct((m, MLA_HEADS * QK_PAD), BF16),
        compiler_params=_cparams(("arbitrary",)),
        name="q_proj",
    )(cq, w_q, w_q_sw, cos_k, sin_k)


def _kv_proj_kernel(lat_ref, kr_ref, wk_ref, wv_ref, k_ref, v_ref):
    lat = lat_ref[...].astype(BF16)
    kn = jnp.dot(lat, wk_ref[...], preferred_element_type=F32)
    v_ref[...] = jnp.dot(lat, wv_ref[...], preferred_element_type=F32).astype(BF16)
    kr = kr_ref[...].astype(BF16)
    for h in range(MLA_HEADS):
        base = h * QK_PAD
        k_ref[:, base:base + QK_NOPE] = kn[:, h * QK_NOPE:(h + 1) * QK_NOPE].astype(BF16)
        k_ref[:, base + QK_NOPE:base + QK_PAD] = kr


def _kv_proj(lat, kr_pad, w_k, w_v, tm):
    m = lat.shape[0]
    row = lambda i: (i, 0)
    const = lambda i: (0, 0)
    return pl.pallas_call(
        _kv_proj_kernel,
        grid=(m // tm,),
        in_specs=[
            pl.BlockSpec((tm, KV_LORA), row),
            pl.BlockSpec((tm, ROPE_PAD), row),
            pl.BlockSpec((KV_LORA, MLA_HEADS * QK_NOPE), const),
            pl.BlockSpec((KV_LORA, D_MLA), const),
        ],
        out_specs=[
            pl.BlockSpec((tm, MLA_HEADS * QK_PAD), row),
            pl.BlockSpec((tm, D_MLA), row),
        ],
        out_shape=[
            jax.ShapeDtypeStruct((m, MLA_HEADS * QK_PAD), BF16),
            jax.ShapeDtypeStruct((m, D_MLA), BF16),
        ],
        compiler_params=_cparams(("arbitrary",)),
        name="kv_proj",
    )(lat, kr_pad, w_k, w_v)


def _attn_kernel(q_ref, k_ref, v_ref, o_ref, *, tq, tk, past, n_keys):
    qi = pl.program_id(2)
    q = q_ref[0]
    q_pos0 = past + qi * tq
    q_chunk = (q_pos0 + lax.broadcasted_iota(jnp.int32, (tq, 1), 0)) // CHUNK
    k_end = jnp.minimum(n_keys, ((q_pos0 + tq - 1) // CHUNK + 1) * CHUNK)
    n_tiles = (k_end + tk - 1) // tk
    k_lane = lax.broadcasted_iota(jnp.int32, (1, tk), 1)

    def body(j, carry):
        m, l, acc = carry
        ks = pl.multiple_of(j * tk, tk)
        k = k_ref[0, pl.ds(ks, tk), :]
        v = v_ref[0, pl.ds(ks, tk), :]
        s = lax.dot_general(q, k, NT_DIMS, preferred_element_type=F32)
        k_idx = ks + k_lane
        visible = jnp.logical_and(k_idx // CHUNK <= q_chunk, k_idx < n_keys)
        s = jnp.where(visible, s, NEG_BIG)
        m_new = jnp.maximum(m, jnp.max(s, axis=-1, keepdims=True))
        alpha = jnp.exp(m - m_new)
        p = jnp.exp(s - m_new)
        l = alpha * l + jnp.sum(p, axis=-1, keepdims=True)
        acc = alpha * acc + jnp.dot(p.astype(BF16), v, preferred_element_type=F32)
        return m_new, l, acc

    m0 = jnp.full((tq, 1), NEG_BIG, F32)
    l0 = jnp.zeros((tq, 1), F32)
    acc0 = jnp.zeros((tq, V_HEAD), F32)
    _, l, acc = lax.fori_loop(0, n_tiles, body, (m0, l0, acc0))
    o_ref[0] = (acc / l).astype(BF16)


def _attention(q, k, v, batch, seq, n_keys_pad, n_keys, past, tq, tk):
    return pl.pallas_call(
        functools.partial(_attn_kernel, tq=tq, tk=tk, past=past, n_keys=n_keys),
        grid=(batch, MLA_HEADS, seq // tq),
        in_specs=[
            pl.BlockSpec((1, tq, QK_PAD), lambda b, h, i: (b, i, h)),
            pl.BlockSpec((1, n_keys_pad, QK_PAD), lambda b, h, i: (b, 0, h)),
            pl.BlockSpec((1, n_keys_pad, V_HEAD), lambda b, h, i: (b, 0, h)),
        ],
        out_specs=pl.BlockSpec((1, tq, V_HEAD), lambda b, h, i: (b, i, h)),
        out_shape=jax.ShapeDtypeStruct((batch, seq, D_MLA), BF16),
        compiler_params=_cparams(("arbitrary", "arbitrary", "arbitrary")),
        name="attn",
    )(q, k, v)


def _out_proj_kernel(oh_ref, oa_ref, x_ref, g0_ref, b0_ref, w_ref, g1_ref, b1_ref, y_ref):
    mix = jnp.dot(oh_ref[...], w_ref[:D_HGRN, :], preferred_element_type=F32)
    mix = mix + jnp.dot(oa_ref[...], w_ref[D_HGRN:, :], preferred_element_type=F32)
    h = _layer_norm(x_ref[...], g0_ref[...], b0_ref[...])
    y_ref[...] = _layer_norm(ALPHA * h + mix, g1_ref[...], b1_ref[...])


def _out_proj(oh, oa, x, ln_in_g, ln_in_b, w_out, ln1_g, ln1_b, tm):
    m = x.shape[0]
    row = lambda i: (i, 0)
    const = lambda i: (0, 0)
    vec = pl.BlockSpec((1, D_MODEL), const)
    return pl.pallas_call(
        _out_proj_kernel,
        grid=(m // tm,),
        in_specs=[
            pl.BlockSpec((tm, D_HGRN), row),
            pl.BlockSpec((tm, D_MLA), row),
            pl.BlockSpec((tm, D_MODEL), row),
            vec, vec,
            pl.BlockSpec((D_HGRN + D_MLA, D_MODEL), const),
            vec, vec,
        ],
        out_specs=pl.BlockSpec((tm, D_MODEL), row),
        out_shape=jax.ShapeDtypeStruct((m, D_MODEL), F32),
        compiler_params=_cparams(("arbitrary",)),
        name="out_proj",
    )(oh, oa, x, ln_in_g, ln_in_b, w_out, ln1_g, ln1_b)


def _ffn_kernel(x_ref, wu_ref, wg_ref, wd_ref, cw_ref, cb_ref, buf_ref, g2_ref, b2_ref,
                y_ref, cnew_ref, xb_ref, carry_ref, *, spt, tiles_per_seq):
    i = pl.program_id(0)
    f = pl.program_id(1)
    tm = x_ref.shape[0]
    tf = wu_ref.shape[1]
    rows = tm // spt

    @pl.when(f == 0)
    def _():
        xb_ref[...] = x_ref[...].astype(BF16)

    xb = xb_ref[...]
    u = jnp.dot(xb, wu_ref[...], preferred_element_type=F32).reshape(spt, rows, tf)
    gate = jnp.dot(xb, wg_ref[...], preferred_element_type=F32).reshape(spt, rows, tf)

    if tiles_per_seq > 1:
        @pl.when(i % tiles_per_seq == 0)
        def _():
            carry_ref[f] = buf_ref[...]
        halo = carry_ref[f]
    else:
        halo = buf_ref[...]
    r = lax.broadcasted_iota(jnp.int32, (spt, rows, tf), 1)
    prev1 = jnp.where(r == 0, halo[:, 1:2, :], pltpu.roll(u, 1, 1))
    prev2 = jnp.where(r == 0, halo[:, 0:1, :],
                      jnp.where(r == 1, halo[:, 1:2, :], pltpu.roll(u, 2, 1)))
    cw = cw_ref[...]
    a = cb_ref[...] + (prev2 * cw[0:1] + prev1 * cw[1:2] + u * cw[2:3])
    hid = (jax.nn.silu(a) * gate).astype(BF16).reshape(tm, tf)
    part = jnp.dot(hid, wd_ref[...], preferred_element_type=F32)

    tail = u[:, rows - (CONV_W - 1):, :]
    cnew_ref[...] = tail
    if tiles_per_seq > 1:
        carry_ref[f] = tail

    @pl.when(f == 0)
    def _():
        y_ref[...] = part

    @pl.when(f > 0)
    def _():
        y_ref[...] += part

    @pl.when(f == pl.num_programs(1) - 1)
    def _():
        y_ref[...] = _layer_norm(ALPHA * x_ref[...] + y_ref[...], g2_ref[...], b2_ref[...])


def _ffn(x, w_up, w_gate, w_down, conv_w, conv_b, conv_buf, ln2_g, ln2_b, seq, tm, tf):
    m = x.shape[0]
    n_seq = m // seq
    if tm >= seq:
        spt, tiles_per_seq = tm // seq, 1
        seq_of = lambda i: i
    else:
        spt, tiles_per_seq = 1, seq // tm
        seq_of = lambda i: i // tiles_per_seq
    n_f = D_FF // tf
    row = lambda i, f: (i, 0)
    const = lambda i, f: (0, 0)
    y, tails = pl.pallas_call(
        functools.partial(_ffn_kernel, spt=spt, tiles_per_seq=tiles_per_seq),
        grid=(m // tm, n_f),
        in_specs=[
            pl.BlockSpec((tm, D_MODEL), row),
            pl.BlockSpec((D_MODEL, tf), lambda i, f: (0, f)),
            pl.BlockSpec((D_MODEL, tf), lambda i, f: (0, f)),
            pl.BlockSpec((tf, D_MODEL), lambda i, f: (f, 0)),
            pl.BlockSpec((CONV_W, tf), lambda i, f: (0, f)),
            pl.BlockSpec((1, tf), lambda i, f: (0, f)),
            pl.BlockSpec((spt, CONV_W - 1, tf), lambda i, f: (seq_of(i), 0, f)),
            pl.BlockSpec((1, D_MODEL), const),
            pl.BlockSpec((1, D_MODEL), const),
        ],
        out_specs=[
            pl.BlockSpec((tm, D_MODEL), row),
            pl.BlockSpec((spt, CONV_W - 1, tf), lambda i, f: (i, 0, f)),
        ],
        out_shape=[
            jax.ShapeDtypeStruct((m, D_MODEL), F32),
            jax.ShapeDtypeStruct((n_seq * tiles_per_seq, CONV_W - 1, D_FF), F32),
        ],
        scratch_shapes=[
            pltpu.VMEM((tm, D_MODEL), BF16),
            pltpu.VMEM((n_f, spt, CONV_W - 1, tf), F32),
        ],
        compiler_params=_cparams(("arbitrary", "arbitrary")),
        name="ffn",
    )(x, w_up, w_gate, w_down, conv_w, conv_b, conv_buf, ln2_g, ln2_b)
    return y, tails[tiles_per_seq - 1::tiles_per_seq]


def _rope_tables(past, seq, rows):
    inv = ROPE_THETA ** (-jnp.arange(0, QK_ROPE, 2, dtype=F32) / QK_ROPE)
    ang = (past + jnp.arange(seq)).astype(F32)[:, None] * inv[None, :]
    cos, sin = jnp.cos(ang), jnp.sin(ang)
    zero = jnp.zeros((seq, ROPE_PAD - QK_ROPE), F32)
    cos_t = jnp.concatenate([cos, cos, zero], axis=1)
    sin_t = jnp.concatenate([-sin, sin, zero], axis=1)
    reps = rows // seq
    return jnp.tile(cos_t, (reps, 1)), jnp.tile(sin_t, (reps, 1))


def _swap_halves(w):
    half = QK_ROPE // 2
    return jnp.concatenate([w[..., half:], w[..., :half]], axis=-1)


def _prep_weights(w_in, w_q_b, w_kv_b, w_out, w_ffn_up, w_ffn_gate, w_ffn_down):
    w_in = w_in[0]
    c0 = 4 * D_HGRN
    w_h = w_in[:, :c0].astype(BF16)
    kr_cols = w_in[:, c0 + Q_LORA + KV_LORA:]
    zpad = jnp.zeros((D_MODEL, ROPE_PAD - QK_ROPE), F32)
    w_m = jnp.concatenate([w_in[:, c0:c0 + Q_LORA + KV_LORA], kr_cols, zpad,
                           _swap_halves(kr_cols), zpad], axis=1).astype(BF16)

    wq = w_q_b[0].reshape(Q_LORA, MLA_HEADS, QK_NOPE + QK_ROPE)
    zq = jnp.zeros((Q_LORA, MLA_HEADS, QK_PAD - QK_NOPE - QK_ROPE), F32)
    w_q = jnp.concatenate([wq, zq], axis=-1).reshape(Q_LORA, MLA_HEADS * QK_PAD).astype(BF16)
    zr = jnp.zeros((Q_LORA, MLA_HEADS, ROPE_PAD - QK_ROPE), F32)
    w_q_sw = jnp.concatenate([_swap_halves(wq[..., QK_NOPE:]), zr], axis=-1)
    w_q_sw = w_q_sw.reshape(Q_LORA, MLA_HEADS * ROPE_PAD).astype(BF16)

    wkv = w_kv_b[0].reshape(KV_LORA, MLA_HEADS, QK_NOPE + V_HEAD)
    w_k = wkv[..., :QK_NOPE].reshape(KV_LORA, MLA_HEADS * QK_NOPE).astype(BF16)
    w_v = wkv[..., QK_NOPE:].reshape(KV_LORA, D_MLA).astype(BF16)
    return dict(w_h=w_h, w_m=w_m, w_q=w_q, w_q_sw=w_q_sw, w_k=w_k, w_v=w_v,
                w_out=w_out[0].astype(BF16), w_up=w_ffn_up[0].astype(BF16),
                w_gate=w_ffn_gate[0].astype(BF16), w_down=w_ffn_down[0].astype(BF16))


def _row(a):
    return a.reshape(1, -1)


def _encoder(x, s0, lat_past, kr_past, conv_buf, p, w, *, blk, tm, tq, tk, key_pad):
    batch, seq, _ = x.shape
    past = 0 if lat_past is None else lat_past.shape[1]
    m = batch * seq
    x2 = x.reshape(m, D_MODEL)
    cos_k, sin_k = _rope_tables(past, seq, max(seq, tm))

    hg, cqn, lat_new, kr_new = _in_proj(x2, p["ln_in_g"], p["ln_in_b"], w["w_h"], w["w_m"],
                                        p["q_a_g"], p["kv_a_g"], cos_k, sin_k, tm)
    o_h, s_new = _hgrn(hg, p["lb"], p["hgrn_norm_g"], s0, batch, seq, blk)

    q = _q_proj(cqn, w["w_q"], w["w_q_sw"], cos_k, sin_k, tm)
    lat3 = lat_new.reshape(batch, seq, KV_LORA)
    kr3 = kr_new.reshape(batch, seq, ROPE_PAD)
    n_keys = past + seq
    n_keys_pad = n_keys + key_pad
    if past or key_pad:
        parts_l, parts_r = [], []
        if past:
            parts_l.append(lat_past)
            parts_r.append(jnp.pad(kr_past, ((0, 0), (0, 0), (0, ROPE_PAD - QK_ROPE))))
        parts_l.append(lat3)
        parts_r.append(kr3)
        if key_pad:
            parts_l.append(jnp.zeros((batch, key_pad, KV_LORA), F32))
            parts_r.append(jnp.zeros((batch, key_pad, ROPE_PAD), F32))
        lat_all = jnp.concatenate(parts_l, axis=1)
        kr_all = jnp.concatenate(parts_r, axis=1)
    else:
        lat_all, kr_all = lat3, kr3
    kv_tm = n_keys_pad if n_keys_pad <= 2048 else tm
    k_cat, v = _kv_proj(lat_all.reshape(batch * n_keys_pad, KV_LORA),
                        kr_all.reshape(batch * n_keys_pad, ROPE_PAD), w["w_k"], w["w_v"], kv_tm)
    o_a = _attention(q.reshape(batch, seq, MLA_HEADS * QK_PAD),
                     k_cat.reshape(batch, n_keys_pad, MLA_HEADS * QK_PAD),
                     v.reshape(batch, n_keys_pad, D_MLA),
                     batch, seq, n_keys_pad, n_keys, past, tq, tk)

    h1 = _out_proj(o_h, o_a.reshape(m, D_MLA), x2, p["ln_in_g"], p["ln_in_b"], w["w_out"],
                   p["ln1_g"], p["ln1_b"], tm)
    y, conv_new = _ffn(h1, w["w_up"], w["w_gate"], w["w_down"], p["conv_w"], p["conv_b"],
                       conv_buf, p["ln2_g"], p["ln2_b"], seq, tm, 512)
    return (y.reshape(batch, seq, D_MODEL), s_new[None], lat3[None],
            kr3[:, :, :QK_ROPE][None], conv_new[None])


def kernel(x_prompt, x_sample, cache_kv_latent, cache_k_rope, state_hgrn, cache_ffn_conv, lb_param, ln_in_g, ln_in_b, w_in, hgrn_norm_g, q_a_g, w_q_b, kv_a_g, w_kv_b, w_out, ln1_g, ln1_b, w_ffn_up, w_ffn_gate, conv_w, conv_b, w_ffn_down, ln2_g, ln2_b):
    w = _prep_weights(w_in, w_q_b, w_kv_b, w_out, w_ffn_up, w_ffn_gate, w_ffn_down)
    lbs = jnp.cumsum(jax.nn.softmax(lb_param.astype(F32), axis=0), axis=0)
    p = dict(ln_in_g=_row(ln_in_g), ln_in_b=_row(ln_in_b), lb=_row(lbs[0]),
             hgrn_norm_g=_row(hgrn_norm_g[0]), q_a_g=_row(q_a_g[0]), kv_a_g=_row(kv_a_g[0]),
             ln1_g=_row(ln1_g[0]), ln1_b=_row(ln1_b[0]), conv_w=conv_w[0], conv_b=_row(conv_b[0]),
             ln2_g=_row(ln2_g[0]), ln2_b=_row(ln2_b[0]))

    n_prompt = x_prompt.shape[0]
    conv0 = jnp.zeros((n_prompt, CONV_W - 1, D_FF), F32)
    y_p, s_p, lat_p, kr_p, conv_p = _encoder(
        x_prompt, None, None, None, conv0, p, w, blk=CHUNK, tm=512, tq=512, tk=512, key_pad=0)

    dec_seq = x_sample.shape[1]
    n_keys = cache_kv_latent.shape[2] + dec_seq
    key_pad = -n_keys % 128
    y_s, s_s, lat_s, kr_s, conv_s = _encoder(
        x_sample, state_hgrn[0], cache_kv_latent[0], cache_k_rope[0], cache_ffn_conv[0], p, w,
        blk=dec_seq, tm=x_sample.shape[0] * dec_seq, tq=dec_seq, tk=n_keys + key_pad, key_pad=key_pad)
    return (y_p, y_s, lat_p, kr_p, s_p, conv_p, lat_s, kr_s, s_s, conv_s)
```

```python
import functools

import jax
import jax.numpy as jnp
from jax import lax
from jax.experimental import pallas as pl
from jax.experimental.pallas import tpu as pltpu

F32 = jnp.float32
BF16 = jnp.bfloat16

D_MODEL = 2048
CHUNK = 64
HGRN_HEADS = 8
HGRN_DK = 128
HGRN_DV = 128
D_HGRN = HGRN_HEADS * HGRN_DK
MLA_HEADS = 8
Q_LORA = 512
KV_LORA = 256
QK_NOPE = 128
QK_ROPE = 64
V_HEAD = 128
D_MLA = MLA_HEADS * V_HEAD
D_FF = 5632
CONV_W = 3
ROPE_THETA = 10000.0
LN_EPS = 1e-5
RMS_EPS = 1e-6
DEPTH = 1
ALPHA = (2.0 * DEPTH) ** 0.25

QK_PAD = 256
ROPE_PAD = 128
ROW_SUB = 256
ATTN_UNROLL = 4
FFN_SUB = 256
NEG_BIG = -1e30
LOG2_E = 1.4426950408889634
V7X_VMEM_LIMIT = 56 * 1024 * 1024

NT_DIMS = (((1,), (1,)), ((), ()))
TN_DIMS = (((0,), (0,)), ((), ()))


def _cparams(sem):
    return pltpu.CompilerParams(dimension_semantics=sem, vmem_limit_bytes=V7X_VMEM_LIMIT)


def _layer_norm(x, g, b):
    mu = jnp.mean(x, axis=-1, keepdims=True)
    xc = x - mu
    var = jnp.mean(xc * xc, axis=-1, keepdims=True)
    return xc * lax.rsqrt(var + LN_EPS) * g + b


def _rms_norm(x, g):
    return x * lax.rsqrt(jnp.mean(x * x, axis=-1, keepdims=True) + RMS_EPS) * g


N_HG_TILES = 4
W_MLA_COLS = Q_LORA + KV_LORA + 2 * ROPE_PAD


def _in_proj_kernel(x_ref, g_ref, b_ref, wh_ref, wm_ref, qg_ref, kvg_ref, cos_ref, sin_ref,
                    hg_ref, cq_ref, lat_ref, kr_ref, hb_ref):
    j = pl.program_id(1)

    @pl.when(j == 0)
    def _():
        for r0 in range(0, x_ref.shape[0], ROW_SUB):
            rows = slice(r0, r0 + ROW_SUB)
            hb = _layer_norm(x_ref[rows, :], g_ref[...], b_ref[...]).astype(BF16)
            hb_ref[rows, :] = hb
            hg_ref[rows, :] = jnp.dot(hb, wh_ref[...], preferred_element_type=F32)

    @pl.when(jnp.logical_and(j > 0, j < N_HG_TILES))
    def _():
        hg_ref[...] = jnp.dot(hb_ref[...], wh_ref[...], preferred_element_type=F32)

    @pl.when(j == N_HG_TILES)
    def _():
        p = jnp.dot(hb_ref[...], wm_ref[...], preferred_element_type=F32)
        cq = p[:, :Q_LORA]
        ckv = p[:, Q_LORA:Q_LORA + KV_LORA]
        kr = p[:, Q_LORA + KV_LORA:Q_LORA + KV_LORA + ROPE_PAD]
        kr_sw = p[:, Q_LORA + KV_LORA + ROPE_PAD:]
        cq_ref[...] = _rms_norm(cq, qg_ref[...]).astype(BF16)
        lat_ref[...] = _rms_norm(ckv, kvg_ref[...])
        kr_ref[...] = kr * cos_ref[...] + kr_sw * sin_ref[...]


def _in_proj(x, ln_g, ln_b, w_h, w_m, q_a_g, kv_a_g, cos_k, sin_k, tm):
    m = x.shape[0]
    n_pos_tiles = cos_k.shape[0] // tm
    grid = (m // tm, N_HG_TILES + 1)
    row = lambda i, j: (i, 0)
    const = lambda i, j: (0, 0)
    pos = lambda i, j: (i % n_pos_tiles, 0)
    hg_col = lambda i, j: (i, jnp.minimum(j, N_HG_TILES - 1))
    return pl.pallas_call(
        _in_proj_kernel,
        grid=grid,
        in_specs=[
            pl.BlockSpec((tm, D_MODEL), row),
            pl.BlockSpec((1, D_MODEL), const),
            pl.BlockSpec((1, D_MODEL), const),
            pl.BlockSpec((D_MODEL, D_HGRN), lambda i, j: (0, jnp.minimum(j, N_HG_TILES - 1))),
            pl.BlockSpec((D_MODEL, W_MLA_COLS), const),
            pl.BlockSpec((1, Q_LORA), const),
            pl.BlockSpec((1, KV_LORA), const),
            pl.BlockSpec((tm, ROPE_PAD), pos),
            pl.BlockSpec((tm, ROPE_PAD), pos),
        ],
        out_specs=[
            pl.BlockSpec((tm, D_HGRN), hg_col),
            pl.BlockSpec((tm, Q_LORA), row),
            pl.BlockSpec((tm, KV_LORA), row),
            pl.BlockSpec((tm, ROPE_PAD), row),
        ],
        out_shape=[
            jax.ShapeDtypeStruct((m, N_HG_TILES * D_HGRN), F32),
            jax.ShapeDtypeStruct((m, Q_LORA), BF16),
            jax.ShapeDtypeStruct((m, KV_LORA), F32),
            jax.ShapeDtypeStruct((m, ROPE_PAD), F32),
        ],
        scratch_shapes=[pltpu.VMEM((tm, D_MODEL), BF16)],
        compiler_params=_cparams(("arbitrary", "arbitrary")),
        name="in_proj",
    )(x, ln_g, ln_b, w_h, w_m, q_a_g, kv_a_g, cos_k, sin_k)


def _cumsum_rows(x):
    c = x.shape[0]
    row = lax.broadcasted_iota(jnp.int32, (c, 1), 0)
    shift = 1
    while shift < c:
        x = x + jnp.where(row >= shift, pltpu.roll(x, shift, 0), 0.0)
        shift *= 2
    return x


def _split_level(c):
    t = lax.broadcasted_iota(jnp.int32, (c, c), 0)
    s = lax.broadcasted_iota(jnp.int32, (c, c), 1)
    x = jnp.bitwise_xor(t, s)
    lvl = jnp.full((c, c), -1, jnp.int32)
    for bit in range(c.bit_length() - 1):
        lvl = jnp.where(x >= (1 << bit), bit, lvl)
    return jnp.where(t < s, -2, lvl)


def _hgrn_head_block(q, fpre, v, lb, st, lvl):
    c = q.shape[0]
    f = lb + (1.0 - lb) * jax.nn.sigmoid(fpre)
    g = jnp.log(f)
    k = 1.0 - f
    b = _cumsum_rows(g)
    row = lax.broadcasted_iota(jnp.int32, (c, 1), 0)

    a = jnp.where(lvl == -1, jnp.sum(q * k, axis=-1, keepdims=True), 0.0)
    for level in range(c.bit_length() - 1):
        half = 1 << level
        pos = row % (2 * half)
        is_query = pos >= half
        if level == 0:
            decay = jnp.where(is_query, f, 1.0)
        else:
            if 2 * half < 8:
                ref = b
                for p in range(2 * half):
                    d = half - 1 - p
                    if d != 0:
                        ref = jnp.where(pos == p, pltpu.roll(b, (-d) % c, 0), ref)
            else:
                ref = jnp.concatenate(
                    [jnp.broadcast_to(b[j + half - 1:j + half], (2 * half, b.shape[1]))
                     for j in range(0, c, 2 * half)], axis=0)
            decay = jnp.exp(-jnp.abs(b - ref))
        z = (jnp.where(is_query, q, k) * decay).astype(BF16)
        gram = lax.dot_general(z, z, NT_DIMS, preferred_element_type=F32)
        a = jnp.where(lvl == level, gram, a)

    vb = v.astype(BF16)
    o = jnp.dot(a.astype(BF16), vb, preferred_element_type=F32)
    o = o + lax.dot_general((q * jnp.exp(b)).astype(BF16), st.astype(BF16), NT_DIMS,
                            preferred_element_type=F32)
    b_last = b[c - 1:c]
    k_dec = (k * jnp.exp(b_last - b)).astype(BF16)
    st_new = jnp.exp(b_last) * st + lax.dot_general(vb, k_dec, TN_DIMS, preferred_element_type=F32)
    return o, st_new


def _hgrn_kernel(*refs, c, has_s0):
    if has_s0:
        q_ref, f_ref, i_ref, g_ref, lb_ref, ng_ref, s0_ref, o_ref, s_ref, st_ref = refs
    else:
        q_ref, f_ref, i_ref, g_ref, lb_ref, ng_ref, o_ref, s_ref, st_ref = refs
        s0_ref = None
    t = pl.program_id(1)

    @pl.when(t == 0)
    def _():
        for h in range(HGRN_HEADS):
            if has_s0:
                st_ref[h] = s0_ref[0, h].T
            else:
                st_ref[h] = jnp.zeros((HGRN_DV, HGRN_DK), F32)

    lvl = _split_level(c)
    for h in range(HGRN_HEADS):
        cols = slice(h * HGRN_DK, (h + 1) * HGRN_DK)
        o, st_new = _hgrn_head_block(q_ref[:, cols], f_ref[:, cols], i_ref[:, cols],
                                     lb_ref[:, cols], st_ref[h], lvl)
        st_ref[h] = st_new
        o = _rms_norm(o, ng_ref[:, cols]) * jax.nn.silu(g_ref[:, cols])
        o_ref[:, cols] = o.astype(BF16)

    @pl.when(t == pl.num_programs(1) - 1)
    def _():
        for h in range(HGRN_HEADS):
            s_ref[0, h] = st_ref[h].T


def _hgrn(hg, lb, norm_g, s0, batch, seq, c):
    nt = seq // c
    has_s0 = s0 is not None
    col = lambda n: (lambda b, t: (b * nt + t, n))
    const = lambda b, t: (0, 0)
    state = lambda b, t: (b, 0, 0, 0)
    in_specs = [pl.BlockSpec((c, D_HGRN), col(n)) for n in range(4)]
    in_specs += [pl.BlockSpec((1, D_HGRN), const), pl.BlockSpec((1, D_HGRN), const)]
    args = [hg, hg, hg, hg, lb, norm_g]
    if has_s0:
        in_specs.append(pl.BlockSpec((1, HGRN_HEADS, HGRN_DK, HGRN_DV), state))
        args.append(s0)
    return pl.pallas_call(
        functools.partial(_hgrn_kernel, c=c, has_s0=has_s0),
        grid=(batch, nt),
        in_specs=in_specs,
        out_specs=[
            pl.BlockSpec((c, D_HGRN), lambda b, t: (b * nt + t, 0)),
            pl.BlockSpec((1, HGRN_HEADS, HGRN_DK, HGRN_DV), state),
        ],
        out_shape=[
            jax.ShapeDtypeStruct((batch * seq, D_HGRN), BF16),
            jax.ShapeDtypeStruct((batch, HGRN_HEADS, HGRN_DK, HGRN_DV), F32),
        ],
        scratch_shapes=[pltpu.VMEM((HGRN_HEADS, HGRN_DV, HGRN_DK), F32)],
        compiler_params=_cparams(("arbitrary", "arbitrary")),
        name="hgrn",
    )(*args)


def _q_proj_kernel(cq_ref, wq_ref, wsw_ref, cos_ref, sin_ref, q_ref, *, scale):
    cq = cq_ref[...]
    q_lin = jnp.dot(cq, wq_ref[...], preferred_element_type=F32)
    q_sw = jnp.dot(cq, wsw_ref[...], preferred_element_type=F32)
    cos = cos_ref[...]
    sin = sin_ref[...]
    for h in range(MLA_HEADS):
        base = h * QK_PAD
        q_ref[:, base:base + QK_NOPE] = (q_lin[:, base:base + QK_NOPE] * scale).astype(BF16)
        rot = (q_lin[:, base + QK_NOPE:base + QK_PAD] * cos
               + q_sw[:, h * ROPE_PAD:(h + 1) * ROPE_PAD] * sin)
        q_ref[:, base + QK_NOPE:base + QK_PAD] = (rot * scale).astype(BF16)


def _q_proj(cq, w_q, w_q_sw, cos_k, sin_k, tm):
    m = cq.shape[0]
    n_pos_tiles = cos_k.shape[0] // tm
    scale = LOG2_E * (QK_NOPE + QK_ROPE) ** -0.5
    row = lambda i: (i, 0)
    const = lambda i: (0, 0)
    pos = lambda i: (i % n_pos_tiles, 0)
    return pl.pallas_call(
        functools.partial(_q_proj_kernel, scale=scale),
        grid=(m // tm,),
        in_specs=[
            pl.BlockSpec((tm, Q_LORA), row),
            pl.BlockSpec((Q_LORA, MLA_HEADS * QK_PAD), const),
            pl.BlockSpec((Q_LORA, MLA_HEADS * ROPE_PAD), const),
            pl.BlockSpec((tm, ROPE_PAD), pos),
            pl.BlockSpec((tm, ROPE_PAD), pos),
        ],
        out_specs=pl.BlockSpec((tm, MLA_HEADS * QK_PAD), row),
        out_shape=jax.ShapeDtypeStruct((m, MLA_HEADS * QK_PAD), BF16),
        compiler_params=_cparams(("arbitrary",)),
        name="q_proj",
    )(cq, w_q, w_q_sw, cos_k, sin_k)


def _kv_proj_kernel(lat_ref, kr_ref, wk_ref, wv_ref, k_ref, v_ref):
    lat = lat_ref[...].astype(BF16)
    kn = jnp.dot(lat, wk_ref[...], preferred_element_type=F32)
    v_ref[...] = jnp.dot(lat, wv_ref[...], preferred_element_type=F32).astype(BF16)
    kr = kr_ref[...].astype(BF16)
    for h in range(MLA_HEADS):
        base = h * QK_PAD
        k_ref[:, base:base + QK_NOPE] = kn[:, h * QK_NOPE:(h + 1) * QK_NOPE].astype(BF16)
        k_ref[:, base + QK_NOPE:base + QK_PAD] = kr


def _kv_proj(lat, kr_pad, w_k, w_v, tm):
    m = lat.shape[0]
    row = lambda i: (i, 0)
    const = lambda i: (0, 0)
    return pl.pallas_call(
        _kv_proj_kernel,
        grid=(m // tm,),
        in_specs=[
            pl.BlockSpec((tm, KV_LORA), row),
            pl.BlockSpec((tm, ROPE_PAD), row),
            pl.BlockSpec((KV_LORA, MLA_HEADS * QK_NOPE), const),
            pl.BlockSpec((KV_LORA, D_MLA), const),
        ],
        out_specs=[
            pl.BlockSpec((tm, MLA_HEADS * QK_PAD), row),
            pl.BlockSpec((tm, D_MLA), row),
        ],
        out_shape=[
            jax.ShapeDtypeStruct((m, MLA_HEADS * QK_PAD), BF16),
            jax.ShapeDtypeStruct((m, D_MLA), BF16),
        ],
        compiler_params=_cparams(("arbitrary",)),
        name="kv_proj",
    )(lat, kr_pad, w_k, w_v)


def _attn_kernel(tile_q_ref, tile_k_ref, q_ref, k_ref, v_ref, o_ref,
                 vx_ref, rel_ref, sa_ref, sb_ref, m_ref, acc_ref, *, tq, tk, past, n_keys, n_plain, n_tiles):
    @pl.when(jnp.logical_and(pl.program_id(0) == 0, pl.program_id(1) == 0))
    def _():
        rel_ref[...] = (lax.broadcasted_iota(jnp.int32, (tq, tk), 1) // CHUNK
                        - lax.broadcasted_iota(jnp.int32, (tq, tk), 0) // CHUNK)

    vx_ref[:, :V_HEAD] = v_ref[0]
    vx_ref[:, V_HEAD:] = jnp.ones((vx_ref.shape[0], V_HEAD), BF16)
    m_ref[...] = jnp.full(m_ref.shape, NEG_BIG, F32)
    acc_ref[...] = jnp.zeros(acc_ref.shape, F32)
    k_lane = lax.broadcasted_iota(jnp.int32, (1, tk), 1)

    def origin(t):
        return pl.multiple_of(tile_q_ref[t] * tq, tq), pl.multiple_of(tile_k_ref[t] * tk, tk)

    def scores(t, s_ref):
        qs, ks = origin(t)
        s_ref[...] = lax.dot_general(q_ref[0, pl.ds(qs, tq), :], k_ref[0, pl.ds(ks, tk), :], NT_DIMS,
                                     preferred_element_type=F32)

    def softmax_pv(t, s_ref, masked):
        qs, ks = origin(t)
        s = s_ref[...]
        if masked:
            pad = jnp.where(k_lane < n_keys - ks, 0, 1 << 20)
            visible = rel_ref[...] + pad <= (past + qs - ks) // CHUNK
            s = jnp.where(visible, s, NEG_BIG)
        rows = pl.ds(qs, tq)
        m = m_ref[rows, :]
        m_new = jnp.maximum(m, jnp.max(s, axis=-1, keepdims=True))
        p = jnp.exp2(s - m_new).astype(BF16)
        pv = jnp.dot(p, vx_ref[pl.ds(ks, tk), :], preferred_element_type=F32)
        acc_ref[rows, :] = jnp.exp2(m - m_new) * acc_ref[rows, :] + pv
        m_ref[rows, :] = m_new

    bufs = (sa_ref, sb_ref)

    def step(t, parity, masked):
        scores(t + 1, bufs[1 - parity])
        softmax_pv(t, bufs[parity], masked)

    def run(t0, t1, masked):
        t = t0
        if t < t1 and t % 2 == 1:
            step(t, 1, masked)
            t += 1
        trips = (t1 - t) // ATTN_UNROLL
        if trips > 0:
            def body(i, _, base=t):
                tt = base + ATTN_UNROLL * i
                for u in range(ATTN_UNROLL):
                    step(tt + u, u % 2, masked)
                return 0
            lax.fori_loop(0, trips, body, 0)
            t += ATTN_UNROLL * trips
        while t < t1:
            step(t, t % 2, masked)
            t += 1

    scores(0, sa_ref)
    run(0, min(n_plain, n_tiles - 1), False)
    run(min(n_plain, n_tiles - 1), n_tiles - 1, True)
    last = n_tiles - 1
    softmax_pv(last, bufs[last % 2], last >= n_plain)
    acc = acc_ref[...]
    o_ref[0] = (acc[:, :V_HEAD] / acc[:, V_HEAD:]).astype(BF16)


def _attention_tiles(seq, n_keys_pad, n_keys, past, tq, tk):
    plain, masked = [], []
    for j in range(n_keys_pad // tk):
        for i in range(seq // tq):
            q_lo, k_lo = past + i * tq, j * tk
            q_hi, k_hi = q_lo + tq - 1, k_lo + tk - 1
            if k_lo // CHUNK > q_hi // CHUNK or k_lo >= n_keys:
                continue
            if k_hi // CHUNK <= q_lo // CHUNK and k_hi < n_keys:
                plain.append((i, j))
            else:
                masked.append((i, j))
    return plain, masked


def _attention(q, k, v, batch, seq, n_keys_pad, n_keys, past, tq, tk):
    assert past % CHUNK == 0 and tk % CHUNK == 0 and (tq % CHUNK == 0 or tq == seq)
    plain, masked = _attention_tiles(seq, n_keys_pad, n_keys, past, tq, tk)
    tiles = plain + masked
    tile_q = jnp.asarray([t[0] for t in tiles], jnp.int32)
    tile_k = jnp.asarray([t[1] for t in tiles], jnp.int32)
    smem = pl.BlockSpec(memory_space=pltpu.SMEM)
    return pl.pallas_call(
        functools.partial(_attn_kernel, tq=tq, tk=tk, past=past, n_keys=n_keys,
                          n_plain=len(plain), n_tiles=len(tiles)),
        grid=(batch, MLA_HEADS),
        in_specs=[
            smem, smem,
            pl.BlockSpec((1, seq, QK_PAD), lambda b, h: (b, 0, h)),
            pl.BlockSpec((1, n_keys_pad, QK_PAD), lambda b, h: (b, 0, h)),
            pl.BlockSpec((1, n_keys_pad, V_HEAD), lambda b, h: (b, 0, h)),
        ],
        out_specs=pl.BlockSpec((1, seq, V_HEAD), lambda b, h: (b, 0, h)),
        out_shape=jax.ShapeDtypeStruct((batch, seq, D_MLA), BF16),
        scratch_shapes=[
            pltpu.VMEM((n_keys_pad, 2 * V_HEAD), BF16),
            pltpu.VMEM((tq, tk), jnp.int32),
            pltpu.VMEM((tq, tk), F32),
            pltpu.VMEM((tq, tk), F32),
            pltpu.VMEM((seq, 1), F32),
            pltpu.VMEM((seq, 2 * V_HEAD), F32),
        ],
        compiler_params=_cparams(("arbitrary", "arbitrary")),
        name="attn",
    )(tile_q, tile_k, q, k, v)


def _out_proj_kernel(oh_ref, oa_ref, x_ref, g0_ref, b0_ref, w_ref, g1_ref, b1_ref, y_ref):
    for r0 in range(0, x_ref.shape[0], ROW_SUB):
        rows = slice(r0, r0 + ROW_SUB)
        mix = jnp.dot(oh_ref[rows, :], w_ref[:D_HGRN, :], preferred_element_type=F32)
        mix = mix + jnp.dot(oa_ref[rows, :], w_ref[D_HGRN:, :], preferred_element_type=F32)
        h = _layer_norm(x_ref[rows, :], g0_ref[...], b0_ref[...])
        y_ref[rows, :] = _layer_norm(ALPHA * h + mix, g1_ref[...], b1_ref[...])


def _out_proj(oh, oa, x, ln_in_g, ln_in_b, w_out, ln1_g, ln1_b, tm):
    m = x.shape[0]
    row = lambda i: (i, 0)
    const = lambda i: (0, 0)
    vec = pl.BlockSpec((1, D_MODEL), const)
    return pl.pallas_call(
        _out_proj_kernel,
        grid=(m // tm,),
        in_specs=[
            pl.BlockSpec((tm, D_HGRN), row),
            pl.BlockSpec((tm, D_MLA), row),
            pl.BlockSpec((tm, D_MODEL), row),
            vec, vec,
            pl.BlockSpec((D_HGRN + D_MLA, D_MODEL), const),
            vec, vec,
        ],
        out_specs=pl.BlockSpec((tm, D_MODEL), row),
        out_shape=jax.ShapeDtypeStruct((m, D_MODEL), F32),
        compiler_params=_cparams(("arbitrary",)),
        name="out_proj",
    )(oh, oa, x, ln_in_g, ln_in_b, w_out, ln1_g, ln1_b)


def _ffn_kernel(x_ref, wu_ref, wg_ref, wd_ref, cw_ref, cb_ref, buf_ref, g2_ref, b2_ref,
                y_ref, cnew_ref, xb_ref, carry_ref, *, spt, tiles_per_seq):
    i = pl.program_id(0)
    f = pl.program_id(1)
    tm = x_ref.shape[0]
    tf = wu_ref.shape[1]
    rows = tm // spt

    @pl.when(f == 0)
    def _():
        xb_ref[...] = x_ref[...].astype(BF16)
        y_ref[...] = jnp.zeros(y_ref.shape, F32)

    if tiles_per_seq > 1:
        @pl.when(i % tiles_per_seq == 0)
        def _():
            carry_ref[f] = buf_ref[...]
        halo_ref = carry_ref.at[f]
    else:
        halo_ref = buf_ref

    xb = xb_ref[...]
    r = lax.broadcasted_iota(jnp.int32, (spt, rows, FFN_SUB), 1)
    for c0 in range(0, tf, FFN_SUB):
        cs = slice(c0, c0 + FFN_SUB)
        u = jnp.dot(xb, wu_ref[:, cs], preferred_element_type=F32).reshape(spt, rows, FFN_SUB)
        gate = jnp.dot(xb, wg_ref[:, cs], preferred_element_type=F32).reshape(spt, rows, FFN_SUB)
        halo = halo_ref[:, :, cs]
        prev1 = jnp.where(r == 0, halo[:, 1:2, :], pltpu.roll(u, 1, 1))
        prev2 = jnp.where(r == 0, halo[:, 0:1, :],
                          jnp.where(r == 1, halo[:, 1:2, :], pltpu.roll(u, 2, 1)))
        cw = cw_ref[:, cs]
        a = cb_ref[:, cs] + (prev2 * cw[0:1] + prev1 * cw[1:2] + u * cw[2:3])
        hid = (jax.nn.silu(a) * gate).astype(BF16).reshape(tm, FFN_SUB)
        y_ref[...] += jnp.dot(hid, wd_ref[cs, :], preferred_element_type=F32)
        tail = u[:, rows - (CONV_W - 1):, :]
        cnew_ref[:, :, cs] = tail
        if tiles_per_seq > 1:
            carry_ref[f, :, :, cs] = tail

    @pl.when(f == pl.num_programs(1) - 1)
    def _():
        y_ref[...] = _layer_norm(ALPHA * x_ref[...] + y_ref[...], g2_ref[...], b2_ref[...])


def _ffn(x, w_up, w_gate, w_down, conv_w, conv_b, conv_buf, ln2_g, ln2_b, seq, tm, tf):
    m = x.shape[0]
    n_seq = m // seq
    if tm >= seq:
        spt, tiles_per_seq = tm // seq, 1
        seq_of = lambda i: i
    else:
        spt, tiles_per_seq = 1, seq // tm
        seq_of = lambda i: i // tiles_per_seq
    n_f = D_FF // tf
    row = lambda i, f: (i, 0)
    const = lambda i, f: (0, 0)
    y, tails = pl.pallas_call(
        functools.partial(_ffn_kernel, spt=spt, tiles_per_seq=tiles_per_seq),
        grid=(m // tm, n_f),
        in_specs=[
            pl.BlockSpec((tm, D_MODEL), row),
            pl.BlockSpec((D_MODEL, tf), lambda i, f: (0, f)),
            pl.BlockSpec((D_MODEL, tf), lambda i, f: (0, f)),
            pl.BlockSpec((tf, D_MODEL), lambda i, f: (f, 0)),
            pl.BlockSpec((CONV_W, tf), lambda i, f: (0, f)),
            pl.BlockSpec((1, tf), lambda i, f: (0, f)),
            pl.BlockSpec((spt, CONV_W - 1, tf), lambda i, f: (seq_of(i), 0, f)),
            pl.BlockSpec((1, D_MODEL), const),
            pl.BlockSpec((1, D_MODEL), const),
        ],
        out_specs=[
            pl.BlockSpec((tm, D_MODEL), row),
            pl.BlockSpec((spt, CONV_W - 1, tf), lambda i, f: (i, 0, f)),
        ],
        out_shape=[
            jax.ShapeDtypeStruct((m, D_MODEL), F32),
            jax.ShapeDtypeStruct((n_seq * tiles_per_seq, CONV_W - 1, D_FF), F32),
        ],
        scratch_shapes=[
            pltpu.VMEM((tm, D_MODEL), BF16),
            pltpu.VMEM((n_f, spt, CONV_W - 1, tf), F32),
        ],
        compiler_params=_cparams(("arbitrary", "arbitrary")),
        name="ffn",
    )(x, w_up, w_gate, w_down, conv_w, conv_b, conv_buf, ln2_g, ln2_b)
    return y, tails[tiles_per_seq - 1::tiles_per_seq]


def _rope_tables(past, seq, rows):
    inv = ROPE_THETA ** (-jnp.arange(0, QK_ROPE, 2, dtype=F32) / QK_ROPE)
    ang = (past + jnp.arange(seq)).astype(F32)[:, None] * inv[None, :]
    cos, sin = jnp.cos(ang), jnp.sin(ang)
    zero = jnp.zeros((seq, ROPE_PAD - QK_ROPE), F32)
    cos_t = jnp.concatenate([cos, cos, zero], axis=1)
    sin_t = jnp.concatenate([-sin, sin, zero], axis=1)
    reps = rows // seq
    return jnp.tile(cos_t, (reps, 1)), jnp.tile(sin_t, (reps, 1))


def _swap_halves(w):
    half = QK_ROPE // 2
    return jnp.concatenate([w[..., half:], w[..., :half]], axis=-1)


def _prep_weights(w_in, w_q_b, w_kv_b, w_out, w_ffn_up, w_ffn_gate, w_ffn_down):
    w_in = w_in[0]
    c0 = 4 * D_HGRN
    w_h = w_in[:, :c0].astype(BF16)
    kr_cols = w_in[:, c0 + Q_LORA + KV_LORA:]
    zpad = jnp.zeros((D_MODEL, ROPE_PAD - QK_ROPE), F32)
    w_m = jnp.concatenate([w_in[:, c0:c0 + Q_LORA + KV_LORA], kr_cols, zpad,
                           _swap_halves(kr_cols), zpad], axis=1).astype(BF16)

    wq = w_q_b[0].reshape(Q_LORA, MLA_HEADS, QK_NOPE + QK_ROPE)
    zq = jnp.zeros((Q_LORA, MLA_HEADS, QK_PAD - QK_NOPE - QK_ROPE), F32)
    w_q = jnp.concatenate([wq, zq], axis=-1).reshape(Q_LORA, MLA_HEADS * QK_PAD).astype(BF16)
    zr = jnp.zeros((Q_LORA, MLA_HEADS, ROPE_PAD - QK_ROPE), F32)
    w_q_sw = jnp.concatenate([_swap_halves(wq[..., QK_NOPE:]), zr], axis=-1)
    w_q_sw = w_q_sw.reshape(Q_LORA, MLA_HEADS * ROPE_PAD).astype(BF16)

    wkv = w_kv_b[0].reshape(KV_LORA, MLA_HEADS, QK_NOPE + V_HEAD)
    w_k = wkv[..., :QK_NOPE].reshape(KV_LORA, MLA_HEADS * QK_NOPE).astype(BF16)
    w_v = wkv[..., QK_NOPE:].reshape(KV_LORA, D_MLA).astype(BF16)
    return dict(w_h=w_h, w_m=w_m, w_q=w_q, w_q_sw=w_q_sw, w_k=w_k, w_v=w_v,
                w_out=w_out[0].astype(BF16), w_up=w_ffn_up[0].astype(BF16),
                w_gate=w_ffn_gate[0].astype(BF16), w_down=w_ffn_down[0].astype(BF16))


def _row(a):
    return a.reshape(1, -1)


ROW_TILE = 512
IN_PROJ_ROW_TILE = 1024
ATTN_TILE = 512
FFN_COL_TILE = 512
KEY_ALIGN = 128


def _tile_plan(batch, seq, n_keys_pad):
    m = batch * seq
    tm = min(ROW_TILE, m)
    assert m % tm == 0 and (tm % seq == 0 or seq % tm == 0)
    return dict(
        tm=tm,
        tm_in=min(IN_PROJ_ROW_TILE, m),
        tq=min(ATTN_TILE, seq),
        tk=ATTN_TILE if n_keys_pad % ATTN_TILE == 0 else n_keys_pad,
        tm_kv=ROW_TILE if n_keys_pad % ROW_TILE == 0 else n_keys_pad,
        tf=FFN_COL_TILE,
    )


def _encoder(x, s0, lat_past, kr_past, conv_buf, p, w, *, blk):
    batch, seq, _ = x.shape
    past = 0 if lat_past is None else lat_past.shape[1]
    m = batch * seq
    n_keys = past + seq
    key_pad = -n_keys % KEY_ALIGN
    n_keys_pad = n_keys + key_pad
    t = _tile_plan(batch, seq, n_keys_pad)
    x2 = x.reshape(m, D_MODEL)
    cos_k, sin_k = _rope_tables(past, seq, max(seq, t["tm_in"]))

    hg, cqn, lat_new, kr_new = _in_proj(x2, p["ln_in_g"], p["ln_in_b"], w["w_h"], w["w_m"],
                                        p["q_a_g"], p["kv_a_g"], cos_k, sin_k, t["tm_in"])
    o_h, s_new = _hgrn(hg, p["lb"], p["hgrn_norm_g"], s0, batch, seq, blk)

    q = _q_proj(cqn, w["w_q"], w["w_q_sw"], cos_k, sin_k, t["tm"])
    lat3 = lat_new.reshape(batch, seq, KV_LORA)
    kr3 = kr_new.reshape(batch, seq, ROPE_PAD)
    if past or key_pad:
        parts_l, parts_r = [], []
        if past:
            parts_l.append(lat_past)
            parts_r.append(jnp.pad(kr_past, ((0, 0), (0, 0), (0, ROPE_PAD - QK_ROPE))))
        parts_l.append(lat3)
        parts_r.append(kr3)
        if key_pad:
            parts_l.append(jnp.zeros((batch, key_pad, KV_LORA), F32))
            parts_r.append(jnp.zeros((batch, key_pad, ROPE_PAD), F32))
        lat_all = jnp.concatenate(parts_l, axis=1)
        kr_all = jnp.concatenate(parts_r, axis=1)
    else:
        lat_all, kr_all = lat3, kr3
    k_cat, v = _kv_proj(lat_all.reshape(batch * n_keys_pad, KV_LORA),
                        kr_all.reshape(batch * n_keys_pad, ROPE_PAD), w["w_k"], w["w_v"], t["tm_kv"])
    o_a = _attention(q.reshape(batch, seq, MLA_HEADS * QK_PAD),
                     k_cat.reshape(batch, n_keys_pad, MLA_HEADS * QK_PAD),
                     v.reshape(batch, n_keys_pad, D_MLA),
                     batch, seq, n_keys_pad, n_keys, past, t["tq"], t["tk"])

    h1 = _out_proj(o_h, o_a.reshape(m, D_MLA), x2, p["ln_in_g"], p["ln_in_b"], w["w_out"],
                   p["ln1_g"], p["ln1_b"], t["tm"])
    y, conv_new = _ffn(h1, w["w_up"], w["w_gate"], w["w_down"], p["conv_w"], p["conv_b"],
                       conv_buf, p["ln2_g"], p["ln2_b"], seq, t["tm"], t["tf"])
    return (y.reshape(batch, seq, D_MODEL), s_new[None], lat3[None],
            kr3[:, :, :QK_ROPE][None], conv_new[None])


def kernel(x_prompt, x_sample, cache_kv_latent, cache_k_rope, state_hgrn, cache_ffn_conv, lb_param, ln_in_g, ln_in_b, w_in, hgrn_norm_g, q_a_g, w_q_b, kv_a_g, w_kv_b, w_out, ln1_g, ln1_b, w_ffn_up, w_ffn_gate, conv_w, conv_b, w_ffn_down, ln2_g, ln2_b):
    w = _prep_weights(w_in, w_q_b, w_kv_b, w_out, w_ffn_up, w_ffn_gate, w_ffn_down)
    lbs = jnp.cumsum(jax.nn.softmax(lb_param.astype(F32), axis=0), axis=0)
    p = dict(ln_in_g=_row(ln_in_g), ln_in_b=_row(ln_in_b), lb=_row(lbs[0]),
             hgrn_norm_g=_row(hgrn_norm_g[0]), q_a_g=_row(q_a_g[0]), kv_a_g=_row(kv_a_g[0]),
             ln1_g=_row(ln1_g[0]), ln1_b=_row(ln1_b[0]), conv_w=conv_w[0], conv_b=_row(conv_b[0]),
             ln2_g=_row(ln2_g[0]), ln2_b=_row(ln2_b[0]))

    n_prompt = x_prompt.shape[0]
    conv0 = jnp.zeros((n_prompt, CONV_W - 1, D_FF), F32)
    y_p, s_p, lat_p, kr_p, conv_p = _encoder(x_prompt, None, None, None, conv0, p, w, blk=CHUNK)
    y_s, s_s, lat_s, kr_s, conv_s = _encoder(
        x_sample, state_hgrn[0], cache_kv_latent[0], cache_k_rope[0], cache_ffn_conv[0], p, w,
        blk=x_sample.shape[1])
    return (y_p, y_s, lat_p, kr_p, s_p, conv_p, lat_s, kr_s, s_s, conv_s)
```

```python
import functools

import jax
import jax.numpy as jnp
from jax import lax
from jax.experimental import pallas as pl
from jax.experimental.pallas import tpu as pltpu

F32 = jnp.float32
BF16 = jnp.bfloat16

D_MODEL = 2048
CHUNK = 64
HGRN_HEADS = 8
HGRN_DK = 128
HGRN_DV = 128
D_HGRN = HGRN_HEADS * HGRN_DK
MLA_HEADS = 8
Q_LORA = 512
KV_LORA = 256
QK_NOPE = 128
QK_ROPE = 64
V_HEAD = 128
D_MLA = MLA_HEADS * V_HEAD
D_FF = 5632
CONV_W = 3
ROPE_THETA = 10000.0
LN_EPS = 1e-5
RMS_EPS = 1e-6
DEPTH = 1
ALPHA = (2.0 * DEPTH) ** 0.25

QK_PAD = 256
ROPE_PAD = 128
ROW_TILE = 512
BIG_ROW_TILE = 1024
HGRN_STEP_ROWS = 256
ABSORB_MAX_ROWS = 512
ATTN_TILE = 512
FFN_COL_TILE = 512
KEY_ALIGN = 128
ROW_SUB = 256
ATTN_UNROLL = 4
FFN_SUB = 256
NEG_BIG = -1e30
LOG2_E = 1.4426950408889634
V7X_VMEM_LIMIT = 56 * 1024 * 1024

NT_DIMS = (((1,), (1,)), ((), ()))
TN_DIMS = (((0,), (0,)), ((), ()))


def _cparams(sem):
    return pltpu.CompilerParams(dimension_semantics=sem, vmem_limit_bytes=V7X_VMEM_LIMIT)


def _layer_norm(x, g, b):
    mu = jnp.mean(x, axis=-1, keepdims=True)
    xc = x - mu
    var = jnp.mean(xc * xc, axis=-1, keepdims=True)
    return xc * lax.rsqrt(var + LN_EPS) * g + b


def _rms_norm(x, g):
    return x * lax.rsqrt(jnp.mean(x * x, axis=-1, keepdims=True) + RMS_EPS) * g


N_HG_TILES = 4
W_MLA_COLS = Q_LORA + KV_LORA + 2 * ROPE_PAD


def _in_proj_kernel(x_ref, g_ref, b_ref, wh_ref, wm_ref, qg_ref, kvg_ref, cos_ref, sin_ref,
                    hg_ref, cq_ref, lat_ref, kr_ref, hb_ref):
    j = pl.program_id(1)

    @pl.when(j == 0)
    def _():
        for r0 in range(0, x_ref.shape[0], ROW_SUB):
            rows = slice(r0, r0 + ROW_SUB)
            hb = _layer_norm(x_ref[rows, :], g_ref[...], b_ref[...]).astype(BF16)
            hb_ref[rows, :] = hb
            hg_ref[rows, :] = jnp.dot(hb, wh_ref[...], preferred_element_type=F32)

    @pl.when(jnp.logical_and(j > 0, j < N_HG_TILES))
    def _():
        hg_ref[...] = jnp.dot(hb_ref[...], wh_ref[...], preferred_element_type=F32)

    @pl.when(j == N_HG_TILES)
    def _():
        p = jnp.dot(hb_ref[...], wm_ref[...], preferred_element_type=F32)
        cq = p[:, :Q_LORA]
        ckv = p[:, Q_LORA:Q_LORA + KV_LORA]
        kr = p[:, Q_LORA + KV_LORA:Q_LORA + KV_LORA + ROPE_PAD]
        kr_sw = p[:, Q_LORA + KV_LORA + ROPE_PAD:]
        cq_ref[...] = _rms_norm(cq, qg_ref[...]).astype(BF16)
        lat_ref[...] = _rms_norm(ckv, kvg_ref[...])
        kr_ref[...] = kr * cos_ref[...] + kr_sw * sin_ref[...]


def _in_proj(x, ln_g, ln_b, w_h, w_m, q_a_g, kv_a_g, cos_k, sin_k, tm):
    m = x.shape[0]
    n_pos_tiles = cos_k.shape[0] // tm
    grid = (m // tm, N_HG_TILES + 1)
    row = lambda i, j: (i, 0)
    const = lambda i, j: (0, 0)
    pos = lambda i, j: (i % n_pos_tiles, 0)
    hg_col = lambda i, j: (i, jnp.minimum(j, N_HG_TILES - 1))
    return pl.pallas_call(
        _in_proj_kernel,
        grid=grid,
        in_specs=[
            pl.BlockSpec((tm, D_MODEL), row),
            pl.BlockSpec((1, D_MODEL), const),
            pl.BlockSpec((1, D_MODEL), const),
            pl.BlockSpec((D_MODEL, D_HGRN), lambda i, j: (0, jnp.minimum(j, N_HG_TILES - 1))),
            pl.BlockSpec((D_MODEL, W_MLA_COLS), const),
            pl.BlockSpec((1, Q_LORA), const),
            pl.BlockSpec((1, KV_LORA), const),
            pl.BlockSpec((tm, ROPE_PAD), pos),
            pl.BlockSpec((tm, ROPE_PAD), pos),
        ],
        out_specs=[
            pl.BlockSpec((tm, D_HGRN), hg_col),
            pl.BlockSpec((tm, Q_LORA), row),
            pl.BlockSpec((tm, KV_LORA), row),
            pl.BlockSpec((tm, ROPE_PAD), row),
        ],
        out_shape=[
            jax.ShapeDtypeStruct((m, N_HG_TILES * D_HGRN), F32),
            jax.ShapeDtypeStruct((m, Q_LORA), BF16),
            jax.ShapeDtypeStruct((m, KV_LORA), F32),
            jax.ShapeDtypeStruct((m, ROPE_PAD), F32),
        ],
        scratch_shapes=[pltpu.VMEM((tm, D_MODEL), BF16)],
        compiler_params=_cparams(("arbitrary", "arbitrary")),
        name="in_proj",
    )(x, ln_g, ln_b, w_h, w_m, q_a_g, kv_a_g, cos_k, sin_k)


def _cumsum_rows(x):
    c = x.shape[0]
    row = lax.broadcasted_iota(jnp.int32, (c, 1), 0)
    shift = 1
    while shift < c:
        x = x + jnp.where(row >= shift, pltpu.roll(x, shift, 0), 0.0)
        shift *= 2
    return x


def _split_level(c):
    t = lax.broadcasted_iota(jnp.int32, (c, c), 0)
    s = lax.broadcasted_iota(jnp.int32, (c, c), 1)
    x = jnp.bitwise_xor(t, s)
    lvl = jnp.full((c, c), -1, jnp.int32)
    for bit in range(c.bit_length() - 1):
        lvl = jnp.where(x >= (1 << bit), bit, lvl)
    return jnp.where(t < s, -2, lvl)


def _hgrn_head_block(q, fpre, v, lb, st, lvl):
    c = q.shape[0]
    f = lb + (1.0 - lb) * jax.nn.sigmoid(fpre)
    g = jnp.log(f)
    k = 1.0 - f
    b = _cumsum_rows(g)
    row = lax.broadcasted_iota(jnp.int32, (c, 1), 0)

    a = jnp.where(lvl == -1, jnp.sum(q * k, axis=-1, keepdims=True), 0.0)
    for level in range(c.bit_length() - 1):
        half = 1 << level
        pos = row % (2 * half)
        is_query = pos >= half
        if level == 0:
            decay = jnp.where(is_query, f, 1.0)
        else:
            if 2 * half < 8:
                ref = b
                for p in range(2 * half):
                    d = half - 1 - p
                    if d != 0:
                        ref = jnp.where(pos == p, pltpu.roll(b, (-d) % c, 0), ref)
            else:
                ref = jnp.concatenate(
                    [jnp.broadcast_to(b[j + half - 1:j + half], (2 * half, b.shape[1]))
                     for j in range(0, c, 2 * half)], axis=0)
            decay = jnp.exp(-jnp.abs(b - ref))
        z = (jnp.where(is_query, q, k) * decay).astype(BF16)
        gram = lax.dot_general(z, z, NT_DIMS, preferred_element_type=F32)
        a = jnp.where(lvl == level, gram, a)

    vb = v.astype(BF16)
    o = jnp.dot(a.astype(BF16), vb, preferred_element_type=F32)
    o = o + lax.dot_general((q * jnp.exp(b)).astype(BF16), st.astype(BF16), NT_DIMS,
                            preferred_element_type=F32)
    b_last = b[c - 1:c]
    k_dec = (k * jnp.exp(b_last - b)).astype(BF16)
    st_new = jnp.exp(b_last) * st + lax.dot_general(vb, k_dec, TN_DIMS, preferred_element_type=F32)
    return o, st_new


def _hgrn_kernel(*refs, c, has_s0):
    if has_s0:
        q_ref, f_ref, i_ref, g_ref, lb_ref, ng_ref, s0_ref, o_ref, s_ref, st_ref = refs
    else:
        q_ref, f_ref, i_ref, g_ref, lb_ref, ng_ref, o_ref, s_ref, st_ref = refs
        s0_ref = None
    t = pl.program_id(1)

    @pl.when(t == 0)
    def _():
        for h in range(HGRN_HEADS):
            if has_s0:
                st_ref[h] = s0_ref[0, h].T
            else:
                st_ref[h] = jnp.zeros((HGRN_DV, HGRN_DK), F32)

    lvl = _split_level(c)

    def block(i, _):
        rows = pl.ds(pl.multiple_of(i * c, c), c)
        for h in range(HGRN_HEADS):
            cols = slice(h * HGRN_DK, (h + 1) * HGRN_DK)
            o, st_new = _hgrn_head_block(q_ref[rows, cols], f_ref[rows, cols], i_ref[rows, cols],
                                         lb_ref[:, cols], st_ref[h], lvl)
            st_ref[h] = st_new
            o = _rms_norm(o, ng_ref[:, cols]) * jax.nn.silu(g_ref[rows, cols])
            o_ref[rows, cols] = o.astype(BF16)
        return 0

    lax.fori_loop(0, q_ref.shape[0] // c, block, 0)

    @pl.when(t == pl.num_programs(1) - 1)
    def _():
        for h in range(HGRN_HEADS):
            s_ref[0, h] = st_ref[h].T


def _hgrn(hg, lb, norm_g, s0, batch, seq, c, rows):
    nt = seq // rows
    has_s0 = s0 is not None
    col = lambda n: (lambda b, t: (b * nt + t, n))
    const = lambda b, t: (0, 0)
    state = lambda b, t: (b, 0, 0, 0)
    in_specs = [pl.BlockSpec((rows, D_HGRN), col(n)) for n in range(4)]
    in_specs += [pl.BlockSpec((1, D_HGRN), const), pl.BlockSpec((1, D_HGRN), const)]
    args = [hg, hg, hg, hg, lb, norm_g]
    if has_s0:
        in_specs.append(pl.BlockSpec((1, HGRN_HEADS, HGRN_DK, HGRN_DV), state))
        args.append(s0)
    return pl.pallas_call(
        functools.partial(_hgrn_kernel, c=c, has_s0=has_s0),
        grid=(batch, nt),
        in_specs=in_specs,
        out_specs=[
            pl.BlockSpec((rows, D_HGRN), lambda b, t: (b * nt + t, 0)),
            pl.BlockSpec((1, HGRN_HEADS, HGRN_DK, HGRN_DV), state),
        ],
        out_shape=[
            jax.ShapeDtypeStruct((batch * seq, D_HGRN), BF16),
            jax.ShapeDtypeStruct((batch, HGRN_HEADS, HGRN_DK, HGRN_DV), F32),
        ],
        scratch_shapes=[pltpu.VMEM((HGRN_HEADS, HGRN_DV, HGRN_DK), F32)],
        compiler_params=_cparams(("arbitrary", "arbitrary")),
        name="hgrn",
    )(*args)


def _q_proj_kernel(cq_ref, wq_ref, wsw_ref, cos_ref, sin_ref, q_ref, *, scale):
    cq = cq_ref[...]
    q_lin = jnp.dot(cq, wq_ref[...], preferred_element_type=F32)
    q_sw = jnp.dot(cq, wsw_ref[...], preferred_element_type=F32)
    cos = cos_ref[...]
    sin = sin_ref[...]
    for h in range(MLA_HEADS):
        base = h * QK_PAD
        q_ref[:, base:base + QK_NOPE] = (q_lin[:, base:base + QK_NOPE] * scale).astype(BF16)
        rot = (q_lin[:, base + QK_NOPE:base + QK_PAD] * cos
               + q_sw[:, h * ROPE_PAD:(h + 1) * ROPE_PAD] * sin)
        q_ref[:, base + QK_NOPE:base + QK_PAD] = (rot * scale).astype(BF16)


def _q_proj(cq, w_q, w_q_sw, cos_k, sin_k, tm):
    m = cq.shape[0]
    n_pos_tiles = cos_k.shape[0] // tm
    scale = LOG2_E * (QK_NOPE + QK_ROPE) ** -0.5
    row = lambda i: (i, 0)
    const = lambda i: (0, 0)
    pos = lambda i: (i % n_pos_tiles, 0)
    return pl.pallas_call(
        functools.partial(_q_proj_kernel, scale=scale),
        grid=(m // tm,),
        in_specs=[
            pl.BlockSpec((tm, Q_LORA), row),
            pl.BlockSpec((Q_LORA, MLA_HEADS * QK_PAD), const),
            pl.BlockSpec((Q_LORA, MLA_HEADS * ROPE_PAD), const),
            pl.BlockSpec((tm, ROPE_PAD), pos),
            pl.BlockSpec((tm, ROPE_PAD), pos),
        ],
        out_specs=pl.BlockSpec((tm, MLA_HEADS * QK_PAD), row),
        out_shape=jax.ShapeDtypeStruct((m, MLA_HEADS * QK_PAD), BF16),
        compiler_params=_cparams(("arbitrary",)),
        name="q_proj",
    )(cq, w_q, w_q_sw, cos_k, sin_k)


def _kv_proj_kernel(lat_ref, kr_ref, wk_ref, wv_ref, k_ref, v_ref):
    lat = lat_ref[...].astype(BF16)
    kn = jnp.dot(lat, wk_ref[...], preferred_element_type=F32)
    v_ref[...] = jnp.dot(lat, wv_ref[...], preferred_element_type=F32).astype(BF16)
    kr = kr_ref[...].astype(BF16)
    for h in range(MLA_HEADS):
        base = h * QK_PAD
        k_ref[:, base:base + QK_NOPE] = kn[:, h * QK_NOPE:(h + 1) * QK_NOPE].astype(BF16)
        k_ref[:, base + QK_NOPE:base + QK_PAD] = kr


def _kv_proj(lat, kr_pad, w_k, w_v, tm):
    m = lat.shape[0]
    row = lambda i: (i, 0)
    const = lambda i: (0, 0)
    return pl.pallas_call(
        _kv_proj_kernel,
        grid=(m // tm,),
        in_specs=[
            pl.BlockSpec((tm, KV_LORA), row),
            pl.BlockSpec((tm, ROPE_PAD), row),
            pl.BlockSpec((KV_LORA, MLA_HEADS * QK_NOPE), const),
            pl.BlockSpec((KV_LORA, D_MLA), const),
        ],
        out_specs=[
            pl.BlockSpec((tm, MLA_HEADS * QK_PAD), row),
            pl.BlockSpec((tm, D_MLA), row),
        ],
        out_shape=[
            jax.ShapeDtypeStruct((m, MLA_HEADS * QK_PAD), BF16),
            jax.ShapeDtypeStruct((m, D_MLA), BF16),
        ],
        compiler_params=_cparams(("arbitrary",)),
        name="kv_proj",
    )(lat, kr_pad, w_k, w_v)


def _attn_kernel(tile_q_ref, tile_k_ref, q_ref, k_ref, v_ref, o_ref,
                 vx_ref, rel_ref, sa_ref, sb_ref, m_ref, acc_ref, *, tq, tk, past, n_keys, n_plain, n_tiles):
    @pl.when(jnp.logical_and(pl.program_id(0) == 0, pl.program_id(1) == 0))
    def _():
        rel_ref[...] = (lax.broadcasted_iota(jnp.int32, (tq, tk), 1) // CHUNK
                        - lax.broadcasted_iota(jnp.int32, (tq, tk), 0) // CHUNK)

    vx_ref[:, :V_HEAD] = v_ref[0]
    vx_ref[:, V_HEAD:] = jnp.ones((vx_ref.shape[0], V_HEAD), BF16)
    m_ref[...] = jnp.full(m_ref.shape, NEG_BIG, F32)
    acc_ref[...] = jnp.zeros(acc_ref.shape, F32)
    k_lane = lax.broadcasted_iota(jnp.int32, (1, tk), 1)

    def origin(t):
        return pl.multiple_of(tile_q_ref[t] * tq, tq), pl.multiple_of(tile_k_ref[t] * tk, tk)

    def scores(t, s_ref):
        qs, ks = origin(t)
        s_ref[...] = lax.dot_general(q_ref[0, pl.ds(qs, tq), :], k_ref[0, pl.ds(ks, tk), :], NT_DIMS,
                                     preferred_element_type=F32)

    def softmax_pv(t, s_ref, masked):
        qs, ks = origin(t)
        s = s_ref[...]
        if masked:
            pad = jnp.where(k_lane < n_keys - ks, 0, 1 << 20)
            visible = rel_ref[...] + pad <= (past + qs - ks) // CHUNK
            s = jnp.where(visible, s, NEG_BIG)
        rows = pl.ds(qs, tq)
        m = m_ref[rows, :]
        m_new = jnp.maximum(m, jnp.max(s, axis=-1, keepdims=True))
        p = jnp.exp2(s - m_new).astype(BF16)
        pv = jnp.dot(p, vx_ref[pl.ds(ks, tk), :], preferred_element_type=F32)
        acc_ref[rows, :] = jnp.exp2(m - m_new) * acc_ref[rows, :] + pv
        m_ref[rows, :] = m_new

    bufs = (sa_ref, sb_ref)

    def step(t, parity, masked):
        scores(t + 1, bufs[1 - parity])
        softmax_pv(t, bufs[parity], masked)

    def run(t0, t1, masked):
        t = t0
        if t < t1 and t % 2 == 1:
            step(t, 1, masked)
            t += 1
        trips = (t1 - t) // ATTN_UNROLL
        if trips > 0:
            def body(i, _, base=t):
                tt = base + ATTN_UNROLL * i
                for u in range(ATTN_UNROLL):
                    step(tt + u, u % 2, masked)
                return 0
            lax.fori_loop(0, trips, body, 0)
            t += ATTN_UNROLL * trips
        while t < t1:
            step(t, t % 2, masked)
            t += 1

    scores(0, sa_ref)
    run(0, min(n_plain, n_tiles - 1), False)
    run(min(n_plain, n_tiles - 1), n_tiles - 1, True)
    last = n_tiles - 1
    softmax_pv(last, bufs[last % 2], last >= n_plain)
    acc = acc_ref[...]
    o_ref[0] = (acc[:, :V_HEAD] / acc[:, V_HEAD:]).astype(BF16)


def _attention_tiles(seq, n_keys_pad, n_keys, past, tq, tk):
    plain, masked = [], []
    for j in range(n_keys_pad // tk):
        for i in range(seq // tq):
            q_lo, k_lo = past + i * tq, j * tk
            q_hi, k_hi = q_lo + tq - 1, k_lo + tk - 1
            if k_lo // CHUNK > q_hi // CHUNK or k_lo >= n_keys:
                continue
            if k_hi // CHUNK <= q_lo // CHUNK and k_hi < n_keys:
                plain.append((i, j))
            else:
                masked.append((i, j))
    return plain, masked


def _attention(q, k, v, batch, seq, n_keys_pad, n_keys, past, tq, tk):
    assert past % CHUNK == 0 and tk % CHUNK == 0 and (tq % CHUNK == 0 or tq == seq)
    plain, masked = _attention_tiles(seq, n_keys_pad, n_keys, past, tq, tk)
    tiles = plain + masked
    tile_q = jnp.asarray([t[0] for t in tiles], jnp.int32)
    tile_k = jnp.asarray([t[1] for t in tiles], jnp.int32)
    smem = pl.BlockSpec(memory_space=pltpu.SMEM)
    return pl.pallas_call(
        functools.partial(_attn_kernel, tq=tq, tk=tk, past=past, n_keys=n_keys,
                          n_plain=len(plain), n_tiles=len(tiles)),
        grid=(batch, MLA_HEADS),
        in_specs=[
            smem, smem,
            pl.BlockSpec((1, seq, QK_PAD), lambda b, h: (b, 0, h)),
            pl.BlockSpec((1, n_keys_pad, QK_PAD), lambda b, h: (b, 0, h)),
            pl.BlockSpec((1, n_keys_pad, V_HEAD), lambda b, h: (b, 0, h)),
        ],
        out_specs=pl.BlockSpec((1, seq, V_HEAD), lambda b, h: (b, 0, h)),
        out_shape=jax.ShapeDtypeStruct((batch, seq, D_MLA), BF16),
        scratch_shapes=[
            pltpu.VMEM((n_keys_pad, 2 * V_HEAD), BF16),
            pltpu.VMEM((tq, tk), jnp.int32),
            pltpu.VMEM((tq, tk), F32),
            pltpu.VMEM((tq, tk), F32),
            pltpu.VMEM((seq, 1), F32),
            pltpu.VMEM((seq, 2 * V_HEAD), F32),
        ],
        compiler_params=_cparams(("arbitrary", "arbitrary")),
        name="attn",
    )(tile_q, tile_k, q, k, v)


LAT_KEY = KV_LORA + ROPE_PAD


def _absorbed_attn_kernel(q_ref, kx_ref, wuk_ref, wuv_ref, o_ref, qx_ref, vx_ref, *, seq, past, n_keys):
    for h in range(MLA_HEADS):
        rows = slice(h * seq, (h + 1) * seq)
        q_lat = jnp.dot(q_ref[:, h * QK_PAD:h * QK_PAD + QK_NOPE], wuk_ref[h], preferred_element_type=F32)
        qx_ref[rows, :KV_LORA] = q_lat.astype(BF16)
        qx_ref[rows, KV_LORA:] = q_ref[:, h * QK_PAD + QK_NOPE:(h + 1) * QK_PAD]
    kx = kx_ref[0]
    s = lax.dot_general(qx_ref[...], kx, NT_DIMS, preferred_element_type=F32)
    q_chunk = (past + lax.broadcasted_iota(jnp.int32, (s.shape[0], 1), 0) % seq) // CHUNK
    k_idx = lax.broadcasted_iota(jnp.int32, (1, s.shape[1]), 1)
    visible = jnp.logical_and(k_idx // CHUNK <= q_chunk, k_idx < n_keys)
    s = jnp.where(visible, s, NEG_BIG)
    p = jnp.exp2(s - jnp.max(s, axis=-1, keepdims=True)).astype(BF16)
    vx_ref[:, :KV_LORA] = kx[:, :KV_LORA]
    vx_ref[:, KV_LORA:] = jnp.ones((vx_ref.shape[0], ROPE_PAD), BF16)
    o_ext = jnp.dot(p, vx_ref[...], preferred_element_type=F32)
    denom = o_ext[:, KV_LORA:]
    o_lat = (o_ext[:, :KV_LORA] / jnp.concatenate([denom] * (KV_LORA // ROPE_PAD), axis=1)).astype(BF16)
    for h in range(MLA_HEADS):
        o_ref[:, h * V_HEAD:(h + 1) * V_HEAD] = jnp.dot(
            o_lat[h * seq:(h + 1) * seq], wuv_ref[h], preferred_element_type=F32).astype(BF16)


def _absorbed_attention(q, kx, w_uk_t, w_uv, batch, seq, n_keys, past):
    n_keys_pad = kx.shape[1]
    const3 = lambda b: (0, 0, 0)
    return pl.pallas_call(
        functools.partial(_absorbed_attn_kernel, seq=seq, past=past, n_keys=n_keys),
        grid=(batch,),
        in_specs=[
            pl.BlockSpec((seq, MLA_HEADS * QK_PAD), lambda b: (b, 0)),
            pl.BlockSpec((1, n_keys_pad, LAT_KEY), lambda b: (b, 0, 0)),
            pl.BlockSpec((MLA_HEADS, QK_NOPE, KV_LORA), const3),
            pl.BlockSpec((MLA_HEADS, KV_LORA, V_HEAD), const3),
        ],
        out_specs=pl.BlockSpec((seq, D_MLA), lambda b: (b, 0)),
        out_shape=jax.ShapeDtypeStruct((batch * seq, D_MLA), BF16),
        scratch_shapes=[
            pltpu.VMEM((MLA_HEADS * seq, LAT_KEY), BF16),
            pltpu.VMEM((n_keys_pad, LAT_KEY), BF16),
        ],
        compiler_params=_cparams(("arbitrary",)),
        name="absorbed_attn",
    )(q, kx, w_uk_t, w_uv)


def _out_proj_kernel(oh_ref, oa_ref, x_ref, g0_ref, b0_ref, w_ref, g1_ref, b1_ref, y_ref):
    for r0 in range(0, x_ref.shape[0], ROW_SUB):
        rows = slice(r0, r0 + ROW_SUB)
        mix = jnp.dot(oh_ref[rows, :], w_ref[:D_HGRN, :], preferred_element_type=F32)
        mix = mix + jnp.dot(oa_ref[rows, :], w_ref[D_HGRN:, :], preferred_element_type=F32)
        h = _layer_norm(x_ref[rows, :], g0_ref[...], b0_ref[...])
        y_ref[rows, :] = _layer_norm(ALPHA * h + mix, g1_ref[...], b1_ref[...])


def _out_proj(oh, oa, x, ln_in_g, ln_in_b, w_out, ln1_g, ln1_b, tm):
    m = x.shape[0]
    row = lambda i: (i, 0)
    const = lambda i: (0, 0)
    vec = pl.BlockSpec((1, D_MODEL), const)
    return pl.pallas_call(
        _out_proj_kernel,
        grid=(m // tm,),
        in_specs=[
            pl.BlockSpec((tm, D_HGRN), row),
            pl.BlockSpec((tm, D_MLA), row),
            pl.BlockSpec((tm, D_MODEL), row),
            vec, vec,
            pl.BlockSpec((D_HGRN + D_MLA, D_MODEL), const),
            vec, vec,
        ],
        out_specs=pl.BlockSpec((tm, D_MODEL), row),
        out_shape=jax.ShapeDtypeStruct((m, D_MODEL), F32),
        compiler_params=_cparams(("arbitrary",)),
        name="out_proj",
    )(oh, oa, x, ln_in_g, ln_in_b, w_out, ln1_g, ln1_b)


def _ffn_kernel(x_ref, wu_ref, wg_ref, wd_ref, cw_ref, cb_ref, buf_ref, g2_ref, b2_ref,
                y_ref, cnew_ref, xb_ref, carry_ref, *, spt, tiles_per_seq):
    i = pl.program_id(0)
    f = pl.program_id(1)
    tm = x_ref.shape[0]
    tf = wu_ref.shape[1]
    rows = tm // spt

    @pl.when(f == 0)
    def _():
        xb_ref[...] = x_ref[...].astype(BF16)
        y_ref[...] = jnp.zeros(y_ref.shape, F32)

    if tiles_per_seq > 1:
        @pl.when(i % tiles_per_seq == 0)
        def _():
            carry_ref[f] = buf_ref[...]
        halo_ref = carry_ref.at[f]
    else:
        halo_ref = buf_ref

    xb = xb_ref[...]
    r = lax.broadcasted_iota(jnp.int32, (spt, rows, FFN_SUB), 1)
    for c0 in range(0, tf, FFN_SUB):
        cs = slice(c0, c0 + FFN_SUB)
        u = jnp.dot(xb, wu_ref[:, cs], preferred_element_type=F32).reshape(spt, rows, FFN_SUB)
        gate = jnp.dot(xb, wg_ref[:, cs], preferred_element_type=F32).reshape(spt, rows, FFN_SUB)
        halo = halo_ref[:, :, cs]
        prev1 = jnp.where(r == 0, halo[:, 1:2, :], pltpu.roll(u, 1, 1))
        prev2 = jnp.where(r == 0, halo[:, 0:1, :],
                          jnp.where(r == 1, halo[:, 1:2, :], pltpu.roll(u, 2, 1)))
        cw = cw_ref[:, cs]
        a = cb_ref[:, cs] + (prev2 * cw[0:1] + prev1 * cw[1:2] + u * cw[2:3])
        hid = (jax.nn.silu(a) * gate).astype(BF16).reshape(tm, FFN_SUB)
        y_ref[...] += jnp.dot(hid, wd_ref[cs, :], preferred_element_type=F32)
        tail = u[:, rows - (CONV_W - 1):, :]
        cnew_ref[:, :, cs] = tail
        if tiles_per_seq > 1:
            carry_ref[f, :, :, cs] = tail

    @pl.when(f == pl.num_programs(1) - 1)
    def _():
        y_ref[...] = _layer_norm(ALPHA * x_ref[...] + y_ref[...], g2_ref[...], b2_ref[...])


def _ffn(x, w_up, w_gate, w_down, conv_w, conv_b, conv_buf, ln2_g, ln2_b, seq, tm):
    m = x.shape[0]
    tf = w_up.shape[2]
    n_seq = m // seq
    if tm >= seq:
        spt, tiles_per_seq = tm // seq, 1
        seq_of = lambda i: i
    else:
        spt, tiles_per_seq = 1, seq // tm
        seq_of = lambda i: i // tiles_per_seq
    n_f = D_FF // tf
    row = lambda i, f: (i, 0)
    const = lambda i, f: (0, 0)
    y, tails = pl.pallas_call(
        functools.partial(_ffn_kernel, spt=spt, tiles_per_seq=tiles_per_seq),
        grid=(m // tm, n_f),
        in_specs=[
            pl.BlockSpec((tm, D_MODEL), row, pipeline_mode=pl.Buffered(1)),
            pl.BlockSpec((None, D_MODEL, tf), lambda i, f: (f, 0, 0)),
            pl.BlockSpec((None, D_MODEL, tf), lambda i, f: (f, 0, 0)),
            pl.BlockSpec((tf, D_MODEL), lambda i, f: (f, 0)),
            pl.BlockSpec((CONV_W, tf), lambda i, f: (0, f)),
            pl.BlockSpec((1, tf), lambda i, f: (0, f)),
            pl.BlockSpec((spt, CONV_W - 1, tf), lambda i, f: (seq_of(i), 0, f)),
            pl.BlockSpec((1, D_MODEL), const),
            pl.BlockSpec((1, D_MODEL), const),
        ],
        out_specs=[
            pl.BlockSpec((tm, D_MODEL), row),
            pl.BlockSpec((spt, CONV_W - 1, tf), lambda i, f: (i, 0, f)),
        ],
        out_shape=[
            jax.ShapeDtypeStruct((m, D_MODEL), F32),
            jax.ShapeDtypeStruct((n_seq * tiles_per_seq, CONV_W - 1, D_FF), F32),
        ],
        scratch_shapes=[
            pltpu.VMEM((tm, D_MODEL), BF16),
            pltpu.VMEM((n_f, spt, CONV_W - 1, tf), F32),
        ],
        compiler_params=_cparams(("arbitrary", "arbitrary")),
        name="ffn",
    )(x, w_up, w_gate, w_down, conv_w, conv_b, conv_buf, ln2_g, ln2_b)
    return y, tails[tiles_per_seq - 1::tiles_per_seq]


def _rope_tables(past, seq, rows):
    inv = ROPE_THETA ** (-jnp.arange(0, QK_ROPE, 2, dtype=F32) / QK_ROPE)
    ang = (past + jnp.arange(seq)).astype(F32)[:, None] * inv[None, :]
    cos, sin = jnp.cos(ang), jnp.sin(ang)
    zero = jnp.zeros((seq, ROPE_PAD - QK_ROPE), F32)
    cos_t = jnp.concatenate([cos, cos, zero], axis=1)
    sin_t = jnp.concatenate([-sin, sin, zero], axis=1)
    reps = rows // seq
    return jnp.tile(cos_t, (reps, 1)), jnp.tile(sin_t, (reps, 1))


def _swap_halves(w):
    half = QK_ROPE // 2
    return jnp.concatenate([w[..., half:], w[..., :half]], axis=-1)


def _col_tiles(w):
    return w.reshape(D_MODEL, D_FF // FFN_COL_TILE, FFN_COL_TILE).transpose(1, 0, 2).astype(BF16)


def _prep_weights(w_in, w_q_b, w_kv_b, w_out, w_ffn_up, w_ffn_gate, w_ffn_down):
    w_in = w_in[0]
    c0 = 4 * D_HGRN
    w_h = w_in[:, :c0].astype(BF16)
    kr_cols = w_in[:, c0 + Q_LORA + KV_LORA:]
    zpad = jnp.zeros((D_MODEL, ROPE_PAD - QK_ROPE), F32)
    w_m = jnp.concatenate([w_in[:, c0:c0 + Q_LORA + KV_LORA], kr_cols, zpad,
                           _swap_halves(kr_cols), zpad], axis=1).astype(BF16)

    wq = w_q_b[0].reshape(Q_LORA, MLA_HEADS, QK_NOPE + QK_ROPE)
    zq = jnp.zeros((Q_LORA, MLA_HEADS, QK_PAD - QK_NOPE - QK_ROPE), F32)
    w_q = jnp.concatenate([wq, zq], axis=-1).reshape(Q_LORA, MLA_HEADS * QK_PAD).astype(BF16)
    zr = jnp.zeros((Q_LORA, MLA_HEADS, ROPE_PAD - QK_ROPE), F32)
    w_q_sw = jnp.concatenate([_swap_halves(wq[..., QK_NOPE:]), zr], axis=-1)
    w_q_sw = w_q_sw.reshape(Q_LORA, MLA_HEADS * ROPE_PAD).astype(BF16)

    wkv = w_kv_b[0].reshape(KV_LORA, MLA_HEADS, QK_NOPE + V_HEAD)
    w_k = wkv[..., :QK_NOPE].reshape(KV_LORA, MLA_HEADS * QK_NOPE).astype(BF16)
    w_v = wkv[..., QK_NOPE:].reshape(KV_LORA, D_MLA).astype(BF16)
    w_uk_t = wkv[..., :QK_NOPE].transpose(1, 2, 0).astype(BF16)
    w_uv = wkv[..., QK_NOPE:].transpose(1, 0, 2).astype(BF16)
    return dict(w_h=w_h, w_m=w_m, w_q=w_q, w_q_sw=w_q_sw, w_k=w_k, w_v=w_v, w_uk_t=w_uk_t, w_uv=w_uv,
                w_out=w_out[0].astype(BF16), w_up=_col_tiles(w_ffn_up[0]),
                w_gate=_col_tiles(w_ffn_gate[0]), w_down=w_ffn_down[0].astype(BF16))


def _row(a):
    return a.reshape(1, -1)


def _tile_plan(batch, seq, n_keys_pad):
    m = batch * seq
    tm = min(ROW_TILE, m)
    tm_big = min(BIG_ROW_TILE, m)
    assert m % tm_big == 0 and (tm_big % seq == 0 or seq % tm_big == 0)
    return dict(
        tm=tm,
        tm_in=tm_big,
        tm_ffn=tm_big,
        rows_hgrn=min(HGRN_STEP_ROWS, seq),
        tq=min(ATTN_TILE, seq),
        tk=ATTN_TILE if n_keys_pad % ATTN_TILE == 0 else n_keys_pad,
        tm_kv=ROW_TILE if n_keys_pad % ROW_TILE == 0 else n_keys_pad,
    )


def _encoder(x, s0, lat_past, kr_past, conv_buf, p, w, *, blk):
    batch, seq, _ = x.shape
    past = 0 if lat_past is None else lat_past.shape[1]
    m = batch * seq
    n_keys = past + seq
    key_pad = -n_keys % KEY_ALIGN
    n_keys_pad = n_keys + key_pad
    t = _tile_plan(batch, seq, n_keys_pad)
    x2 = x.reshape(m, D_MODEL)
    cos_k, sin_k = _rope_tables(past, seq, max(seq, t["tm_in"]))

    hg, cqn, lat_new, kr_new = _in_proj(x2, p["ln_in_g"], p["ln_in_b"], w["w_h"], w["w_m"],
                                        p["q_a_g"], p["kv_a_g"], cos_k, sin_k, t["tm_in"])
    o_h, s_new = _hgrn(hg, p["lb"], p["hgrn_norm_g"], s0, batch, seq, blk, t["rows_hgrn"])

    q = _q_proj(cqn, w["w_q"], w["w_q_sw"], cos_k, sin_k, t["tm"])
    lat3 = lat_new.reshape(batch, seq, KV_LORA)
    kr3 = kr_new.reshape(batch, seq, ROPE_PAD)
    if past or key_pad:
        parts_l, parts_r = [], []
        if past:
            parts_l.append(lat_past)
            parts_r.append(jnp.pad(kr_past, ((0, 0), (0, 0), (0, ROPE_PAD - QK_ROPE))))
        parts_l.append(lat3)
        parts_r.append(kr3)
        if key_pad:
            parts_l.append(jnp.zeros((batch, key_pad, KV_LORA), F32))
            parts_r.append(jnp.zeros((batch, key_pad, ROPE_PAD), F32))
        lat_all = jnp.concatenate(parts_l, axis=1)
        kr_all = jnp.concatenate(parts_r, axis=1)
    else:
        lat_all, kr_all = lat3, kr3
    if seq * MLA_HEADS <= ABSORB_MAX_ROWS:
        kx = jnp.concatenate([lat_all, kr_all], axis=-1).astype(BF16)
        o_a = _absorbed_attention(q, kx, w["w_uk_t"], w["w_uv"], batch, seq, n_keys, past)
    else:
        k_cat, v = _kv_proj(lat_all.reshape(batch * n_keys_pad, KV_LORA),
                            kr_all.reshape(batch * n_keys_pad, ROPE_PAD), w["w_k"], w["w_v"], t["tm_kv"])
        o_a = _attention(q.reshape(batch, seq, MLA_HEADS * QK_PAD),
                         k_cat.reshape(batch, n_keys_pad, MLA_HEADS * QK_PAD),
                         v.reshape(batch, n_keys_pad, D_MLA),
                         batch, seq, n_keys_pad, n_keys, past, t["tq"], t["tk"]).reshape(m, D_MLA)

    h1 = _out_proj(o_h, o_a, x2, p["ln_in_g"], p["ln_in_b"], w["w_out"],
                   p["ln1_g"], p["ln1_b"], t["tm"])
    y, conv_new = _ffn(h1, w["w_up"], w["w_gate"], w["w_down"], p["conv_w"], p["conv_b"],
                       conv_buf, p["ln2_g"], p["ln2_b"], seq, t["tm_ffn"])
    return (y.reshape(batch, seq, D_MODEL), s_new[None], lat3[None],
            kr3[:, :, :QK_ROPE][None], conv_new[None])


def kernel(x_prompt, x_sample, cache_kv_latent, cache_k_rope, state_hgrn, cache_ffn_conv, lb_param, ln_in_g, ln_in_b, w_in, hgrn_norm_g, q_a_g, w_q_b, kv_a_g, w_kv_b, w_out, ln1_g, ln1_b, w_ffn_up, w_ffn_gate, conv_w, conv_b, w_ffn_down, ln2_g, ln2_b):
    w = _prep_weights(w_in, w_q_b, w_kv_b, w_out, w_ffn_up, w_ffn_gate, w_ffn_down)
    lbs = jnp.cumsum(jax.nn.softmax(lb_param.astype(F32), axis=0), axis=0)
    p = dict(ln_in_g=_row(ln_in_g), ln_in_b=_row(ln_in_b), lb=_row(lbs[0]),
             hgrn_norm_g=_row(hgrn_norm_g[0]), q_a_g=_row(q_a_g[0]), kv_a_g=_row(kv_a_g[0]),
             ln1_g=_row(ln1_g[0]), ln1_b=_row(ln1_b[0]), conv_w=conv_w[0], conv_b=_row(conv_b[0]),
             ln2_g=_row(ln2_g[0]), ln2_b=_row(ln2_b[0]))

    n_prompt = x_prompt.shape[0]
    conv0 = jnp.zeros((n_prompt, CONV_W - 1, D_FF), F32)
    y_p, s_p, lat_p, kr_p, conv_p = _encoder(x_prompt, None, None, None, conv0, p, w, blk=CHUNK)
    y_s, s_s, lat_s, kr_s, conv_s = _encoder(
        x_sample, state_hgrn[0], cache_kv_latent[0], cache_k_rope[0], cache_ffn_conv[0], p, w,
        blk=x_sample.shape[1])
    return (y_p, y_s, lat_p, kr_p, s_p, conv_p, lat_s, kr_s, s_s, conv_s)
```

```python
import functools

import jax
import jax.numpy as jnp
from jax import lax
from jax.experimental import pallas as pl
from jax.experimental.pallas import tpu as pltpu

F32 = jnp.float32
BF16 = jnp.bfloat16

D_MODEL = 2048
CHUNK = 64
HGRN_HEADS = 8
HGRN_DK = 128
HGRN_DV = 128
D_HGRN = HGRN_HEADS * HGRN_DK
MLA_HEADS = 8
Q_LORA = 512
KV_LORA = 256
QK_NOPE = 128
QK_ROPE = 64
V_HEAD = 128
D_MLA = MLA_HEADS * V_HEAD
D_FF = 5632
CONV_W = 3
ROPE_THETA = 10000.0
LN_EPS = 1e-5
RMS_EPS = 1e-6
DEPTH = 1
ALPHA = (2.0 * DEPTH) ** 0.25

QK_PAD = 256
ROPE_PAD = 128
ROW_TILE = 512
BIG_ROW_TILE = 1024
HGRN_STEP_ROWS = 256
ABSORB_MAX_ROWS = 512
ATTN_TILE = 512
FFN_COL_TILE = 512
KEY_ALIGN = 128
ROW_SUB = 256
ATTN_UNROLL = 4
FFN_SUB = 256
NEG_BIG = -1e30
LOG2_E = 1.4426950408889634
V7X_VMEM_LIMIT = 56 * 1024 * 1024

NT_DIMS = (((1,), (1,)), ((), ()))
TN_DIMS = (((0,), (0,)), ((), ()))


def _cparams(sem):
    return pltpu.CompilerParams(dimension_semantics=sem, vmem_limit_bytes=V7X_VMEM_LIMIT)


def _layer_norm(x, g, b):
    mu = jnp.mean(x, axis=-1, keepdims=True)
    xc = x - mu
    var = jnp.mean(xc * xc, axis=-1, keepdims=True)
    return xc * lax.rsqrt(var + LN_EPS) * g + b


def _rms_norm(x, g):
    return x * lax.rsqrt(jnp.mean(x * x, axis=-1, keepdims=True) + RMS_EPS) * g


N_HG_TILES = 4
W_MLA_COLS = Q_LORA + KV_LORA + 2 * ROPE_PAD


def _in_proj_kernel(x_ref, g_ref, b_ref, wh_ref, wm_ref, qg_ref, kvg_ref, cos_ref, sin_ref,
                    hg_ref, cq_ref, lat_ref, kr_ref, hb_ref):
    j = pl.program_id(1)

    @pl.when(j == 0)
    def _():
        for r0 in range(0, x_ref.shape[0], ROW_SUB):
            rows = slice(r0, r0 + ROW_SUB)
            hb = _layer_norm(x_ref[rows, :], g_ref[...], b_ref[...]).astype(BF16)
            hb_ref[rows, :] = hb
            hg_ref[rows, :] = jnp.dot(hb, wh_ref[...], preferred_element_type=F32)

    @pl.when(jnp.logical_and(j > 0, j < N_HG_TILES))
    def _():
        hg_ref[...] = jnp.dot(hb_ref[...], wh_ref[...], preferred_element_type=F32)

    @pl.when(j == N_HG_TILES)
    def _():
        p = jnp.dot(hb_ref[...], wm_ref[...], preferred_element_type=F32)
        cq = p[:, :Q_LORA]
        ckv = p[:, Q_LORA:Q_LORA + KV_LORA]
        kr = p[:, Q_LORA + KV_LORA:Q_LORA + KV_LORA + ROPE_PAD]
        kr_sw = p[:, Q_LORA + KV_LORA + ROPE_PAD:]
        cq_ref[...] = _rms_norm(cq, qg_ref[...]).astype(BF16)
        lat_ref[...] = _rms_norm(ckv, kvg_ref[...])
        kr_ref[...] = kr * cos_ref[...] + kr_sw * sin_ref[...]


def _in_proj(x, ln_g, ln_b, w_h, w_m, q_a_g, kv_a_g, cos_k, sin_k, tm):
    m = x.shape[0]
    n_pos_tiles = cos_k.shape[0] // tm
    grid = (m // tm, N_HG_TILES + 1)
    row = lambda i, j: (i, 0)
    const = lambda i, j: (0, 0)
    pos = lambda i, j: (i % n_pos_tiles, 0)
    hg_col = lambda i, j: (i, jnp.minimum(j, N_HG_TILES - 1))
    return pl.pallas_call(
        _in_proj_kernel,
        grid=grid,
        in_specs=[
            pl.BlockSpec((tm, D_MODEL), row),
            pl.BlockSpec((1, D_MODEL), const),
            pl.BlockSpec((1, D_MODEL), const),
            pl.BlockSpec((D_MODEL, D_HGRN), lambda i, j: (0, jnp.minimum(j, N_HG_TILES - 1))),
            pl.BlockSpec((D_MODEL, W_MLA_COLS), const),
            pl.BlockSpec((1, Q_LORA), const),
            pl.BlockSpec((1, KV_LORA), const),
            pl.BlockSpec((tm, ROPE_PAD), pos),
            pl.BlockSpec((tm, ROPE_PAD), pos),
        ],
        out_specs=[
            pl.BlockSpec((tm, D_HGRN), hg_col),
            pl.BlockSpec((tm, Q_LORA), row),
            pl.BlockSpec((tm, KV_LORA), row),
            pl.BlockSpec((tm, ROPE_PAD), row),
        ],
        out_shape=[
            jax.ShapeDtypeStruct((m, N_HG_TILES * D_HGRN), F32),
            jax.ShapeDtypeStruct((m, Q_LORA), BF16),
            jax.ShapeDtypeStruct((m, KV_LORA), F32),
            jax.ShapeDtypeStruct((m, ROPE_PAD), F32),
        ],
        scratch_shapes=[pltpu.VMEM((tm, D_MODEL), BF16)],
        compiler_params=_cparams(("arbitrary", "arbitrary")),
        name="in_proj",
    )(x, ln_g, ln_b, w_h, w_m, q_a_g, kv_a_g, cos_k, sin_k)


def _cumsum_rows(x):
    c = x.shape[0]
    row = lax.broadcasted_iota(jnp.int32, (c, 1), 0)
    shift = 1
    while shift < c:
        x = x + jnp.where(row >= shift, pltpu.roll(x, shift, 0), 0.0)
        shift *= 2
    return x


def _split_level(c):
    t = lax.broadcasted_iota(jnp.int32, (c, c), 0)
    s = lax.broadcasted_iota(jnp.int32, (c, c), 1)
    x = jnp.bitwise_xor(t, s)
    lvl = jnp.full((c, c), -1, jnp.int32)
    for bit in range(c.bit_length() - 1):
        lvl = jnp.where(x >= (1 << bit), bit, lvl)
    return jnp.where(t < s, -2, lvl)


def _hgrn_head_block(q, fpre, v, lb, st, lvl):
    c = q.shape[0]
    f = lb + (1.0 - lb) * jax.nn.sigmoid(fpre)
    g = jnp.log(f)
    k = 1.0 - f
    b = _cumsum_rows(g)
    row = lax.broadcasted_iota(jnp.int32, (c, 1), 0)

    a = jnp.where(lvl == -1, jnp.sum(q * k, axis=-1, keepdims=True), 0.0)
    for level in range(c.bit_length() - 1):
        half = 1 << level
        pos = row % (2 * half)
        is_query = pos >= half
        if level == 0:
            decay = jnp.where(is_query, f, 1.0)
        else:
            if 2 * half < 8:
                ref = b
                for p in range(2 * half):
                    d = half - 1 - p
                    if d != 0:
                        ref = jnp.where(pos == p, pltpu.roll(b, (-d) % c, 0), ref)
            else:
                ref = jnp.concatenate(
                    [jnp.broadcast_to(b[j + half - 1:j + half], (2 * half, b.shape[1]))
                     for j in range(0, c, 2 * half)], axis=0)
            decay = jnp.exp(-jnp.abs(b - ref))
        z = (jnp.where(is_query, q, k) * decay).astype(BF16)
        gram = lax.dot_general(z, z, NT_DIMS, preferred_element_type=F32)
        a = jnp.where(lvl == level, gram, a)

    vb = v.astype(BF16)
    o = jnp.dot(a.astype(BF16), vb, preferred_element_type=F32)
    o = o + lax.dot_general((q * jnp.exp(b)).astype(BF16), st.astype(BF16), NT_DIMS,
                            preferred_element_type=F32)
    b_last = b[c - 1:c]
    k_dec = (k * jnp.exp(b_last - b)).astype(BF16)
    st_new = jnp.exp(b_last) * st + lax.dot_general(vb, k_dec, TN_DIMS, preferred_element_type=F32)
    return o, st_new


def _hgrn_kernel(*refs, c, has_s0):
    if has_s0:
        q_ref, f_ref, i_ref, g_ref, lb_ref, ng_ref, s0_ref, o_ref, s_ref, st_ref = refs
    else:
        q_ref, f_ref, i_ref, g_ref, lb_ref, ng_ref, o_ref, s_ref, st_ref = refs
        s0_ref = None
    t = pl.program_id(1)

    @pl.when(t == 0)
    def _():
        for h in range(HGRN_HEADS):
            if has_s0:
                st_ref[h] = s0_ref[0, h].T
            else:
                st_ref[h] = jnp.zeros((HGRN_DV, HGRN_DK), F32)

    lvl = _split_level(c)

    def block(i, _):
        rows = pl.ds(pl.multiple_of(i * c, c), c)
        for h in range(HGRN_HEADS):
            cols = slice(h * HGRN_DK, (h + 1) * HGRN_DK)
            o, st_new = _hgrn_head_block(q_ref[rows, cols], f_ref[rows, cols], i_ref[rows, cols],
                                         lb_ref[:, cols], st_ref[h], lvl)
            st_ref[h] = st_new
            o = _rms_norm(o, ng_ref[:, cols]) * jax.nn.silu(g_ref[rows, cols])
            o_ref[rows, cols] = o.astype(BF16)
        return 0

    lax.fori_loop(0, q_ref.shape[0] // c, block, 0)

    @pl.when(t == pl.num_programs(1) - 1)
    def _():
        for h in range(HGRN_HEADS):
            s_ref[0, h] = st_ref[h].T


def _hgrn(hg, lb, norm_g, s0, batch, seq, c, rows):
    nt = seq // rows
    has_s0 = s0 is not None
    col = lambda n: (lambda b, t: (b * nt + t, n))
    const = lambda b, t: (0, 0)
    state = lambda b, t: (b, 0, 0, 0)
    in_specs = [pl.BlockSpec((rows, D_HGRN), col(n)) for n in range(4)]
    in_specs += [pl.BlockSpec((1, D_HGRN), const), pl.BlockSpec((1, D_HGRN), const)]
    args = [hg, hg, hg, hg, lb, norm_g]
    if has_s0:
        in_specs.append(pl.BlockSpec((1, HGRN_HEADS, HGRN_DK, HGRN_DV), state))
        args.append(s0)
    return pl.pallas_call(
        functools.partial(_hgrn_kernel, c=c, has_s0=has_s0),
        grid=(batch, nt),
        in_specs=in_specs,
        out_specs=[
            pl.BlockSpec((rows, D_HGRN), lambda b, t: (b * nt + t, 0)),
            pl.BlockSpec((1, HGRN_HEADS, HGRN_DK, HGRN_DV), state),
        ],
        out_shape=[
            jax.ShapeDtypeStruct((batch * seq, D_HGRN), BF16),
            jax.ShapeDtypeStruct((batch, HGRN_HEADS, HGRN_DK, HGRN_DV), F32),
        ],
        scratch_shapes=[pltpu.VMEM((HGRN_HEADS, HGRN_DV, HGRN_DK), F32)],
        compiler_params=_cparams(("arbitrary", "arbitrary")),
        name="hgrn",
    )(*args)


def _q_proj_kernel(cq_ref, wq_ref, wsw_ref, cos_ref, sin_ref, q_ref, *, scale):
    cq = cq_ref[...]
    q_lin = jnp.dot(cq, wq_ref[...], preferred_element_type=F32)
    q_sw = jnp.dot(cq, wsw_ref[...], preferred_element_type=F32)
    cos = cos_ref[...]
    sin = sin_ref[...]
    for h in range(MLA_HEADS):
        base = h * QK_PAD
        q_ref[:, base:base + QK_NOPE] = (q_lin[:, base:base + QK_NOPE] * scale).astype(BF16)
        rot = (q_lin[:, base + QK_NOPE:base + QK_PAD] * cos
               + q_sw[:, h * ROPE_PAD:(h + 1) * ROPE_PAD] * sin)
        q_ref[:, base + QK_NOPE:base + QK_PAD] = (rot * scale).astype(BF16)


def _q_proj(cq, w_q, w_q_sw, cos_k, sin_k, tm):
    m = cq.shape[0]
    n_pos_tiles = cos_k.shape[0] // tm
    scale = LOG2_E * (QK_NOPE + QK_ROPE) ** -0.5
    row = lambda i: (i, 0)
    const = lambda i: (0, 0)
    pos = lambda i: (i % n_pos_tiles, 0)
    return pl.pallas_call(
        functools.partial(_q_proj_kernel, scale=scale),
        grid=(m // tm,),
        in_specs=[
            pl.BlockSpec((tm, Q_LORA), row),
            pl.BlockSpec((Q_LORA, MLA_HEADS * QK_PAD), const),
            pl.BlockSpec((Q_LORA, MLA_HEADS * ROPE_PAD), const),
            pl.BlockSpec((tm, ROPE_PAD), pos),
            pl.BlockSpec((tm, ROPE_PAD), pos),
        ],
        out_specs=pl.BlockSpec((tm, MLA_HEADS * QK_PAD), row),
        out_shape=jax.ShapeDtypeStruct((m, MLA_HEADS * QK_PAD), BF16),
        compiler_params=_cparams(("arbitrary",)),
        name="q_proj",
    )(cq, w_q, w_q_sw, cos_k, sin_k)


def _kv_proj_kernel(lat_ref, kr_ref, wk_ref, wv_ref, k_ref, v_ref):
    lat = lat_ref[...].astype(BF16)
    kn = jnp.dot(lat, wk_ref[...], preferred_element_type=F32)
    v_ref[...] = jnp.dot(lat, wv_ref[...], preferred_element_type=F32).astype(BF16)
    kr = kr_ref[...].astype(BF16)
    for h in range(MLA_HEADS):
        base = h * QK_PAD
        k_ref[:, base:base + QK_NOPE] = kn[:, h * QK_NOPE:(h + 1) * QK_NOPE].astype(BF16)
        k_ref[:, base + QK_NOPE:base + QK_PAD] = kr


def _kv_proj(lat, kr_pad, w_k, w_v, tm):
    m = lat.shape[0]
    row = lambda i: (i, 0)
    const = lambda i: (0, 0)
    return pl.pallas_call(
        _kv_proj_kernel,
        grid=(m // tm,),
        in_specs=[
            pl.BlockSpec((tm, KV_LORA), row),
            pl.BlockSpec((tm, ROPE_PAD), row),
            pl.BlockSpec((KV_LORA, MLA_HEADS * QK_NOPE), const),
            pl.BlockSpec((KV_LORA, D_MLA), const),
        ],
        out_specs=[
            pl.BlockSpec((tm, MLA_HEADS * QK_PAD), row),
            pl.BlockSpec((tm, D_MLA), row),
        ],
        out_shape=[
            jax.ShapeDtypeStruct((m, MLA_HEADS * QK_PAD), BF16),
            jax.ShapeDtypeStruct((m, D_MLA), BF16),
        ],
        compiler_params=_cparams(("arbitrary",)),
        name="kv_proj",
    )(lat, kr_pad, w_k, w_v)


def _attn_kernel(tile_q_ref, tile_k_ref, q_ref, k_ref, v_ref, o_ref,
                 vx_ref, rel_ref, sa_ref, sb_ref, m_ref, acc_ref, *, tq, tk, past, n_keys, n_plain, n_tiles):
    @pl.when(jnp.logical_and(pl.program_id(0) == 0, pl.program_id(1) == 0))
    def _():
        rel_ref[...] = (lax.broadcasted_iota(jnp.int32, (tq, tk), 1) // CHUNK
                        - lax.broadcasted_iota(jnp.int32, (tq, tk), 0) // CHUNK)

    vx_ref[:, :V_HEAD] = v_ref[0]
    vx_ref[:, V_HEAD:] = jnp.ones((vx_ref.shape[0], V_HEAD), BF16)
    m_ref[...] = jnp.full(m_ref.shape, NEG_BIG, F32)
    acc_ref[...] = jnp.zeros(acc_ref.shape, F32)
    k_lane = lax.broadcasted_iota(jnp.int32, (1, tk), 1)

    def origin(t):
        return pl.multiple_of(tile_q_ref[t] * tq, tq), pl.multiple_of(tile_k_ref[t] * tk, tk)

    def scores(t, s_ref):
        qs, ks = origin(t)
        s_ref[...] = lax.dot_general(q_ref[0, pl.ds(qs, tq), :], k_ref[0, pl.ds(ks, tk), :], NT_DIMS,
                                     preferred_element_type=F32)

    def softmax_pv(t, s_ref, masked):
        qs, ks = origin(t)
        s = s_ref[...]
        if masked:
            pad = jnp.where(k_lane < n_keys - ks, 0, 1 << 20)
            visible = rel_ref[...] + pad <= (past + qs - ks) // CHUNK
            s = jnp.where(visible, s, NEG_BIG)
        rows = pl.ds(qs, tq)
        m = m_ref[rows, :]
        m_new = jnp.maximum(m, jnp.max(s, axis=-1, keepdims=True))
        p = jnp.exp2(s - m_new).astype(BF16)
        pv = jnp.dot(p, vx_ref[pl.ds(ks, tk), :], preferred_element_type=F32)
        acc_ref[rows, :] = jnp.exp2(m - m_new) * acc_ref[rows, :] + pv
        m_ref[rows, :] = m_new

    bufs = (sa_ref, sb_ref)

    def step(t, parity, masked):
        scores(t + 1, bufs[1 - parity])
        softmax_pv(t, bufs[parity], masked)

    def run(t0, t1, masked):
        t = t0
        if t < t1 and t % 2 == 1:
            step(t, 1, masked)
            t += 1
        trips = (t1 - t) // ATTN_UNROLL
        if trips > 0:
            def body(i, _, base=t):
                tt = base + ATTN_UNROLL * i
                for u in range(ATTN_UNROLL):
                    step(tt + u, u % 2, masked)
                return 0
            lax.fori_loop(0, trips, body, 0)
            t += ATTN_UNROLL * trips
        while t < t1:
            step(t, t % 2, masked)
            t += 1

    scores(0, sa_ref)
    run(0, min(n_plain, n_tiles - 1), False)
    run(min(n_plain, n_tiles - 1), n_tiles - 1, True)
    last = n_tiles - 1
    softmax_pv(last, bufs[last % 2], last >= n_plain)
    acc = acc_ref[...]
    o_ref[0] = (acc[:, :V_HEAD] / acc[:, V_HEAD:]).astype(BF16)


def _attention_tiles(seq, n_keys_pad, n_keys, past, tq, tk):
    plain, masked = [], []
    for j in range(n_keys_pad // tk):
        for i in range(seq // tq):
            q_lo, k_lo = past + i * tq, j * tk
            q_hi, k_hi = q_lo + tq - 1, k_lo + tk - 1
            if k_lo // CHUNK > q_hi // CHUNK or k_lo >= n_keys:
                continue
            if k_hi // CHUNK <= q_lo // CHUNK and k_hi < n_keys:
                plain.append((i, j))
            else:
                masked.append((i, j))
    return plain, masked


def _attention(q, k, v, batch, seq, n_keys_pad, n_keys, past, tq, tk):
    assert past % CHUNK == 0 and tk % CHUNK == 0 and (tq % CHUNK == 0 or tq == seq)
    plain, masked = _attention_tiles(seq, n_keys_pad, n_keys, past, tq, tk)
    tiles = plain + masked
    tile_q = jnp.asarray([t[0] for t in tiles], jnp.int32)
    tile_k = jnp.asarray([t[1] for t in tiles], jnp.int32)
    smem = pl.BlockSpec(memory_space=pltpu.SMEM)
    return pl.pallas_call(
        functools.partial(_attn_kernel, tq=tq, tk=tk, past=past, n_keys=n_keys,
                          n_plain=len(plain), n_tiles=len(tiles)),
        grid=(batch, MLA_HEADS),
        in_specs=[
            smem, smem,
            pl.BlockSpec((1, seq, QK_PAD), lambda b, h: (b, 0, h)),
            pl.BlockSpec((1, n_keys_pad, QK_PAD), lambda b, h: (b, 0, h)),
            pl.BlockSpec((1, n_keys_pad, V_HEAD), lambda b, h: (b, 0, h)),
        ],
        out_specs=pl.BlockSpec((1, seq, V_HEAD), lambda b, h: (b, 0, h)),
        out_shape=jax.ShapeDtypeStruct((batch, seq, D_MLA), BF16),
        scratch_shapes=[
            pltpu.VMEM((n_keys_pad, 2 * V_HEAD), BF16),
            pltpu.VMEM((tq, tk), jnp.int32),
            pltpu.VMEM((tq, tk), F32),
            pltpu.VMEM((tq, tk), F32),
            pltpu.VMEM((seq, 1), F32),
            pltpu.VMEM((seq, 2 * V_HEAD), F32),
        ],
        compiler_params=_cparams(("arbitrary", "arbitrary")),
        name="attn",
    )(tile_q, tile_k, q, k, v)


LAT_KEY = KV_LORA + ROPE_PAD


def _absorbed_attn_kernel(q_ref, kx_ref, wuk_ref, wuv_ref, o_ref, qx_ref, vx_ref, *, seq, past, n_keys):
    for h in range(MLA_HEADS):
        rows = slice(h * seq, (h + 1) * seq)
        q_lat = jnp.dot(q_ref[:, h * QK_PAD:h * QK_PAD + QK_NOPE], wuk_ref[h], preferred_element_type=F32)
        qx_ref[rows, :KV_LORA] = q_lat.astype(BF16)
        qx_ref[rows, KV_LORA:] = q_ref[:, h * QK_PAD + QK_NOPE:(h + 1) * QK_PAD]
    kx = kx_ref[0]
    s = lax.dot_general(qx_ref[...], kx, NT_DIMS, preferred_element_type=F32)
    q_chunk = (past + lax.broadcasted_iota(jnp.int32, (s.shape[0], 1), 0) % seq) // CHUNK
    k_idx = lax.broadcasted_iota(jnp.int32, (1, s.shape[1]), 1)
    visible = jnp.logical_and(k_idx // CHUNK <= q_chunk, k_idx < n_keys)
    s = jnp.where(visible, s, NEG_BIG)
    p = jnp.exp2(s - jnp.max(s, axis=-1, keepdims=True)).astype(BF16)
    vx_ref[:, :KV_LORA] = kx[:, :KV_LORA]
    vx_ref[:, KV_LORA:] = jnp.ones((vx_ref.shape[0], ROPE_PAD), BF16)
    o_ext = jnp.dot(p, vx_ref[...], preferred_element_type=F32)
    denom = o_ext[:, KV_LORA:]
    o_lat = (o_ext[:, :KV_LORA] / jnp.concatenate([denom] * (KV_LORA // ROPE_PAD), axis=1)).astype(BF16)
    for h in range(MLA_HEADS):
        o_ref[:, h * V_HEAD:(h + 1) * V_HEAD] = jnp.dot(
            o_lat[h * seq:(h + 1) * seq], wuv_ref[h], preferred_element_type=F32).astype(BF16)


def _absorbed_attention(q, kx, w_uk_t, w_uv, batch, seq, n_keys, past):
    n_keys_pad = kx.shape[1]
    const3 = lambda b: (0, 0, 0)
    return pl.pallas_call(
        functools.partial(_absorbed_attn_kernel, seq=seq, past=past, n_keys=n_keys),
        grid=(batch,),
        in_specs=[
            pl.BlockSpec((seq, MLA_HEADS * QK_PAD), lambda b: (b, 0)),
            pl.BlockSpec((1, n_keys_pad, LAT_KEY), lambda b: (b, 0, 0)),
            pl.BlockSpec((MLA_HEADS, QK_NOPE, KV_LORA), const3),
            pl.BlockSpec((MLA_HEADS, KV_LORA, V_HEAD), const3),
        ],
        out_specs=pl.BlockSpec((seq, D_MLA), lambda b: (b, 0)),
        out_shape=jax.ShapeDtypeStruct((batch * seq, D_MLA), BF16),
        scratch_shapes=[
            pltpu.VMEM((MLA_HEADS * seq, LAT_KEY), BF16),
            pltpu.VMEM((n_keys_pad, LAT_KEY), BF16),
        ],
        compiler_params=_cparams(("arbitrary",)),
        name="absorbed_attn",
    )(q, kx, w_uk_t, w_uv)


def _out_proj_kernel(oh_ref, oa_ref, x_ref, g0_ref, b0_ref, w_ref, g1_ref, b1_ref, y_ref):
    for r0 in range(0, x_ref.shape[0], ROW_SUB):
        rows = slice(r0, r0 + ROW_SUB)
        mix = jnp.dot(oh_ref[rows, :], w_ref[:D_HGRN, :], preferred_element_type=F32)
        mix = mix + jnp.dot(oa_ref[rows, :], w_ref[D_HGRN:, :], preferred_element_type=F32)
        h = _layer_norm(x_ref[rows, :], g0_ref[...], b0_ref[...])
        y_ref[rows, :] = _layer_norm(ALPHA * h + mix, g1_ref[...], b1_ref[...])


def _out_proj(oh, oa, x, ln_in_g, ln_in_b, w_out, ln1_g, ln1_b, tm):
    m = x.shape[0]
    row = lambda i: (i, 0)
    const = lambda i: (0, 0)
    vec = pl.BlockSpec((1, D_MODEL), const)
    return pl.pallas_call(
        _out_proj_kernel,
        grid=(m // tm,),
        in_specs=[
            pl.BlockSpec((tm, D_HGRN), row),
            pl.BlockSpec((tm, D_MLA), row),
            pl.BlockSpec((tm, D_MODEL), row),
            vec, vec,
            pl.BlockSpec((D_HGRN + D_MLA, D_MODEL), const),
            vec, vec,
        ],
        out_specs=pl.BlockSpec((tm, D_MODEL), row),
        out_shape=jax.ShapeDtypeStruct((m, D_MODEL), F32),
        compiler_params=_cparams(("arbitrary",)),
        name="out_proj",
    )(oh, oa, x, ln_in_g, ln_in_b, w_out, ln1_g, ln1_b)


def _ffn_up_kernel(x_ref, wu_ref, wg_ref, cw_ref, cb_ref, buf_ref, h_ref, cnew_ref, xb_ref, carry_ref, *,
                   spt, tiles_per_seq):
    i = pl.program_id(0)
    f = pl.program_id(1)
    tm = x_ref.shape[0]
    tf = wu_ref.shape[1]
    rows = tm // spt

    @pl.when(f == 0)
    def _():
        xb_ref[...] = x_ref[...].astype(BF16)

    if tiles_per_seq > 1:
        @pl.when(i % tiles_per_seq == 0)
        def _():
            carry_ref[f] = buf_ref[...]
        halo_ref = carry_ref.at[f]
    else:
        halo_ref = buf_ref

    xb = xb_ref[...]
    r = lax.broadcasted_iota(jnp.int32, (spt, rows, FFN_SUB), 1)
    for c0 in range(0, tf, FFN_SUB):
        cs = slice(c0, c0 + FFN_SUB)
        u = jnp.dot(xb, wu_ref[:, cs], preferred_element_type=F32).reshape(spt, rows, FFN_SUB)
        gate = jnp.dot(xb, wg_ref[:, cs], preferred_element_type=F32).reshape(spt, rows, FFN_SUB)
        halo = halo_ref[:, :, cs]
        prev1 = jnp.where(r == 0, halo[:, 1:2, :], pltpu.roll(u, 1, 1))
        prev2 = jnp.where(r == 0, halo[:, 0:1, :],
                          jnp.where(r == 1, halo[:, 1:2, :], pltpu.roll(u, 2, 1)))
        cw = cw_ref[:, cs]
        a = cb_ref[:, cs] + (prev2 * cw[0:1] + prev1 * cw[1:2] + u * cw[2:3])
        h_ref[:, cs] = (jax.nn.silu(a) * gate).astype(BF16).reshape(tm, FFN_SUB)
        tail = u[:, rows - (CONV_W - 1):, :]
        cnew_ref[:, :, cs] = tail
        if tiles_per_seq > 1:
            carry_ref[f, :, :, cs] = tail


def _ffn_up(x, w_up, w_gate, conv_w, conv_b, conv_buf, seq, tm):
    m = x.shape[0]
    tf = w_up.shape[2]
    n_seq = m // seq
    if tm >= seq:
        spt, tiles_per_seq = tm // seq, 1
        seq_of = lambda i: i
    else:
        spt, tiles_per_seq = 1, seq // tm
        seq_of = lambda i: i // tiles_per_seq
    n_f = D_FF // tf
    hid, tails = pl.pallas_call(
        functools.partial(_ffn_up_kernel, spt=spt, tiles_per_seq=tiles_per_seq),
        grid=(m // tm, n_f),
        in_specs=[
            pl.BlockSpec((tm, D_MODEL), lambda i, f: (i, 0), pipeline_mode=pl.Buffered(1)),
            pl.BlockSpec((None, D_MODEL, tf), lambda i, f: (f, 0, 0)),
            pl.BlockSpec((None, D_MODEL, tf), lambda i, f: (f, 0, 0)),
            pl.BlockSpec((CONV_W, tf), lambda i, f: (0, f)),
            pl.BlockSpec((1, tf), lambda i, f: (0, f)),
            pl.BlockSpec((spt, CONV_W - 1, tf), lambda i, f: (seq_of(i), 0, f)),
        ],
        out_specs=[
            pl.BlockSpec((tm, tf), lambda i, f: (i, f)),
            pl.BlockSpec((spt, CONV_W - 1, tf), lambda i, f: (i, 0, f)),
        ],
        out_shape=[
            jax.ShapeDtypeStruct((m, D_FF), BF16),
            jax.ShapeDtypeStruct((n_seq * tiles_per_seq, CONV_W - 1, D_FF), F32),
        ],
        scratch_shapes=[
            pltpu.VMEM((tm, D_MODEL), BF16),
            pltpu.VMEM((n_f, spt, CONV_W - 1, tf), F32),
        ],
        compiler_params=_cparams(("arbitrary", "arbitrary")),
        name="ffn_up",
    )(x, w_up, w_gate, conv_w, conv_b, conv_buf)
    return hid, tails[tiles_per_seq - 1::tiles_per_seq]


def _ffn_down_kernel(h_ref, x_ref, w_ref, g_ref, b_ref, y_ref):
    for r0 in range(0, x_ref.shape[0], ROW_SUB):
        rows = slice(r0, r0 + ROW_SUB)
        ff = jnp.dot(h_ref[rows, :], w_ref[...], preferred_element_type=F32)
        y_ref[rows, :] = _layer_norm(ALPHA * x_ref[rows, :] + ff, g_ref[...], b_ref[...])


def _ffn_down(hid, x, w_down, ln2_g, ln2_b, tm):
    m = x.shape[0]
    row = lambda i: (i, 0)
    const = lambda i: (0, 0)
    return pl.pallas_call(
        _ffn_down_kernel,
        grid=(m // tm,),
        in_specs=[
            pl.BlockSpec((tm, D_FF), row),
            pl.BlockSpec((tm, D_MODEL), row),
            pl.BlockSpec((D_FF, D_MODEL), const, pipeline_mode=pl.Buffered(1)),
            pl.BlockSpec((1, D_MODEL), const),
            pl.BlockSpec((1, D_MODEL), const),
        ],
        out_specs=pl.BlockSpec((tm, D_MODEL), row),
        out_shape=jax.ShapeDtypeStruct((m, D_MODEL), F32),
        compiler_params=_cparams(("arbitrary",)),
        name="ffn_down",
    )(hid, x, w_down, ln2_g, ln2_b)


def _rope_tables(past, seq, rows):
    inv = ROPE_THETA ** (-jnp.arange(0, QK_ROPE, 2, dtype=F32) / QK_ROPE)
    ang = (past + jnp.arange(seq)).astype(F32)[:, None] * inv[None, :]
    cos, sin = jnp.cos(ang), jnp.sin(ang)
    zero = jnp.zeros((seq, ROPE_PAD - QK_ROPE), F32)
    cos_t = jnp.concatenate([cos, cos, zero], axis=1)
    sin_t = jnp.concatenate([-sin, sin, zero], axis=1)
    reps = rows // seq
    return jnp.tile(cos_t, (reps, 1)), jnp.tile(sin_t, (reps, 1))


def _swap_halves(w):
    half = QK_ROPE // 2
    return jnp.concatenate([w[..., half:], w[..., :half]], axis=-1)


def _col_tiles(w):
    return w.reshape(D_MODEL, D_FF // FFN_COL_TILE, FFN_COL_TILE).transpose(1, 0, 2).astype(BF16)


def _prep_weights(w_in, w_q_b, w_kv_b, w_out, w_ffn_up, w_ffn_gate, w_ffn_down):
    w_in = w_in[0]
    c0 = 4 * D_HGRN
    w_h = w_in[:, :c0].astype(BF16)
    kr_cols = w_in[:, c0 + Q_LORA + KV_LORA:]
    zpad = jnp.zeros((D_MODEL, ROPE_PAD - QK_ROPE), F32)
    w_m = jnp.concatenate([w_in[:, c0:c0 + Q_LORA + KV_LORA], kr_cols, zpad,
                           _swap_halves(kr_cols), zpad], axis=1).astype(BF16)

    wq = w_q_b[0].reshape(Q_LORA, MLA_HEADS, QK_NOPE + QK_ROPE)
    zq = jnp.zeros((Q_LORA, MLA_HEADS, QK_PAD - QK_NOPE - QK_ROPE), F32)
    w_q = jnp.concatenate([wq, zq], axis=-1).reshape(Q_LORA, MLA_HEADS * QK_PAD).astype(BF16)
    zr = jnp.zeros((Q_LORA, MLA_HEADS, ROPE_PAD - QK_ROPE), F32)
    w_q_sw = jnp.concatenate([_swap_halves(wq[..., QK_NOPE:]), zr], axis=-1)
    w_q_sw = w_q_sw.reshape(Q_LORA, MLA_HEADS * ROPE_PAD).astype(BF16)

    wkv = w_kv_b[0].reshape(KV_LORA, MLA_HEADS, QK_NOPE + V_HEAD)
    w_k = wkv[..., :QK_NOPE].reshape(KV_LORA, MLA_HEADS * QK_NOPE).astype(BF16)
    w_v = wkv[..., QK_NOPE:].reshape(KV_LORA, D_MLA).astype(BF16)
    w_uk_t = wkv[..., :QK_NOPE].transpose(1, 2, 0).astype(BF16)
    w_uv = wkv[..., QK_NOPE:].transpose(1, 0, 2).astype(BF16)
    return dict(w_h=w_h, w_m=w_m, w_q=w_q, w_q_sw=w_q_sw, w_k=w_k, w_v=w_v, w_uk_t=w_uk_t, w_uv=w_uv,
                w_out=w_out[0].astype(BF16), w_up=_col_tiles(w_ffn_up[0]),
                w_gate=_col_tiles(w_ffn_gate[0]), w_down=w_ffn_down[0].astype(BF16))


def _row(a):
    return a.reshape(1, -1)


def _tile_plan(batch, seq, n_keys_pad):
    m = batch * seq
    tm = min(ROW_TILE, m)
    tm_big = min(BIG_ROW_TILE, m)
    assert m % tm_big == 0 and (tm_big % seq == 0 or seq % tm_big == 0)
    return dict(
        tm=tm,
        tm_in=tm_big,
        tm_ffn=tm_big,
        rows_hgrn=min(HGRN_STEP_ROWS, seq),
        tq=min(ATTN_TILE, seq),
        tk=ATTN_TILE if n_keys_pad % ATTN_TILE == 0 else n_keys_pad,
        tm_kv=ROW_TILE if n_keys_pad % ROW_TILE == 0 else n_keys_pad,
    )


def _encoder(x, s0, lat_past, kr_past, conv_buf, p, w, *, blk):
    batch, seq, _ = x.shape
    past = 0 if lat_past is None else lat_past.shape[1]
    m = batch * seq
    n_keys = past + seq
    key_pad = -n_keys % KEY_ALIGN
    n_keys_pad = n_keys + key_pad
    t = _tile_plan(batch, seq, n_keys_pad)
    x2 = x.reshape(m, D_MODEL)
    cos_k, sin_k = _rope_tables(past, seq, max(seq, t["tm_in"]))

    hg, cqn, lat_new, kr_new = _in_proj(x2, p["ln_in_g"], p["ln_in_b"], w["w_h"], w["w_m"],
                                        p["q_a_g"], p["kv_a_g"], cos_k, sin_k, t["tm_in"])
    o_h, s_new = _hgrn(hg, p["lb"], p["hgrn_norm_g"], s0, batch, seq, blk, t["rows_hgrn"])

    q = _q_proj(cqn, w["w_q"], w["w_q_sw"], cos_k, sin_k, t["tm"])
    lat3 = lat_new.reshape(batch, seq, KV_LORA)
    kr3 = kr_new.reshape(batch, seq, ROPE_PAD)
    if past or key_pad:
        parts_l, parts_r = [], []
        if past:
            parts_l.append(lat_past)
            parts_r.append(jnp.pad(kr_past, ((0, 0), (0, 0), (0, ROPE_PAD - QK_ROPE))))
        parts_l.append(lat3)
        parts_r.append(kr3)
        if key_pad:
            parts_l.append(jnp.zeros((batch, key_pad, KV_LORA), F32))
            parts_r.append(jnp.zeros((batch, key_pad, ROPE_PAD), F32))
        lat_all = jnp.concatenate(parts_l, axis=1)
        kr_all = jnp.concatenate(parts_r, axis=1)
    else:
        lat_all, kr_all = lat3, kr3
    if seq * MLA_HEADS <= ABSORB_MAX_ROWS:
        kx = jnp.concatenate([lat_all, kr_all], axis=-1).astype(BF16)
        o_a = _absorbed_attention(q, kx, w["w_uk_t"], w["w_uv"], batch, seq, n_keys, past)
    else:
        k_cat, v = _kv_proj(lat_all.reshape(batch * n_keys_pad, KV_LORA),
                            kr_all.reshape(batch * n_keys_pad, ROPE_PAD), w["w_k"], w["w_v"], t["tm_kv"])
        o_a = _attention(q.reshape(batch, seq, MLA_HEADS * QK_PAD),
                         k_cat.reshape(batch, n_keys_pad, MLA_HEADS * QK_PAD),
                         v.reshape(batch, n_keys_pad, D_MLA),
                         batch, seq, n_keys_pad, n_keys, past, t["tq"], t["tk"]).reshape(m, D_MLA)

    h1 = _out_proj(o_h, o_a, x2, p["ln_in_g"], p["ln_in_b"], w["w_out"],
                   p["ln1_g"], p["ln1_b"], t["tm"])
    hid, conv_new = _ffn_up(h1, w["w_up"], w["w_gate"], p["conv_w"], p["conv_b"], conv_buf, seq, t["tm_ffn"])
    y = _ffn_down(hid, h1, w["w_down"], p["ln2_g"], p["ln2_b"], t["tm"])
    return (y.reshape(batch, seq, D_MODEL), s_new[None], lat3[None],
            kr3[:, :, :QK_ROPE][None], conv_new[None])


def kernel(x_prompt, x_sample, cache_kv_latent, cache_k_rope, state_hgrn, cache_ffn_conv, lb_param, ln_in_g, ln_in_b, w_in, hgrn_norm_g, q_a_g, w_q_b, kv_a_g, w_kv_b, w_out, ln1_g, ln1_b, w_ffn_up, w_ffn_gate, conv_w, conv_b, w_ffn_down, ln2_g, ln2_b):
    w = _prep_weights(w_in, w_q_b, w_kv_b, w_out, w_ffn_up, w_ffn_gate, w_ffn_down)
    lbs = jnp.cumsum(jax.nn.softmax(lb_param.astype(F32), axis=0), axis=0)
    p = dict(ln_in_g=_row(ln_in_g), ln_in_b=_row(ln_in_b), lb=_row(lbs[0]),
             hgrn_norm_g=_row(hgrn_norm_g[0]), q_a_g=_row(q_a_g[0]), kv_a_g=_row(kv_a_g[0]),
             ln1_g=_row(ln1_g[0]), ln1_b=_row(ln1_b[0]), conv_w=conv_w[0], conv_b=_row(conv_b[0]),
             ln2_g=_row(ln2_g[0]), ln2_b=_row(ln2_b[0]))

    n_prompt = x_prompt.shape[0]
    conv0 = jnp.zeros((n_prompt, CONV_W - 1, D_FF), F32)
    y_p, s_p, lat_p, kr_p, conv_p = _encoder(x_prompt, None, None, None, conv0, p, w, blk=CHUNK)
    y_s, s_s, lat_s, kr_s, conv_s = _encoder(
        x_sample, state_hgrn[0], cache_kv_latent[0], cache_k_rope[0], cache_ffn_conv[0], p, w,
        blk=x_sample.shape[1])
    return (y_p, y_s, lat_p, kr_p, s_p, conv_p, lat_s, kr_s, s_s, conv_s)
```

```python
import functools

import numpy as np
import jax
import jax.numpy as jnp
from jax import lax
from jax.experimental import pallas as pl
from jax.experimental.pallas import tpu as pltpu

F32 = jnp.float32
BF16 = jnp.bfloat16

D_MODEL = 2048
CHUNK = 64
HGRN_HEADS = 8
HGRN_DK = 128
HGRN_DV = 128
D_HGRN = HGRN_HEADS * HGRN_DK
MLA_HEADS = 8
Q_LORA = 512
KV_LORA = 256
QK_NOPE = 128
QK_ROPE = 64
V_HEAD = 128
D_MLA = MLA_HEADS * V_HEAD
D_FF = 5632
CONV_W = 3
ROPE_THETA = 10000.0
LN_EPS = 1e-5
RMS_EPS = 1e-6
DEPTH = 1
ALPHA = (2.0 * DEPTH) ** 0.25

QK_PAD = 256
ROPE_PAD = 128
ROW_TILE = 512
BIG_ROW_TILE = 1024
HGRN_STEP_ROWS = 256
ABSORB_MAX_ROWS = 512
ATTN_TILE = 512
FFN_COL_TILE = 1408
KEY_ALIGN = 128
ROW_SUB = 256
ATTN_UNROLL = 4
FFN_SUB = 256
NEG_BIG = -1e30
LOG2_E = 1.4426950408889634
V7X_VMEM_LIMIT = 56 * 1024 * 1024

NT_DIMS = (((1,), (1,)), ((), ()))
TN_DIMS = (((0,), (0,)), ((), ()))


def _cparams(sem):
    return pltpu.CompilerParams(dimension_semantics=sem, vmem_limit_bytes=V7X_VMEM_LIMIT)


def _layer_norm(x, g, b):
    mu = jnp.mean(x, axis=-1, keepdims=True)
    xc = x - mu
    var = jnp.mean(xc * xc, axis=-1, keepdims=True)
    return xc * lax.rsqrt(var + LN_EPS) * g + b


def _rms_norm(x, g):
    return x * lax.rsqrt(jnp.mean(x * x, axis=-1, keepdims=True) + RMS_EPS) * g


N_HG_TILES = 4
W_MLA_COLS = Q_LORA + KV_LORA + 2 * ROPE_PAD


def _in_proj_kernel(x_ref, g_ref, b_ref, wh_ref, wm_ref, qg_ref, kvg_ref, cos_ref, sin_ref,
                    hg_ref, cq_ref, lat_ref, kr_ref, hb_ref):
    j = pl.program_id(1)

    @pl.when(j == 0)
    def _():
        for r0 in range(0, x_ref.shape[0], ROW_SUB):
            rows = slice(r0, r0 + ROW_SUB)
            hb = _layer_norm(x_ref[rows, :], g_ref[...], b_ref[...]).astype(BF16)
            hb_ref[rows, :] = hb
            hg_ref[rows, :] = jnp.dot(hb, wh_ref[...], preferred_element_type=F32)

    @pl.when(jnp.logical_and(j > 0, j < N_HG_TILES))
    def _():
        hg_ref[...] = jnp.dot(hb_ref[...], wh_ref[...], preferred_element_type=F32)

    @pl.when(j == N_HG_TILES)
    def _():
        p = jnp.dot(hb_ref[...], wm_ref[...], preferred_element_type=F32)
        cq = p[:, :Q_LORA]
        ckv = p[:, Q_LORA:Q_LORA + KV_LORA]
        kr = p[:, Q_LORA + KV_LORA:Q_LORA + KV_LORA + ROPE_PAD]
        kr_sw = p[:, Q_LORA + KV_LORA + ROPE_PAD:]
        cq_ref[...] = _rms_norm(cq, qg_ref[...]).astype(BF16)
        lat_ref[...] = _rms_norm(ckv, kvg_ref[...])
        kr_ref[...] = kr * cos_ref[...] + kr_sw * sin_ref[...]


def _in_proj(x, ln_g, ln_b, w_h, w_m, q_a_g, kv_a_g, cos_k, sin_k, tm):
    m = x.shape[0]
    n_pos_tiles = cos_k.shape[0] // tm
    grid = (m // tm, N_HG_TILES + 1)
    row = lambda i, j: (i, 0)
    const = lambda i, j: (0, 0)
    pos = lambda i, j: (i % n_pos_tiles, 0)
    hg_col = lambda i, j: (i, jnp.minimum(j, N_HG_TILES - 1))
    return pl.pallas_call(
        _in_proj_kernel,
        grid=grid,
        in_specs=[
            pl.BlockSpec((tm, D_MODEL), row),
            pl.BlockSpec((1, D_MODEL), const),
            pl.BlockSpec((1, D_MODEL), const),
            pl.BlockSpec((D_MODEL, D_HGRN), lambda i, j: (0, jnp.minimum(j, N_HG_TILES - 1))),
            pl.BlockSpec((D_MODEL, W_MLA_COLS), const),
            pl.BlockSpec((1, Q_LORA), const),
            pl.BlockSpec((1, KV_LORA), const),
            pl.BlockSpec((tm, ROPE_PAD), pos),
            pl.BlockSpec((tm, ROPE_PAD), pos),
        ],
        out_specs=[
            pl.BlockSpec((tm, D_HGRN), hg_col),
            pl.BlockSpec((tm, Q_LORA), row),
            pl.BlockSpec((tm, KV_LORA), row),
            pl.BlockSpec((tm, ROPE_PAD), row),
        ],
        out_shape=[
            jax.ShapeDtypeStruct((m, N_HG_TILES * D_HGRN), F32),
            jax.ShapeDtypeStruct((m, Q_LORA), BF16),
            jax.ShapeDtypeStruct((m, KV_LORA), F32),
            jax.ShapeDtypeStruct((m, ROPE_PAD), F32),
        ],
        scratch_shapes=[pltpu.VMEM((tm, D_MODEL), BF16)],
        compiler_params=_cparams(("arbitrary", "arbitrary")),
        name="in_proj",
    )(x, ln_g, ln_b, w_h, w_m, q_a_g, kv_a_g, cos_k, sin_k)


def _cumsum_rows(x):
    c = x.shape[0]
    row = lax.broadcasted_iota(jnp.int32, (c, 1), 0)
    shift = 1
    while shift < c:
        x = x + jnp.where(row >= shift, pltpu.roll(x, shift, 0), 0.0)
        shift *= 2
    return x


def _split_level(c):
    t = lax.broadcasted_iota(jnp.int32, (c, c), 0)
    s = lax.broadcasted_iota(jnp.int32, (c, c), 1)
    x = jnp.bitwise_xor(t, s)
    lvl = jnp.full((c, c), -1, jnp.int32)
    for bit in range(c.bit_length() - 1):
        lvl = jnp.where(x >= (1 << bit), bit, lvl)
    return jnp.where(t < s, -2, lvl)


def _hgrn_head_block(q, fpre, v, lb, st, lvl):
    c = q.shape[0]
    f = lb + (1.0 - lb) * jax.nn.sigmoid(fpre)
    g = jnp.log(f)
    k = 1.0 - f
    b = _cumsum_rows(g)
    row = lax.broadcasted_iota(jnp.int32, (c, 1), 0)

    a = jnp.where(lvl == -1, jnp.sum(q * k, axis=-1, keepdims=True), 0.0)
    for level in range(c.bit_length() - 1):
        half = 1 << level
        pos = row % (2 * half)
        is_query = pos >= half
        if level == 0:
            decay = jnp.where(is_query, f, 1.0)
        else:
            if 2 * half < 8:
                ref = b
                for p in range(2 * half):
                    d = half - 1 - p
                    if d != 0:
                        ref = jnp.where(pos == p, pltpu.roll(b, (-d) % c, 0), ref)
            else:
                ref = jnp.concatenate(
                    [jnp.broadcast_to(b[j + half - 1:j + half], (2 * half, b.shape[1]))
                     for j in range(0, c, 2 * half)], axis=0)
            decay = jnp.exp(-jnp.abs(b - ref))
        z = (jnp.where(is_query, q, k) * decay).astype(BF16)
        gram = lax.dot_general(z, z, NT_DIMS, preferred_element_type=F32)
        a = jnp.where(lvl == level, gram, a)

    vb = v.astype(BF16)
    o = jnp.dot(a.astype(BF16), vb, preferred_element_type=F32)
    o = o + lax.dot_general((q * jnp.exp(b)).astype(BF16), st.astype(BF16), NT_DIMS,
                            preferred_element_type=F32)
    b_last = b[c - 1:c]
    k_dec = (k * jnp.exp(b_last - b)).astype(BF16)
    st_new = jnp.exp(b_last) * st + lax.dot_general(vb, k_dec, TN_DIMS, preferred_element_type=F32)
    return o, st_new


def _hgrn_kernel(*refs, c, has_s0):
    if has_s0:
        q_ref, f_ref, i_ref, g_ref, lb_ref, ng_ref, s0_ref, o_ref, s_ref, st_ref = refs
    else:
        q_ref, f_ref, i_ref, g_ref, lb_ref, ng_ref, o_ref, s_ref, st_ref = refs
        s0_ref = None
    t = pl.program_id(1)

    @pl.when(t == 0)
    def _():
        for h in range(HGRN_HEADS):
            if has_s0:
                st_ref[h] = s0_ref[0, h].T
            else:
                st_ref[h] = jnp.zeros((HGRN_DV, HGRN_DK), F32)

    lvl = _split_level(c)

    def block(i, _):
        rows = pl.ds(pl.multiple_of(i * c, c), c)
        for h in range(HGRN_HEADS):
            cols = slice(h * HGRN_DK, (h + 1) * HGRN_DK)
            o, st_new = _hgrn_head_block(q_ref[rows, cols], f_ref[rows, cols], i_ref[rows, cols],
                                         lb_ref[:, cols], st_ref[h], lvl)
            st_ref[h] = st_new
            o = _rms_norm(o, ng_ref[:, cols]) * jax.nn.silu(g_ref[rows, cols])
            o_ref[rows, cols] = o.astype(BF16)
        return 0

    lax.fori_loop(0, q_ref.shape[0] // c, block, 0)

    @pl.when(t == pl.num_programs(1) - 1)
    def _():
        for h in range(HGRN_HEADS):
            s_ref[0, h] = st_ref[h].T


def _hgrn(hg, lb, norm_g, s0, batch, seq, c, rows):
    nt = seq // rows
    has_s0 = s0 is not None
    col = lambda n: (lambda b, t: (b * nt + t, n))
    const = lambda b, t: (0, 0)
    state = lambda b, t: (b, 0, 0, 0)
    in_specs = [pl.BlockSpec((rows, D_HGRN), col(n)) for n in range(4)]
    in_specs += [pl.BlockSpec((1, D_HGRN), const), pl.BlockSpec((1, D_HGRN), const)]
    args = [hg, hg, hg, hg, lb, norm_g]
    if has_s0:
        in_specs.append(pl.BlockSpec((1, HGRN_HEADS, HGRN_DK, HGRN_DV), state))
        args.append(s0)
    return pl.pallas_call(
        functools.partial(_hgrn_kernel, c=c, has_s0=has_s0),
        grid=(batch, nt),
        in_specs=in_specs,
        out_specs=[
            pl.BlockSpec((rows, D_HGRN), lambda b, t: (b * nt + t, 0)),
            pl.BlockSpec((1, HGRN_HEADS, HGRN_DK, HGRN_DV), state),
        ],
        out_shape=[
            jax.ShapeDtypeStruct((batch * seq, D_HGRN), BF16),
            jax.ShapeDtypeStruct((batch, HGRN_HEADS, HGRN_DK, HGRN_DV), F32),
        ],
        scratch_shapes=[pltpu.VMEM((HGRN_HEADS, HGRN_DV, HGRN_DK), F32)],
        compiler_params=_cparams(("arbitrary", "arbitrary")),
        name="hgrn",
    )(*args)


def _q_proj_kernel(cq_ref, wq_ref, wsw_ref, cos_ref, sin_ref, q_ref, *, scale):
    cq = cq_ref[...]
    q_lin = jnp.dot(cq, wq_ref[...], preferred_element_type=F32)
    q_sw = jnp.dot(cq, wsw_ref[...], preferred_element_type=F32)
    cos = cos_ref[...]
    sin = sin_ref[...]
    for h in range(MLA_HEADS):
        base = h * QK_PAD
        q_ref[:, base:base + QK_NOPE] = (q_lin[:, base:base + QK_NOPE] * scale).astype(BF16)
        rot = (q_lin[:, base + QK_NOPE:base + QK_PAD] * cos
               + q_sw[:, h * ROPE_PAD:(h + 1) * ROPE_PAD] * sin)
        q_ref[:, base + QK_NOPE:base + QK_PAD] = (rot * scale).astype(BF16)


def _q_proj(cq, w_q, w_q_sw, cos_k, sin_k, tm):
    m = cq.shape[0]
    n_pos_tiles = cos_k.shape[0] // tm
    scale = LOG2_E * (QK_NOPE + QK_ROPE) ** -0.5
    row = lambda i: (i, 0)
    const = lambda i: (0, 0)
    pos = lambda i: (i % n_pos_tiles, 0)
    return pl.pallas_call(
        functools.partial(_q_proj_kernel, scale=scale),
        grid=(m // tm,),
        in_specs=[
            pl.BlockSpec((tm, Q_LORA), row),
            pl.BlockSpec((Q_LORA, MLA_HEADS * QK_PAD), const),
            pl.BlockSpec((Q_LORA, MLA_HEADS * ROPE_PAD), const),
            pl.BlockSpec((tm, ROPE_PAD), pos),
            pl.BlockSpec((tm, ROPE_PAD), pos),
        ],
        out_specs=pl.BlockSpec((tm, MLA_HEADS * QK_PAD), row),
        out_shape=jax.ShapeDtypeStruct((m, MLA_HEADS * QK_PAD), BF16),
        compiler_params=_cparams(("arbitrary",)),
        name="q_proj",
    )(cq, w_q, w_q_sw, cos_k, sin_k)


def _kv_proj_kernel(lat_ref, kr_ref, wk_ref, wv_ref, k_ref, v_ref):
    lat = lat_ref[...].astype(BF16)
    kn = jnp.dot(lat, wk_ref[...], preferred_element_type=F32)
    v_ref[...] = jnp.dot(lat, wv_ref[...], preferred_element_type=F32).astype(BF16)
    kr = kr_ref[...].astype(BF16)
    for h in range(MLA_HEADS):
        base = h * QK_PAD
        k_ref[:, base:base + QK_NOPE] = kn[:, h * QK_NOPE:(h + 1) * QK_NOPE].astype(BF16)
        k_ref[:, base + QK_NOPE:base + QK_PAD] = kr


def _kv_proj(lat, kr_pad, w_k, w_v, tm):
    m = lat.shape[0]
    row = lambda i: (i, 0)
    const = lambda i: (0, 0)
    return pl.pallas_call(
        _kv_proj_kernel,
        grid=(m // tm,),
        in_specs=[
            pl.BlockSpec((tm, KV_LORA), row),
            pl.BlockSpec((tm, ROPE_PAD), row),
            pl.BlockSpec((KV_LORA, MLA_HEADS * QK_NOPE), const),
            pl.BlockSpec((KV_LORA, D_MLA), const),
        ],
        out_specs=[
            pl.BlockSpec((tm, MLA_HEADS * QK_PAD), row),
            pl.BlockSpec((tm, D_MLA), row),
        ],
        out_shape=[
            jax.ShapeDtypeStruct((m, MLA_HEADS * QK_PAD), BF16),
            jax.ShapeDtypeStruct((m, D_MLA), BF16),
        ],
        compiler_params=_cparams(("arbitrary",)),
        name="kv_proj",
    )(lat, kr_pad, w_k, w_v)


def _attn_kernel(tile_q_ref, tile_k_ref, q_ref, k_ref, v_ref, o_ref,
                 vx_ref, rel_ref, sa_ref, sb_ref, m_ref, acc_ref, *, tq, tk, past, n_keys, n_plain, n_tiles):
    @pl.when(jnp.logical_and(pl.program_id(0) == 0, pl.program_id(1) == 0))
    def _():
        rel_ref[...] = (lax.broadcasted_iota(jnp.int32, (tq, tk), 1) // CHUNK
                        - lax.broadcasted_iota(jnp.int32, (tq, tk), 0) // CHUNK)

    vx_ref[:, :V_HEAD] = v_ref[0]
    vx_ref[:, V_HEAD:] = jnp.ones((vx_ref.shape[0], V_HEAD), BF16)
    m_ref[...] = jnp.full(m_ref.shape, NEG_BIG, F32)
    acc_ref[...] = jnp.zeros(acc_ref.shape, F32)
    k_lane = lax.broadcasted_iota(jnp.int32, (1, tk), 1)

    def origin(t):
        return pl.multiple_of(tile_q_ref[t] * tq, tq), pl.multiple_of(tile_k_ref[t] * tk, tk)

    def scores(t, s_ref):
        qs, ks = origin(t)
        s_ref[...] = lax.dot_general(q_ref[0, pl.ds(qs, tq), :], k_ref[0, pl.ds(ks, tk), :], NT_DIMS,
                                     preferred_element_type=F32)

    def softmax_pv(t, s_ref, masked):
        qs, ks = origin(t)
        s = s_ref[...]
        if masked:
            pad = jnp.where(k_lane < n_keys - ks, 0, 1 << 20)
            visible = rel_ref[...] + pad <= (past + qs - ks) // CHUNK
            s = jnp.where(visible, s, NEG_BIG)
        rows = pl.ds(qs, tq)
        m = m_ref[rows, :]
        m_new = jnp.maximum(m, jnp.max(s, axis=-1, keepdims=True))
        p = jnp.exp2(s - m_new).astype(BF16)
        pv = jnp.dot(p, vx_ref[pl.ds(ks, tk), :], preferred_element_type=F32)
        acc_ref[rows, :] = jnp.exp2(m - m_new) * acc_ref[rows, :] + pv
        m_ref[rows, :] = m_new

    bufs = (sa_ref, sb_ref)

    def step(t, parity, masked):
        scores(t + 1, bufs[1 - parity])
        softmax_pv(t, bufs[parity], masked)

    def run(t0, t1, masked):
        t = t0
        if t < t1 and t % 2 == 1:
            step(t, 1, masked)
            t += 1
        trips = (t1 - t) // ATTN_UNROLL
        if trips > 0:
            def body(i, _, base=t):
                tt = base + ATTN_UNROLL * i
                for u in range(ATTN_UNROLL):
                    step(tt + u, u % 2, masked)
                return 0
            lax.fori_loop(0, trips, body, 0)
            t += ATTN_UNROLL * trips
        while t < t1:
            step(t, t % 2, masked)
            t += 1

    scores(0, sa_ref)
    run(0, min(n_plain, n_tiles - 1), False)
    run(min(n_plain, n_tiles - 1), n_tiles - 1, True)
    last = n_tiles - 1
    softmax_pv(last, bufs[last % 2], last >= n_plain)
    acc = acc_ref[...]
    o_ref[0] = (acc[:, :V_HEAD] / acc[:, V_HEAD:]).astype(BF16)


def _attention_tiles(seq, n_keys_pad, n_keys, past, tq, tk):
    plain, masked = [], []
    for j in range(n_keys_pad // tk):
        for i in range(seq // tq):
            q_lo, k_lo = past + i * tq, j * tk
            q_hi, k_hi = q_lo + tq - 1, k_lo + tk - 1
            if k_lo // CHUNK > q_hi // CHUNK or k_lo >= n_keys:
                continue
            if k_hi // CHUNK <= q_lo // CHUNK and k_hi < n_keys:
                plain.append((i, j))
            else:
                masked.append((i, j))
    return plain, masked


def _attention(q, k, v, batch, seq, n_keys_pad, n_keys, past, tq, tk):
    assert past % CHUNK == 0 and tk % CHUNK == 0 and (tq % CHUNK == 0 or tq == seq)
    plain, masked = _attention_tiles(seq, n_keys_pad, n_keys, past, tq, tk)
    tiles = plain + masked
    tile_q = jnp.asarray([t[0] for t in tiles], jnp.int32)
    tile_k = jnp.asarray([t[1] for t in tiles], jnp.int32)
    smem = pl.BlockSpec(memory_space=pltpu.SMEM)
    return pl.pallas_call(
        functools.partial(_attn_kernel, tq=tq, tk=tk, past=past, n_keys=n_keys,
                          n_plain=len(plain), n_tiles=len(tiles)),
        grid=(batch, MLA_HEADS),
        in_specs=[
            smem, smem,
            pl.BlockSpec((1, seq, QK_PAD), lambda b, h: (b, 0, h)),
            pl.BlockSpec((1, n_keys_pad, QK_PAD), lambda b, h: (b, 0, h)),
            pl.BlockSpec((1, n_keys_pad, V_HEAD), lambda b, h: (b, 0, h)),
        ],
        out_specs=pl.BlockSpec((1, seq, V_HEAD), lambda b, h: (b, 0, h)),
        out_shape=jax.ShapeDtypeStruct((batch, seq, D_MLA), BF16),
        scratch_shapes=[
            pltpu.VMEM((n_keys_pad, 2 * V_HEAD), BF16),
            pltpu.VMEM((tq, tk), jnp.int32),
            pltpu.VMEM((tq, tk), F32),
            pltpu.VMEM((tq, tk), F32),
            pltpu.VMEM((seq, 1), F32),
            pltpu.VMEM((seq, 2 * V_HEAD), F32),
        ],
        compiler_params=_cparams(("arbitrary", "arbitrary")),
        name="attn",
    )(tile_q, tile_k, q, k, v)


LAT_KEY = KV_LORA + ROPE_PAD


def _absorbed_attn_kernel(q_ref, kx_ref, wuk_ref, wuv_ref, o_ref, qx_ref, vx_ref, *, seq, past, n_keys):
    for h in range(MLA_HEADS):
        rows = slice(h * seq, (h + 1) * seq)
        q_lat = jnp.dot(q_ref[:, h * QK_PAD:h * QK_PAD + QK_NOPE], wuk_ref[h], preferred_element_type=F32)
        qx_ref[rows, :KV_LORA] = q_lat.astype(BF16)
        qx_ref[rows, KV_LORA:] = q_ref[:, h * QK_PAD + QK_NOPE:(h + 1) * QK_PAD]
    kx = kx_ref[0]
    s = lax.dot_general(qx_ref[...], kx, NT_DIMS, preferred_element_type=F32)
    q_chunk = (past + lax.broadcasted_iota(jnp.int32, (s.shape[0], 1), 0) % seq) // CHUNK
    k_idx = lax.broadcasted_iota(jnp.int32, (1, s.shape[1]), 1)
    visible = jnp.logical_and(k_idx // CHUNK <= q_chunk, k_idx < n_keys)
    s = jnp.where(visible, s, NEG_BIG)
    p = jnp.exp2(s - jnp.max(s, axis=-1, keepdims=True)).astype(BF16)
    vx_ref[:, :KV_LORA] = kx[:, :KV_LORA]
    vx_ref[:, KV_LORA:] = jnp.ones((vx_ref.shape[0], ROPE_PAD), BF16)
    o_ext = jnp.dot(p, vx_ref[...], preferred_element_type=F32)
    denom = o_ext[:, KV_LORA:]
    o_lat = (o_ext[:, :KV_LORA] / jnp.concatenate([denom] * (KV_LORA // ROPE_PAD), axis=1)).astype(BF16)
    for h in range(MLA_HEADS):
        o_ref[:, h * V_HEAD:(h + 1) * V_HEAD] = jnp.dot(
            o_lat[h * seq:(h + 1) * seq], wuv_ref[h], preferred_element_type=F32).astype(BF16)


def _absorbed_attention(q, kx, w_uk_t, w_uv, batch, seq, n_keys, past):
    n_keys_pad = kx.shape[1]
    const3 = lambda b: (0, 0, 0)
    return pl.pallas_call(
        functools.partial(_absorbed_attn_kernel, seq=seq, past=past, n_keys=n_keys),
        grid=(batch,),
        in_specs=[
            pl.BlockSpec((seq, MLA_HEADS * QK_PAD), lambda b: (b, 0)),
            pl.BlockSpec((1, n_keys_pad, LAT_KEY), lambda b: (b, 0, 0)),
            pl.BlockSpec((MLA_HEADS, QK_NOPE, KV_LORA), const3),
            pl.BlockSpec((MLA_HEADS, KV_LORA, V_HEAD), const3),
        ],
        out_specs=pl.BlockSpec((seq, D_MLA), lambda b: (b, 0)),
        out_shape=jax.ShapeDtypeStruct((batch * seq, D_MLA), BF16),
        scratch_shapes=[
            pltpu.VMEM((MLA_HEADS * seq, LAT_KEY), BF16),
            pltpu.VMEM((n_keys_pad, LAT_KEY), BF16),
        ],
        compiler_params=_cparams(("arbitrary",)),
        name="absorbed_attn",
    )(q, kx, w_uk_t, w_uv)


def _out_proj_kernel(oh_ref, oa_ref, x_ref, g0_ref, b0_ref, w_ref, g1_ref, b1_ref, y_ref):
    for r0 in range(0, x_ref.shape[0], ROW_SUB):
        rows = slice(r0, r0 + ROW_SUB)
        mix = jnp.dot(oh_ref[rows, :], w_ref[:D_HGRN, :], preferred_element_type=F32)
        mix = mix + jnp.dot(oa_ref[rows, :], w_ref[D_HGRN:, :], preferred_element_type=F32)
        h = _layer_norm(x_ref[rows, :], g0_ref[...], b0_ref[...])
        y_ref[rows, :] = _layer_norm(ALPHA * h + mix, g1_ref[...], b1_ref[...])


def _out_proj(oh, oa, x, ln_in_g, ln_in_b, w_out, ln1_g, ln1_b, tm):
    m = x.shape[0]
    row = lambda i: (i, 0)
    const = lambda i: (0, 0)
    vec = pl.BlockSpec((1, D_MODEL), const)
    return pl.pallas_call(
        _out_proj_kernel,
        grid=(m // tm,),
        in_specs=[
            pl.BlockSpec((tm, D_HGRN), row),
            pl.BlockSpec((tm, D_MLA), row),
            pl.BlockSpec((tm, D_MODEL), row),
            vec, vec,
            pl.BlockSpec((D_HGRN + D_MLA, D_MODEL), const),
            vec, vec,
        ],
        out_specs=pl.BlockSpec((tm, D_MODEL), row),
        out_shape=jax.ShapeDtypeStruct((m, D_MODEL), F32),
        compiler_params=_cparams(("arbitrary",)),
        name="out_proj",
    )(oh, oa, x, ln_in_g, ln_in_b, w_out, ln1_g, ln1_b)


def _ffn_up_kernel(x_ref, wu_ref, wg_ref, cw_ref, cb_ref, buf_ref, h_ref, cnew_ref, xb_ref, carry_ref, *,
                   spt, tiles_per_seq):
    i = pl.program_id(0)
    f = pl.program_id(1)
    tm = x_ref.shape[0]
    tf = wu_ref.shape[1]
    rows = tm // spt

    @pl.when(f == 0)
    def _():
        xb_ref[...] = x_ref[...].astype(BF16)

    if tiles_per_seq > 1:
        @pl.when(i % tiles_per_seq == 0)
        def _():
            carry_ref[f] = buf_ref[...]
        halo_ref = carry_ref.at[f]
    else:
        halo_ref = buf_ref

    xb = xb_ref[...]
    for c0 in range(0, tf, FFN_SUB):
        width = min(FFN_SUB, tf - c0)
        cs = slice(c0, c0 + width)
        r = lax.broadcasted_iota(jnp.int32, (spt, rows, width), 1)
        u = jnp.dot(xb, wu_ref[:, cs], preferred_element_type=F32).reshape(spt, rows, width)
        gate = jnp.dot(xb, wg_ref[:, cs], preferred_element_type=F32).reshape(spt, rows, width)
        halo = halo_ref[:, :, cs]
        prev1 = jnp.where(r == 0, halo[:, 1:2, :], pltpu.roll(u, 1, 1))
        prev2 = jnp.where(r == 0, halo[:, 0:1, :],
                          jnp.where(r == 1, halo[:, 1:2, :], pltpu.roll(u, 2, 1)))
        cw = cw_ref[:, cs]
        a = cb_ref[:, cs] + (prev2 * cw[0:1] + prev1 * cw[1:2] + u * cw[2:3])
        h_ref[:, cs] = (jax.nn.silu(a) * gate).astype(BF16).reshape(tm, width)
        tail = u[:, rows - (CONV_W - 1):, :]
        cnew_ref[:, :, cs] = tail
        if tiles_per_seq > 1:
            carry_ref[f, :, :, cs] = tail


def _ffn_up(x, w_up, w_gate, conv_w, conv_b, conv_buf, seq, tm, tf):
    m = x.shape[0]
    n_seq = m // seq
    if tm >= seq:
        spt, tiles_per_seq = tm // seq, 1
        seq_of = lambda i: i
    else:
        spt, tiles_per_seq = 1, seq // tm
        seq_of = lambda i: i // tiles_per_seq
    n_f = D_FF // tf
    hid, tails = pl.pallas_call(
        functools.partial(_ffn_up_kernel, spt=spt, tiles_per_seq=tiles_per_seq),
        grid=(m // tm, n_f),
        in_specs=[
            pl.BlockSpec((tm, D_MODEL), lambda i, f: (i, 0), pipeline_mode=pl.Buffered(1)),
            pl.BlockSpec((D_MODEL, tf), lambda i, f: (0, f)),
            pl.BlockSpec((D_MODEL, tf), lambda i, f: (0, f)),
            pl.BlockSpec((CONV_W, tf), lambda i, f: (0, f)),
            pl.BlockSpec((1, tf), lambda i, f: (0, f)),
            pl.BlockSpec((spt, CONV_W - 1, tf), lambda i, f: (seq_of(i), 0, f)),
        ],
        out_specs=[
            pl.BlockSpec((tm, tf), lambda i, f: (i, f)),
            pl.BlockSpec((spt, CONV_W - 1, tf), lambda i, f: (i, 0, f)),
        ],
        out_shape=[
            jax.ShapeDtypeStruct((m, D_FF), BF16),
            jax.ShapeDtypeStruct((n_seq * tiles_per_seq, CONV_W - 1, D_FF), F32),
        ],
        scratch_shapes=[
            pltpu.VMEM((tm, D_MODEL), BF16),
            pltpu.VMEM((n_f, spt, CONV_W - 1, tf), F32),
        ],
        compiler_params=_cparams(("arbitrary", "arbitrary")),
        name="ffn_up",
    )(x, w_up, w_gate, conv_w, conv_b, conv_buf)
    return hid, tails[tiles_per_seq - 1::tiles_per_seq]


def _ffn_down_kernel(h_ref, x_ref, w_ref, g_ref, b_ref, y_ref):
    for r0 in range(0, x_ref.shape[0], ROW_SUB):
        rows = slice(r0, r0 + ROW_SUB)
        ff = jnp.dot(h_ref[rows, :], w_ref[...], preferred_element_type=F32)
        y_ref[rows, :] = _layer_norm(ALPHA * x_ref[rows, :] + ff, g_ref[...], b_ref[...])


def _ffn_down(hid, x, w_down, ln2_g, ln2_b, tm):
    m = x.shape[0]
    row = lambda i: (i, 0)
    const = lambda i: (0, 0)
    return pl.pallas_call(
        _ffn_down_kernel,
        grid=(m // tm,),
        in_specs=[
            pl.BlockSpec((tm, D_FF), row),
            pl.BlockSpec((tm, D_MODEL), row),
            pl.BlockSpec((D_FF, D_MODEL), const, pipeline_mode=pl.Buffered(1)),
            pl.BlockSpec((1, D_MODEL), const),
            pl.BlockSpec((1, D_MODEL), const),
        ],
        out_specs=pl.BlockSpec((tm, D_MODEL), row),
        out_shape=jax.ShapeDtypeStruct((m, D_MODEL), F32),
        compiler_params=_cparams(("arbitrary",)),
        name="ffn_down",
    )(hid, x, w_down, ln2_g, ln2_b)


def _rope_tables(past, seq, rows):
    inv = ROPE_THETA ** (-np.arange(0, QK_ROPE, 2, dtype=np.float64) / QK_ROPE)
    ang = (past + np.arange(seq, dtype=np.float64))[:, None] * inv[None, :]
    cos, sin = np.cos(ang), np.sin(ang)
    zero = np.zeros((seq, ROPE_PAD - QK_ROPE))
    cos_t = np.concatenate([cos, cos, zero], axis=1).astype(np.float32)
    sin_t = np.concatenate([-sin, sin, zero], axis=1).astype(np.float32)
    reps = rows // seq
    return jnp.asarray(np.tile(cos_t, (reps, 1))), jnp.asarray(np.tile(sin_t, (reps, 1)))


def _swap_halves(w):
    half = QK_ROPE // 2
    return jnp.concatenate([w[..., half:], w[..., :half]], axis=-1)


def _prep_weights(w_in, w_q_b, w_kv_b, w_out, w_ffn_up, w_ffn_gate, w_ffn_down):
    w_in = w_in[0]
    c0 = 4 * D_HGRN
    w_h = w_in.astype(BF16)
    kr_cols = w_in[:, c0 + Q_LORA + KV_LORA:]
    zpad = jnp.zeros((D_MODEL, ROPE_PAD - QK_ROPE), F32)
    w_m = jnp.concatenate([w_in[:, c0:c0 + Q_LORA + KV_LORA], kr_cols, zpad,
                           _swap_halves(kr_cols), zpad], axis=1).astype(BF16)

    wq = w_q_b[0].reshape(Q_LORA, MLA_HEADS, QK_NOPE + QK_ROPE)
    zq = jnp.zeros((Q_LORA, MLA_HEADS, QK_PAD - QK_NOPE - QK_ROPE), F32)
    w_q = jnp.concatenate([wq, zq], axis=-1).reshape(Q_LORA, MLA_HEADS * QK_PAD).astype(BF16)
    zr = jnp.zeros((Q_LORA, MLA_HEADS, ROPE_PAD - QK_ROPE), F32)
    w_q_sw = jnp.concatenate([_swap_halves(wq[..., QK_NOPE:]), zr], axis=-1)
    w_q_sw = w_q_sw.reshape(Q_LORA, MLA_HEADS * ROPE_PAD).astype(BF16)

    wkv = w_kv_b[0].reshape(KV_LORA, MLA_HEADS, QK_NOPE + V_HEAD)
    w_k = wkv[..., :QK_NOPE].reshape(KV_LORA, MLA_HEADS * QK_NOPE).astype(BF16)
    w_v = wkv[..., QK_NOPE:].reshape(KV_LORA, D_MLA).astype(BF16)
    w_uk_t = wkv[..., :QK_NOPE].transpose(1, 2, 0).astype(BF16)
    w_uv = wkv[..., QK_NOPE:].transpose(1, 0, 2).astype(BF16)
    return dict(w_h=w_h, w_m=w_m, w_q=w_q, w_q_sw=w_q_sw, w_k=w_k, w_v=w_v, w_uk_t=w_uk_t, w_uv=w_uv,
                w_out=w_out[0].astype(BF16), w_up=w_ffn_up[0].astype(BF16),
                w_gate=w_ffn_gate[0].astype(BF16), w_down=w_ffn_down[0].astype(BF16))


def _row(a):
    return a.reshape(1, -1)


def _tile_plan(batch, seq, n_keys_pad):
    m = batch * seq
    tm = min(ROW_TILE, m)
    tm_big = min(BIG_ROW_TILE, m)
    assert m % tm_big == 0 and (tm_big % seq == 0 or seq % tm_big == 0)
    return dict(
        tm=tm,
        tm_in=tm_big,
        tm_ffn=tm_big,
        rows_hgrn=min(HGRN_STEP_ROWS, seq),
        tq=min(ATTN_TILE, seq),
        tk=ATTN_TILE if n_keys_pad % ATTN_TILE == 0 else n_keys_pad,
        tm_kv=ROW_TILE if n_keys_pad % ROW_TILE == 0 else n_keys_pad,
    )


def _encoder(x, s0, lat_past, kr_past, conv_buf, p, w, *, blk):
    batch, seq, _ = x.shape
    past = 0 if lat_past is None else lat_past.shape[1]
    m = batch * seq
    n_keys = past + seq
    key_pad = -n_keys % KEY_ALIGN
    n_keys_pad = n_keys + key_pad
    t = _tile_plan(batch, seq, n_keys_pad)
    x2 = x.reshape(m, D_MODEL)
    cos_k, sin_k = _rope_tables(past, seq, max(seq, t["tm_in"]))

    hg, cqn, lat_new, kr_new = _in_proj(x2, p["ln_in_g"], p["ln_in_b"], w["w_h"], w["w_m"],
                                        p["q_a_g"], p["kv_a_g"], cos_k, sin_k, t["tm_in"])
    o_h, s_new = _hgrn(hg, p["lb"], p["hgrn_norm_g"], s0, batch, seq, blk, t["rows_hgrn"])

    q = _q_proj(cqn, w["w_q"], w["w_q_sw"], cos_k, sin_k, t["tm"])
    lat3 = lat_new.reshape(batch, seq, KV_LORA)
    kr3 = kr_new.reshape(batch, seq, ROPE_PAD)
    if past or key_pad:
        parts_l, parts_r = [], []
        if past:
            parts_l.append(lat_past)
            parts_r.append(jnp.pad(kr_past, ((0, 0), (0, 0), (0, ROPE_PAD - QK_ROPE))))
        parts_l.append(lat3)
        parts_r.append(kr3)
        if key_pad:
            parts_l.append(jnp.zeros((batch, key_pad, KV_LORA), F32))
            parts_r.append(jnp.zeros((batch, key_pad, ROPE_PAD), F32))
        lat_all = jnp.concatenate(parts_l, axis=1)
        kr_all = jnp.concatenate(parts_r, axis=1)
    else:
        lat_all, kr_all = lat3, kr3
    if seq * MLA_HEADS <= ABSORB_MAX_ROWS:
        kx = jnp.concatenate([lat_all, kr_all], axis=-1).astype(BF16)
        o_a = _absorbed_attention(q, kx, w["w_uk_t"], w["w_uv"], batch, seq, n_keys, past)
    else:
        k_cat, v = _kv_proj(lat_all.reshape(batch * n_keys_pad, KV_LORA),
                            kr_all.reshape(batch * n_keys_pad, ROPE_PAD), w["w_k"], w["w_v"], t["tm_kv"])
        o_a = _attention(q.reshape(batch, seq, MLA_HEADS * QK_PAD),
                         k_cat.reshape(batch, n_keys_pad, MLA_HEADS * QK_PAD),
                         v.reshape(batch, n_keys_pad, D_MLA),
                         batch, seq, n_keys_pad, n_keys, past, t["tq"], t["tk"]).reshape(m, D_MLA)

    h1 = _out_proj(o_h, o_a, x2, p["ln_in_g"], p["ln_in_b"], w["w_out"],
                   p["ln1_g"], p["ln1_b"], t["tm"])
    hid, conv_new = _ffn_up(h1, w["w_up"], w["w_gate"], p["conv_w"], p["conv_b"], conv_buf, seq, t["tm_ffn"],
                            FFN_COL_TILE)
    y = _ffn_down(hid, h1, w["w_down"], p["ln2_g"], p["ln2_b"], t["tm"])
    return (y.reshape(batch, seq, D_MODEL), s_new[None], lat3[None],
            kr3[:, :, :QK_ROPE][None], conv_new[None])


def kernel(x_prompt, x_sample, cache_kv_latent, cache_k_rope, state_hgrn, cache_ffn_conv, lb_param, ln_in_g, ln_in_b, w_in, hgrn_norm_g, q_a_g, w_q_b, kv_a_g, w_kv_b, w_out, ln1_g, ln1_b, w_ffn_up, w_ffn_gate, conv_w, conv_b, w_ffn_down, ln2_g, ln2_b):
    w = _prep_weights(w_in, w_q_b, w_kv_b, w_out, w_ffn_up, w_ffn_gate, w_ffn_down)
    lbs = jnp.cumsum(jax.nn.softmax(lb_param.astype(F32), axis=0), axis=0)
    p = dict(ln_in_g=_row(ln_in_g), ln_in_b=_row(ln_in_b), lb=_row(lbs[0]),
             hgrn_norm_g=_row(hgrn_norm_g[0]), q_a_g=_row(q_a_g[0]), kv_a_g=_row(kv_a_g[0]),
             ln1_g=_row(ln1_g[0]), ln1_b=_row(ln1_b[0]), conv_w=conv_w[0], conv_b=_row(conv_b[0]),
             ln2_g=_row(ln2_g[0]), ln2_b=_row(ln2_b[0]))

    n_prompt = x_prompt.shape[0]
    conv0 = jnp.zeros((n_prompt, CONV_W - 1, D_FF), F32)
    y_p, s_p, lat_p, kr_p, conv_p = _encoder(x_prompt, None, None, None, conv0, p, w, blk=CHUNK)
    y_s, s_s, lat_s, kr_s, conv_s = _encoder(
        x_sample, state_hgrn[0], cache_kv_latent[0], cache_k_rope[0], cache_ffn_conv[0], p, w,
        blk=x_sample.shape[1])
    return (y_p, y_s, lat_p, kr_p, s_p, conv_p, lat_s, kr_s, s_s, conv_s)
```

```python
import functools

import numpy as np
import jax
import jax.numpy as jnp
from jax import lax
from jax.experimental import pallas as pl
from jax.experimental.pallas import tpu as pltpu

F32 = jnp.float32
BF16 = jnp.bfloat16

D_MODEL = 2048
CHUNK = 64
HGRN_HEADS = 8
HGRN_DK = 128
HGRN_DV = 128
D_HGRN = HGRN_HEADS * HGRN_DK
MLA_HEADS = 8
Q_LORA = 512
KV_LORA = 256
QK_NOPE = 128
QK_ROPE = 64
V_HEAD = 128
D_MLA = MLA_HEADS * V_HEAD
D_FF = 5632
CONV_W = 3
ROPE_THETA = 10000.0
LN_EPS = 1e-5
RMS_EPS = 1e-6
DEPTH = 1
ALPHA = (2.0 * DEPTH) ** 0.25

QK_PAD = 256
ROPE_PAD = 128
ROW_TILE = 512
BIG_ROW_TILE = 1024
FFN_ROW_TILE = 2048
HGRN_STEP_ROWS = 256
ABSORB_MAX_ROWS = 512
ATTN_TILE = 512
FFN_COL_TILE = 1408
KEY_ALIGN = 128
ROW_SUB = 256
ATTN_UNROLL = 4
FFN_SUB = 256
NEG_BIG = -1e30
LOG2_E = 1.4426950408889634
V7X_VMEM_LIMIT = 60 * 1024 * 1024

NT_DIMS = (((1,), (1,)), ((), ()))
TN_DIMS = (((0,), (0,)), ((), ()))


def _cparams(sem):
    return pltpu.CompilerParams(dimension_semantics=sem, vmem_limit_bytes=V7X_VMEM_LIMIT)


def _layer_norm(x, g, b):
    mu = jnp.mean(x, axis=-1, keepdims=True)
    xc = x - mu
    var = jnp.mean(xc * xc, axis=-1, keepdims=True)
    return xc * lax.rsqrt(var + LN_EPS) * g + b


def _rms_norm(x, g):
    return x * lax.rsqrt(jnp.mean(x * x, axis=-1, keepdims=True) + RMS_EPS) * g


N_HG_TILES = 4
W_MLA_COLS = Q_LORA + KV_LORA + 2 * ROPE_PAD


def _in_proj_kernel(x_ref, g_ref, b_ref, wh_ref, wm_ref, qg_ref, kvg_ref, cos_ref, sin_ref,
                    hg_ref, cq_ref, lat_ref, kr_ref, hb_ref):
    j = pl.program_id(1)

    @pl.when(j == 0)
    def _():
        for r0 in range(0, x_ref.shape[0], ROW_SUB):
            rows = slice(r0, r0 + ROW_SUB)
            hb = _layer_norm(x_ref[rows, :], g_ref[...], b_ref[...]).astype(BF16)
            hb_ref[rows, :] = hb
            hg_ref[rows, :] = jnp.dot(hb, wh_ref[...], preferred_element_type=F32)

    @pl.when(jnp.logical_and(j > 0, j < N_HG_TILES))
    def _():
        hg_ref[...] = jnp.dot(hb_ref[...], wh_ref[...], preferred_element_type=F32)

    @pl.when(j == N_HG_TILES)
    def _():
        p = jnp.dot(hb_ref[...], wm_ref[...], preferred_element_type=F32)
        cq = p[:, :Q_LORA]
        ckv = p[:, Q_LORA:Q_LORA + KV_LORA]
        kr = p[:, Q_LORA + KV_LORA:Q_LORA + KV_LORA + ROPE_PAD]
        kr_sw = p[:, Q_LORA + KV_LORA + ROPE_PAD:]
        cq_ref[...] = _rms_norm(cq, qg_ref[...]).astype(BF16)
        lat_ref[...] = _rms_norm(ckv, kvg_ref[...])
        kr_ref[...] = kr * cos_ref[...] + kr_sw * sin_ref[...]


def _in_proj(x, ln_g, ln_b, w_h, w_m, q_a_g, kv_a_g, cos_k, sin_k, tm):
    m = x.shape[0]
    n_pos_tiles = cos_k.shape[0] // tm
    grid = (m // tm, N_HG_TILES + 1)
    row = lambda i, j: (i, 0)
    const = lambda i, j: (0, 0)
    pos = lambda i, j: (i % n_pos_tiles, 0)
    hg_col = lambda i, j: (i, jnp.minimum(j, N_HG_TILES - 1))
    return pl.pallas_call(
        _in_proj_kernel,
        grid=grid,
        in_specs=[
            pl.BlockSpec((tm, D_MODEL), row),
            pl.BlockSpec((1, D_MODEL), const),
            pl.BlockSpec((1, D_MODEL), const),
            pl.BlockSpec((D_MODEL, D_HGRN), lambda i, j: (0, jnp.minimum(j, N_HG_TILES - 1))),
            pl.BlockSpec((D_MODEL, W_MLA_COLS), const),
            pl.BlockSpec((1, Q_LORA), const),
            pl.BlockSpec((1, KV_LORA), const),
            pl.BlockSpec((tm, ROPE_PAD), pos),
            pl.BlockSpec((tm, ROPE_PAD), pos),
        ],
        out_specs=[
            pl.BlockSpec((tm, D_HGRN), hg_col),
            pl.BlockSpec((tm, Q_LORA), row),
            pl.BlockSpec((tm, KV_LORA), row),
            pl.BlockSpec((tm, ROPE_PAD), row),
        ],
        out_shape=[
            jax.ShapeDtypeStruct((m, N_HG_TILES * D_HGRN), F32),
            jax.ShapeDtypeStruct((m, Q_LORA), BF16),
            jax.ShapeDtypeStruct((m, KV_LORA), F32),
            jax.ShapeDtypeStruct((m, ROPE_PAD), F32),
        ],
        scratch_shapes=[pltpu.VMEM((tm, D_MODEL), BF16)],
        compiler_params=_cparams(("arbitrary", "arbitrary")),
        name="in_proj",
    )(x, ln_g, ln_b, w_h, w_m, q_a_g, kv_a_g, cos_k, sin_k)


def _cumsum_rows(x):
    c = x.shape[0]
    row = lax.broadcasted_iota(jnp.int32, (c, 1), 0)
    shift = 1
    while shift < c:
        x = x + jnp.where(row >= shift, pltpu.roll(x, shift, 0), 0.0)
        shift *= 2
    return x


def _split_level(c):
    t = lax.broadcasted_iota(jnp.int32, (c, c), 0)
    s = lax.broadcasted_iota(jnp.int32, (c, c), 1)
    x = jnp.bitwise_xor(t, s)
    lvl = jnp.full((c, c), -1, jnp.int32)
    for bit in range(c.bit_length() - 1):
        lvl = jnp.where(x >= (1 << bit), bit, lvl)
    return jnp.where(t < s, -2, lvl)


def _hgrn_head_block(q, fpre, v, lb, st, lvl):
    c = q.shape[0]
    f = lb + (1.0 - lb) * jax.nn.sigmoid(fpre)
    g = jnp.log(f)
    k = 1.0 - f
    b = _cumsum_rows(g)
    row = lax.broadcasted_iota(jnp.int32, (c, 1), 0)

    a = jnp.where(lvl == -1, jnp.sum(q * k, axis=-1, keepdims=True), 0.0)
    for level in range(c.bit_length() - 1):
        half = 1 << level
        pos = row % (2 * half)
        is_query = pos >= half
        if level == 0:
            decay = jnp.where(is_query, f, 1.0)
        else:
            if 2 * half < 8:
                ref = b
                for p in range(2 * half):
                    d = half - 1 - p
                    if d != 0:
                        ref = jnp.where(pos == p, pltpu.roll(b, (-d) % c, 0), ref)
            else:
                ref = jnp.concatenate(
                    [jnp.broadcast_to(b[j + half - 1:j + half], (2 * half, b.shape[1]))
                     for j in range(0, c, 2 * half)], axis=0)
            decay = jnp.exp(-jnp.abs(b - ref))
        z = (jnp.where(is_query, q, k) * decay).astype(BF16)
        gram = lax.dot_general(z, z, NT_DIMS, preferred_element_type=F32)
        a = jnp.where(lvl == level, gram, a)

    vb = v.astype(BF16)
    o = jnp.dot(a.astype(BF16), vb, preferred_element_type=F32)
    o = o + lax.dot_general((q * jnp.exp(b)).astype(BF16), st.astype(BF16), NT_DIMS,
                            preferred_element_type=F32)
    b_last = b[c - 1:c]
    k_dec = (k * jnp.exp(b_last - b)).astype(BF16)
    st_new = jnp.exp(b_last) * st + lax.dot_general(vb, k_dec, TN_DIMS, preferred_element_type=F32)
    return o, st_new


def _hgrn_kernel(*refs, c, has_s0):
    if has_s0:
        q_ref, f_ref, i_ref, g_ref, lb_ref, ng_ref, s0_ref, o_ref, s_ref, st_ref = refs
    else:
        q_ref, f_ref, i_ref, g_ref, lb_ref, ng_ref, o_ref, s_ref, st_ref = refs
        s0_ref = None
    t = pl.program_id(1)

    @pl.when(t == 0)
    def _():
        for h in range(HGRN_HEADS):
            if has_s0:
                st_ref[h] = s0_ref[0, h].T
            else:
                st_ref[h] = jnp.zeros((HGRN_DV, HGRN_DK), F32)

    lvl = _split_level(c)

    def block(i, _):
        rows = pl.ds(pl.multiple_of(i * c, c), c)
        for h in range(HGRN_HEADS):
            cols = slice(h * HGRN_DK, (h + 1) * HGRN_DK)
            o, st_new = _hgrn_head_block(q_ref[rows, cols], f_ref[rows, cols], i_ref[rows, cols],
                                         lb_ref[:, cols], st_ref[h], lvl)
            st_ref[h] = st_new
            o = _rms_norm(o, ng_ref[:, cols]) * jax.nn.silu(g_ref[rows, cols])
            o_ref[rows, cols] = o.astype(BF16)
        return 0

    lax.fori_loop(0, q_ref.shape[0] // c, block, 0)

    @pl.when(t == pl.num_programs(1) - 1)
    def _():
        for h in range(HGRN_HEADS):
            s_ref[0, h] = st_ref[h].T


def _hgrn(hg, lb, norm_g, s0, batch, seq, c, rows):
    nt = seq // rows
    has_s0 = s0 is not None
    col = lambda n: (lambda b, t: (b * nt + t, n))
    const = lambda b, t: (0, 0)
    state = lambda b, t: (b, 0, 0, 0)
    in_specs = [pl.BlockSpec((rows, D_HGRN), col(n)) for n in range(4)]
    in_specs += [pl.BlockSpec((1, D_HGRN), const), pl.BlockSpec((1, D_HGRN), const)]
    args = [hg, hg, hg, hg, lb, norm_g]
    if has_s0:
        in_specs.append(pl.BlockSpec((1, HGRN_HEADS, HGRN_DK, HGRN_DV), state))
        args.append(s0)
    return pl.pallas_call(
        functools.partial(_hgrn_kernel, c=c, has_s0=has_s0),
        grid=(batch, nt),
        in_specs=in_specs,
        out_specs=[
            pl.BlockSpec((rows, D_HGRN), lambda b, t: (b * nt + t, 0)),
            pl.BlockSpec((1, HGRN_HEADS, HGRN_DK, HGRN_DV), state),
        ],
        out_shape=[
            jax.ShapeDtypeStruct((batch * seq, D_HGRN), BF16),
            jax.ShapeDtypeStruct((batch, HGRN_HEADS, HGRN_DK, HGRN_DV), F32),
        ],
        scratch_shapes=[pltpu.VMEM((HGRN_HEADS, HGRN_DV, HGRN_DK), F32)],
        compiler_params=_cparams(("arbitrary", "arbitrary")),
        name="hgrn",
    )(*args)


def _q_proj_kernel(cq_ref, wq_ref, wsw_ref, cos_ref, sin_ref, q_ref, *, scale):
    cq = cq_ref[...]
    q_lin = jnp.dot(cq, wq_ref[...], preferred_element_type=F32)
    q_sw = jnp.dot(cq, wsw_ref[...], preferred_element_type=F32)
    cos = cos_ref[...]
    sin = sin_ref[...]
    for h in range(MLA_HEADS):
        base = h * QK_PAD
        q_ref[:, base:base + QK_NOPE] = (q_lin[:, base:base + QK_NOPE] * scale).astype(BF16)
        rot = (q_lin[:, base + QK_NOPE:base + QK_PAD] * cos
               + q_sw[:, h * ROPE_PAD:(h + 1) * ROPE_PAD] * sin)
        q_ref[:, base + QK_NOPE:base + QK_PAD] = (rot * scale).astype(BF16)


def _q_proj(cq, w_q, w_q_sw, cos_k, sin_k, tm):
    m = cq.shape[0]
    n_pos_tiles = cos_k.shape[0] // tm
    scale = LOG2_E * (QK_NOPE + QK_ROPE) ** -0.5
    row = lambda i: (i, 0)
    const = lambda i: (0, 0)
    pos = lambda i: (i % n_pos_tiles, 0)
    return pl.pallas_call(
        functools.partial(_q_proj_kernel, scale=scale),
        grid=(m // tm,),
        in_specs=[
            pl.BlockSpec((tm, Q_LORA), row),
            pl.BlockSpec((Q_LORA, MLA_HEADS * QK_PAD), const),
            pl.BlockSpec((Q_LORA, MLA_HEADS * ROPE_PAD), const),
            pl.BlockSpec((tm, ROPE_PAD), pos),
            pl.BlockSpec((tm, ROPE_PAD), pos),
        ],
        out_specs=pl.BlockSpec((tm, MLA_HEADS * QK_PAD), row),
        out_shape=jax.ShapeDtypeStruct((m, MLA_HEADS * QK_PAD), BF16),
        compiler_params=_cparams(("arbitrary",)),
        name="q_proj",
    )(cq, w_q, w_q_sw, cos_k, sin_k)


def _kv_proj_kernel(lat_ref, kr_ref, wk_ref, wv_ref, k_ref, v_ref):
    lat = lat_ref[...].astype(BF16)
    kn = jnp.dot(lat, wk_ref[...], preferred_element_type=F32)
    v_ref[...] = jnp.dot(lat, wv_ref[...], preferred_element_type=F32).astype(BF16)
    kr = kr_ref[...].astype(BF16)
    for h in range(MLA_HEADS):
        base = h * QK_PAD
        k_ref[:, base:base + QK_NOPE] = kn[:, h * QK_NOPE:(h + 1) * QK_NOPE].astype(BF16)
        k_ref[:, base + QK_NOPE:base + QK_PAD] = kr


def _kv_proj(lat, kr_pad, w_k, w_v, tm):
    m = lat.shape[0]
    row = lambda i: (i, 0)
    const = lambda i: (0, 0)
    return pl.pallas_call(
        _kv_proj_kernel,
        grid=(m // tm,),
        in_specs=[
            pl.BlockSpec((tm, KV_LORA), row),
            pl.BlockSpec((tm, ROPE_PAD), row),
            pl.BlockSpec((KV_LORA, MLA_HEADS * QK_NOPE), const),
            pl.BlockSpec((KV_LORA, D_MLA), const),
        ],
        out_specs=[
            pl.BlockSpec((tm, MLA_HEADS * QK_PAD), row),
            pl.BlockSpec((tm, D_MLA), row),
        ],
        out_shape=[
            jax.ShapeDtypeStruct((m, MLA_HEADS * QK_PAD), BF16),
            jax.ShapeDtypeStruct((m, D_MLA), BF16),
        ],
        compiler_params=_cparams(("arbitrary",)),
        name="kv_proj",
    )(lat, kr_pad, w_k, w_v)


def _attn_kernel(tile_q_ref, tile_k_ref, q_ref, k_ref, v_ref, o_ref,
                 vx_ref, rel_ref, sa_ref, sb_ref, m_ref, acc_ref, *, tq, tk, past, n_keys, n_plain, n_tiles):
    @pl.when(jnp.logical_and(pl.program_id(0) == 0, pl.program_id(1) == 0))
    def _():
        rel_ref[...] = (lax.broadcasted_iota(jnp.int32, (tq, tk), 1) // CHUNK
                        - lax.broadcasted_iota(jnp.int32, (tq, tk), 0) // CHUNK)

    vx_ref[:, :V_HEAD] = v_ref[0]
    vx_ref[:, V_HEAD:] = jnp.ones((vx_ref.shape[0], V_HEAD), BF16)
    m_ref[...] = jnp.full(m_ref.shape, NEG_BIG, F32)
    acc_ref[...] = jnp.zeros(acc_ref.shape, F32)
    k_lane = lax.broadcasted_iota(jnp.int32, (1, tk), 1)

    def origin(t):
        return pl.multiple_of(tile_q_ref[t] * tq, tq), pl.multiple_of(tile_k_ref[t] * tk, tk)

    def scores(t, s_ref):
        qs, ks = origin(t)
        s_ref[...] = lax.dot_general(q_ref[0, pl.ds(qs, tq), :], k_ref[0, pl.ds(ks, tk), :], NT_DIMS,
                                     preferred_element_type=F32)

    def softmax_pv(t, s_ref, masked):
        qs, ks = origin(t)
        s = s_ref[...]
        if masked:
            pad = jnp.where(k_lane < n_keys - ks, 0, 1 << 20)
            visible = rel_ref[...] + pad <= (past + qs - ks) // CHUNK
            s = jnp.where(visible, s, NEG_BIG)
        rows = pl.ds(qs, tq)
        m = m_ref[rows, :]
        m_new = jnp.maximum(m, jnp.max(s, axis=-1, keepdims=True))
        p = jnp.exp2(s - m_new).astype(BF16)
        pv = jnp.dot(p, vx_ref[pl.ds(ks, tk), :], preferred_element_type=F32)
        acc_ref[rows, :] = jnp.exp2(m - m_new) * acc_ref[rows, :] + pv
        m_ref[rows, :] = m_new

    bufs = (sa_ref, sb_ref)

    def step(t, parity, masked):
        scores(t + 1, bufs[1 - parity])
        softmax_pv(t, bufs[parity], masked)

    def run(t0, t1, masked):
        t = t0
        if t < t1 and t % 2 == 1:
            step(t, 1, masked)
            t += 1
        trips = (t1 - t) // ATTN_UNROLL
        if trips > 0:
            def body(i, _, base=t):
                tt = base + ATTN_UNROLL * i
                for u in range(ATTN_UNROLL):
                    step(tt + u, u % 2, masked)
                return 0
            lax.fori_loop(0, trips, body, 0)
            t += ATTN_UNROLL * trips
        while t < t1:
            step(t, t % 2, masked)
            t += 1

    scores(0, sa_ref)
    run(0, min(n_plain, n_tiles - 1), False)
    run(min(n_plain, n_tiles - 1), n_tiles - 1, True)
    last = n_tiles - 1
    softmax_pv(last, bufs[last % 2], last >= n_plain)
    acc = acc_ref[...]
    o_ref[0] = (acc[:, :V_HEAD] / acc[:, V_HEAD:]).astype(BF16)


def _attention_tiles(seq, n_keys_pad, n_keys, past, tq, tk):
    plain, masked = [], []
    for j in range(n_keys_pad // tk):
        for i in range(seq // tq):
            q_lo, k_lo = past + i * tq, j * tk
            q_hi, k_hi = q_lo + tq - 1, k_lo + tk - 1
            if k_lo // CHUNK > q_hi // CHUNK or k_lo >= n_keys:
                continue
            if k_hi // CHUNK <= q_lo // CHUNK and k_hi < n_keys:
                plain.append((i, j))
            else:
                masked.append((i, j))
    return plain, masked


def _attention(q, k, v, batch, seq, n_keys_pad, n_keys, past, tq, tk):
    assert past % CHUNK == 0 and tk % CHUNK == 0 and (tq % CHUNK == 0 or tq == seq)
    plain, masked = _attention_tiles(seq, n_keys_pad, n_keys, past, tq, tk)
    tiles = plain + masked
    tile_q = jnp.asarray([t[0] for t in tiles], jnp.int32)
    tile_k = jnp.asarray([t[1] for t in tiles], jnp.int32)
    smem = pl.BlockSpec(memory_space=pltpu.SMEM)
    return pl.pallas_call(
        functools.partial(_attn_kernel, tq=tq, tk=tk, past=past, n_keys=n_keys,
                          n_plain=len(plain), n_tiles=len(tiles)),
        grid=(batch, MLA_HEADS),
        in_specs=[
            smem, smem,
            pl.BlockSpec((1, seq, QK_PAD), lambda b, h: (b, 0, h)),
            pl.BlockSpec((1, n_keys_pad, QK_PAD), lambda b, h: (b, 0, h)),
            pl.BlockSpec((1, n_keys_pad, V_HEAD), lambda b, h: (b, 0, h)),
        ],
        out_specs=pl.BlockSpec((1, seq, V_HEAD), lambda b, h: (b, 0, h)),
        out_shape=jax.ShapeDtypeStruct((batch, seq, D_MLA), BF16),
        scratch_shapes=[
            pltpu.VMEM((n_keys_pad, 2 * V_HEAD), BF16),
            pltpu.VMEM((tq, tk), jnp.int32),
            pltpu.VMEM((tq, tk), F32),
            pltpu.VMEM((tq, tk), F32),
            pltpu.VMEM((seq, 1), F32),
            pltpu.VMEM((seq, 2 * V_HEAD), F32),
        ],
        compiler_params=_cparams(("arbitrary", "arbitrary")),
        name="attn",
    )(tile_q, tile_k, q, k, v)


LAT_KEY = KV_LORA + ROPE_PAD


def _absorbed_attn_kernel(q_ref, kx_ref, wuk_ref, wuv_ref, o_ref, qx_ref, vx_ref, *, seq, past, n_keys):
    for h in range(MLA_HEADS):
        rows = slice(h * seq, (h + 1) * seq)
        q_lat = jnp.dot(q_ref[:, h * QK_PAD:h * QK_PAD + QK_NOPE], wuk_ref[h], preferred_element_type=F32)
        qx_ref[rows, :KV_LORA] = q_lat.astype(BF16)
        qx_ref[rows, KV_LORA:] = q_ref[:, h * QK_PAD + QK_NOPE:(h + 1) * QK_PAD]
    kx = kx_ref[0]
    s = lax.dot_general(qx_ref[...], kx, NT_DIMS, preferred_element_type=F32)
    q_chunk = (past + lax.broadcasted_iota(jnp.int32, (s.shape[0], 1), 0) % seq) // CHUNK
    k_idx = lax.broadcasted_iota(jnp.int32, (1, s.shape[1]), 1)
    visible = jnp.logical_and(k_idx // CHUNK <= q_chunk, k_idx < n_keys)
    s = jnp.where(visible, s, NEG_BIG)
    p = jnp.exp2(s - jnp.max(s, axis=-1, keepdims=True)).astype(BF16)
    vx_ref[:, :KV_LORA] = kx[:, :KV_LORA]
    vx_ref[:, KV_LORA:] = jnp.ones((vx_ref.shape[0], ROPE_PAD), BF16)
    o_ext = jnp.dot(p, vx_ref[...], preferred_element_type=F32)
    denom = o_ext[:, KV_LORA:]
    o_lat = (o_ext[:, :KV_LORA] / jnp.concatenate([denom] * (KV_LORA // ROPE_PAD), axis=1)).astype(BF16)
    for h in range(MLA_HEADS):
        o_ref[:, h * V_HEAD:(h + 1) * V_HEAD] = jnp.dot(
            o_lat[h * seq:(h + 1) * seq], wuv_ref[h], preferred_element_type=F32).astype(BF16)


def _absorbed_attention(q, kx, w_uk_t, w_uv, batch, seq, n_keys, past):
    n_keys_pad = kx.shape[1]
    const3 = lambda b: (0, 0, 0)
    return pl.pallas_call(
        functools.partial(_absorbed_attn_kernel, seq=seq, past=past, n_keys=n_keys),
        grid=(batch,),
        in_specs=[
            pl.BlockSpec((seq, MLA_HEADS * QK_PAD), lambda b: (b, 0)),
            pl.BlockSpec((1, n_keys_pad, LAT_KEY), lambda b: (b, 0, 0)),
            pl.BlockSpec((MLA_HEADS, QK_NOPE, KV_LORA), const3),
            pl.BlockSpec((MLA_HEADS, KV_LORA, V_HEAD), const3),
        ],
        out_specs=pl.BlockSpec((seq, D_MLA), lambda b: (b, 0)),
        out_shape=jax.ShapeDtypeStruct((batch * seq, D_MLA), BF16),
        scratch_shapes=[
            pltpu.VMEM((MLA_HEADS * seq, LAT_KEY), BF16),
            pltpu.VMEM((n_keys_pad, LAT_KEY), BF16),
        ],
        compiler_params=_cparams(("arbitrary",)),
        name="absorbed_attn",
    )(q, kx, w_uk_t, w_uv)


def _out_proj_kernel(oh_ref, oa_ref, x_ref, g0_ref, b0_ref, w_ref, g1_ref, b1_ref, y_ref, yb_ref):
    for r0 in range(0, x_ref.shape[0], ROW_SUB):
        rows = slice(r0, r0 + ROW_SUB)
        mix = jnp.dot(oh_ref[rows, :], w_ref[:D_HGRN, :], preferred_element_type=F32)
        mix = mix + jnp.dot(oa_ref[rows, :], w_ref[D_HGRN:, :], preferred_element_type=F32)
        h = _layer_norm(x_ref[rows, :], g0_ref[...], b0_ref[...])
        y = _layer_norm(ALPHA * h + mix, g1_ref[...], b1_ref[...])
        y_ref[rows, :] = y
        yb_ref[rows, :] = y.astype(BF16)


def _out_proj(oh, oa, x, ln_in_g, ln_in_b, w_out, ln1_g, ln1_b, tm):
    m = x.shape[0]
    row = lambda i: (i, 0)
    const = lambda i: (0, 0)
    vec = pl.BlockSpec((1, D_MODEL), const)
    return pl.pallas_call(
        _out_proj_kernel,
        grid=(m // tm,),
        in_specs=[
            pl.BlockSpec((tm, D_HGRN), row),
            pl.BlockSpec((tm, D_MLA), row),
            pl.BlockSpec((tm, D_MODEL), row),
            vec, vec,
            pl.BlockSpec((D_HGRN + D_MLA, D_MODEL), const),
            vec, vec,
        ],
        out_specs=[pl.BlockSpec((tm, D_MODEL), row), pl.BlockSpec((tm, D_MODEL), row)],
        out_shape=[jax.ShapeDtypeStruct((m, D_MODEL), F32), jax.ShapeDtypeStruct((m, D_MODEL), BF16)],
        compiler_params=_cparams(("arbitrary",)),
        name="out_proj",
    )(oh, oa, x, ln_in_g, ln_in_b, w_out, ln1_g, ln1_b)


def _ffn_up_kernel(x_ref, wu_ref, wg_ref, cw_ref, cb_ref, buf_ref, h_ref, cnew_ref, carry_ref, *,
                   spt, tiles_per_seq):
    i = pl.program_id(0)
    f = pl.program_id(1)
    tm = x_ref.shape[0]
    tf = wu_ref.shape[1]
    rows = tm // spt

    if tiles_per_seq > 1:
        @pl.when(i % tiles_per_seq == 0)
        def _():
            carry_ref[f] = buf_ref[...]
        halo_ref = carry_ref.at[f]
    else:
        halo_ref = buf_ref

    xb = x_ref[...]
    for c0 in range(0, tf, FFN_SUB):
        width = min(FFN_SUB, tf - c0)
        cs = slice(c0, c0 + width)
        r = lax.broadcasted_iota(jnp.int32, (spt, rows, width), 1)
        if 2 * width <= FFN_SUB:
            both = jnp.dot(xb, jnp.concatenate([wu_ref[:, cs], wg_ref[:, cs]], axis=1),
                           preferred_element_type=F32)
            u, gate = both[:, :width], both[:, width:]
        else:
            u = jnp.dot(xb, wu_ref[:, cs], preferred_element_type=F32)
            gate = jnp.dot(xb, wg_ref[:, cs], preferred_element_type=F32)
        u = u.reshape(spt, rows, width)
        gate = gate.reshape(spt, rows, width)
        halo = halo_ref[:, :, cs]
        prev1 = jnp.where(r == 0, halo[:, 1:2, :], pltpu.roll(u, 1, 1))
        prev2 = jnp.where(r == 0, halo[:, 0:1, :],
                          jnp.where(r == 1, halo[:, 1:2, :], pltpu.roll(u, 2, 1)))
        cw = cw_ref[:, cs]
        a = cb_ref[:, cs] + (prev2 * cw[0:1] + prev1 * cw[1:2] + u * cw[2:3])
        h_ref[:, cs] = (jax.nn.silu(a) * gate).astype(BF16).reshape(tm, width)
        tail = u[:, rows - (CONV_W - 1):, :]
        cnew_ref[:, :, cs] = tail
        if tiles_per_seq > 1:
            carry_ref[f, :, :, cs] = tail


def _ffn_up(x, w_up, w_gate, conv_w, conv_b, conv_buf, seq, tm, tf):
    m = x.shape[0]
    n_seq = m // seq
    if tm >= seq:
        spt, tiles_per_seq = tm // seq, 1
        seq_of = lambda i: i
    else:
        spt, tiles_per_seq = 1, seq // tm
        seq_of = lambda i: i // tiles_per_seq
    n_f = D_FF // tf
    hid, tails = pl.pallas_call(
        functools.partial(_ffn_up_kernel, spt=spt, tiles_per_seq=tiles_per_seq),
        grid=(m // tm, n_f),
        in_specs=[
            pl.BlockSpec((tm, D_MODEL), lambda i, f: (i, 0), pipeline_mode=pl.Buffered(1)),
            pl.BlockSpec((D_MODEL, tf), lambda i, f: (0, f)),
            pl.BlockSpec((D_MODEL, tf), lambda i, f: (0, f)),
            pl.BlockSpec((CONV_W, tf), lambda i, f: (0, f)),
            pl.BlockSpec((1, tf), lambda i, f: (0, f)),
            pl.BlockSpec((spt, CONV_W - 1, tf), lambda i, f: (seq_of(i), 0, f)),
        ],
        out_specs=[
            pl.BlockSpec((tm, tf), lambda i, f: (i, f)),
            pl.BlockSpec((spt, CONV_W - 1, tf), lambda i, f: (i, 0, f)),
        ],
        out_shape=[
            jax.ShapeDtypeStruct((m, D_FF), BF16),
            jax.ShapeDtypeStruct((n_seq * tiles_per_seq, CONV_W - 1, D_FF), F32),
        ],
        scratch_shapes=[pltpu.VMEM((n_f, spt, CONV_W - 1, tf), F32)],
        compiler_params=_cparams(("arbitrary", "arbitrary")),
        name="ffn_up",
    )(x, w_up, w_gate, conv_w, conv_b, conv_buf)
    return hid, tails[tiles_per_seq - 1::tiles_per_seq]


def _ffn_down_kernel(h_ref, x_ref, w_ref, g_ref, b_ref, y_ref):
    for r0 in range(0, x_ref.shape[0], ROW_SUB):
        rows = slice(r0, r0 + ROW_SUB)
        ff = jnp.dot(h_ref[rows, :], w_ref[...], preferred_element_type=F32)
        y_ref[rows, :] = _layer_norm(ALPHA * x_ref[rows, :] + ff, g_ref[...], b_ref[...])


def _ffn_down(hid, x, w_down, ln2_g, ln2_b, tm):
    m = x.shape[0]
    row = lambda i: (i, 0)
    const = lambda i: (0, 0)
    return pl.pallas_call(
        _ffn_down_kernel,
        grid=(m // tm,),
        in_specs=[
            pl.BlockSpec((tm, D_FF), row),
            pl.BlockSpec((tm, D_MODEL), row),
            pl.BlockSpec((D_FF, D_MODEL), const, pipeline_mode=pl.Buffered(1)),
            pl.BlockSpec((1, D_MODEL), const),
            pl.BlockSpec((1, D_MODEL), const),
        ],
        out_specs=pl.BlockSpec((tm, D_MODEL), row),
        out_shape=jax.ShapeDtypeStruct((m, D_MODEL), F32),
        compiler_params=_cparams(("arbitrary",)),
        name="ffn_down",
    )(hid, x, w_down, ln2_g, ln2_b)


def _rope_tables(past, seq, rows):
    inv = ROPE_THETA ** (-np.arange(0, QK_ROPE, 2, dtype=np.float64) / QK_ROPE)
    ang = (past + np.arange(seq, dtype=np.float64))[:, None] * inv[None, :]
    cos, sin = np.cos(ang), np.sin(ang)
    zero = np.zeros((seq, ROPE_PAD - QK_ROPE))
    cos_t = np.concatenate([cos, cos, zero], axis=1).astype(np.float32)
    sin_t = np.concatenate([-sin, sin, zero], axis=1).astype(np.float32)
    reps = rows // seq
    return jnp.asarray(np.tile(cos_t, (reps, 1))), jnp.asarray(np.tile(sin_t, (reps, 1)))


def _swap_halves(w):
    half = QK_ROPE // 2
    return jnp.concatenate([w[..., half:], w[..., :half]], axis=-1)


def _prep_weights(w_in, w_q_b, w_kv_b, w_out, w_ffn_up, w_ffn_gate, w_ffn_down):
    w_in = w_in[0]
    c0 = 4 * D_HGRN
    w_h = w_in.astype(BF16)
    kr_cols = w_in[:, c0 + Q_LORA + KV_LORA:]
    zpad = jnp.zeros((D_MODEL, ROPE_PAD - QK_ROPE), F32)
    w_m = jnp.concatenate([w_in[:, c0:c0 + Q_LORA + KV_LORA], kr_cols, zpad,
                           _swap_halves(kr_cols), zpad], axis=1).astype(BF16)

    wq = w_q_b[0].reshape(Q_LORA, MLA_HEADS, QK_NOPE + QK_ROPE)
    zq = jnp.zeros((Q_LORA, MLA_HEADS, QK_PAD - QK_NOPE - QK_ROPE), F32)
    w_q = jnp.concatenate([wq, zq], axis=-1).reshape(Q_LORA, MLA_HEADS * QK_PAD).astype(BF16)
    zr = jnp.zeros((Q_LORA, MLA_HEADS, ROPE_PAD - QK_ROPE), F32)
    w_q_sw = jnp.concatenate([_swap_halves(wq[..., QK_NOPE:]), zr], axis=-1)
    w_q_sw = w_q_sw.reshape(Q_LORA, MLA_HEADS * ROPE_PAD).astype(BF16)

    wkv = w_kv_b[0].reshape(KV_LORA, MLA_HEADS, QK_NOPE + V_HEAD)
    w_k = wkv[..., :QK_NOPE].reshape(KV_LORA, MLA_HEADS * QK_NOPE).astype(BF16)
    w_v = wkv[..., QK_NOPE:].reshape(KV_LORA, D_MLA).astype(BF16)
    w_uk_t = wkv[..., :QK_NOPE].transpose(1, 2, 0).astype(BF16)
    w_uv = wkv[..., QK_NOPE:].transpose(1, 0, 2).astype(BF16)
    return dict(w_h=w_h, w_m=w_m, w_q=w_q, w_q_sw=w_q_sw, w_k=w_k, w_v=w_v, w_uk_t=w_uk_t, w_uv=w_uv,
                w_out=w_out[0].astype(BF16), w_up=w_ffn_up[0].astype(BF16),
                w_gate=w_ffn_gate[0].astype(BF16), w_down=w_ffn_down[0].astype(BF16))


def _row(a):
    return a.reshape(1, -1)


def _tile_plan(batch, seq, n_keys_pad):
    m = batch * seq
    tm = min(ROW_TILE, m)
    tm_big = min(BIG_ROW_TILE, m)
    tm_ffn = min(FFN_ROW_TILE, m)
    assert m % tm_big == 0 and m % tm_ffn == 0 and (tm_ffn % seq == 0 or seq % tm_ffn == 0)
    return dict(
        tm=tm,
        tm_in=tm_big,
        tm_ffn=tm_ffn,
        rows_hgrn=min(HGRN_STEP_ROWS, seq),
        tq=min(ATTN_TILE, seq),
        tk=ATTN_TILE if n_keys_pad % ATTN_TILE == 0 else n_keys_pad,
        tm_kv=ROW_TILE if n_keys_pad % ROW_TILE == 0 else n_keys_pad,
    )


def _encoder(x, s0, lat_past, kr_past, conv_buf, p, w, *, blk):
    batch, seq, _ = x.shape
    past = 0 if lat_past is None else lat_past.shape[1]
    m = batch * seq
    n_keys = past + seq
    key_pad = -n_keys % KEY_ALIGN
    n_keys_pad = n_keys + key_pad
    t = _tile_plan(batch, seq, n_keys_pad)
    x2 = x.reshape(m, D_MODEL)
    cos_k, sin_k = _rope_tables(past, seq, max(seq, t["tm_in"]))

    hg, cqn, lat_new, kr_new = _in_proj(x2, p["ln_in_g"], p["ln_in_b"], w["w_h"], w["w_m"],
                                        p["q_a_g"], p["kv_a_g"], cos_k, sin_k, t["tm_in"])
    o_h, s_new = _hgrn(hg, p["lb"], p["hgrn_norm_g"], s0, batch, seq, blk, t["rows_hgrn"])

    q = _q_proj(cqn, w["w_q"], w["w_q_sw"], cos_k, sin_k, t["tm"])
    lat3 = lat_new.reshape(batch, seq, KV_LORA)
    kr3 = kr_new.reshape(batch, seq, ROPE_PAD)
    if past or key_pad:
        parts_l, parts_r = [], []
        if past:
            parts_l.append(lat_past)
            parts_r.append(jnp.pad(kr_past, ((0, 0), (0, 0), (0, ROPE_PAD - QK_ROPE))))
        parts_l.append(lat3)
        parts_r.append(kr3)
        if key_pad:
            parts_l.append(jnp.zeros((batch, key_pad, KV_LORA), F32))
            parts_r.append(jnp.zeros((batch, key_pad, ROPE_PAD), F32))
        lat_all = jnp.concatenate(parts_l, axis=1)
        kr_all = jnp.concatenate(parts_r, axis=1)
    else:
        lat_all, kr_all = lat3, kr3
    if seq * MLA_HEADS <= ABSORB_MAX_ROWS:
        kx = jnp.concatenate([lat_all, kr_all], axis=-1).astype(BF16)
        o_a = _absorbed_attention(q, kx, w["w_uk_t"], w["w_uv"], batch, seq, n_keys, past)
    else:
        k_cat, v = _kv_proj(lat_all.reshape(batch * n_keys_pad, KV_LORA),
                            kr_all.reshape(batch * n_keys_pad, ROPE_PAD), w["w_k"], w["w_v"], t["tm_kv"])
        o_a = _attention(q.reshape(batch, seq, MLA_HEADS * QK_PAD),
                         k_cat.reshape(batch, n_keys_pad, MLA_HEADS * QK_PAD),
                         v.reshape(batch, n_keys_pad, D_MLA),
                         batch, seq, n_keys_pad, n_keys, past, t["tq"], t["tk"]).reshape(m, D_MLA)

    h1, h1_bf16 = _out_proj(o_h, o_a, x2, p["ln_in_g"], p["ln_in_b"], w["w_out"],
                            p["ln1_g"], p["ln1_b"], t["tm"])
    hid, conv_new = _ffn_up(h1_bf16, w["w_up"], w["w_gate"], p["conv_w"], p["conv_b"], conv_buf, seq, t["tm_ffn"],
                            FFN_COL_TILE)
    y = _ffn_down(hid, h1, w["w_down"], p["ln2_g"], p["ln2_b"], t["tm"])
    return (y.reshape(batch, seq, D_MODEL), s_new[None], lat3[None],
            kr3[:, :, :QK_ROPE][None], conv_new[None])


def kernel(x_prompt, x_sample, cache_kv_latent, cache_k_rope, state_hgrn, cache_ffn_conv, lb_param, ln_in_g, ln_in_b, w_in, hgrn_norm_g, q_a_g, w_q_b, kv_a_g, w_kv_b, w_out, ln1_g, ln1_b, w_ffn_up, w_ffn_gate, conv_w, conv_b, w_ffn_down, ln2_g, ln2_b):
    w = _prep_weights(w_in, w_q_b, w_kv_b, w_out, w_ffn_up, w_ffn_gate, w_ffn_down)
    lbs = jnp.cumsum(jax.nn.softmax(lb_param.astype(F32), axis=0), axis=0)
    p = dict(ln_in_g=_row(ln_in_g), ln_in_b=_row(ln_in_b), lb=_row(lbs[0]),
             hgrn_norm_g=_row(hgrn_norm_g[0]), q_a_g=_row(q_a_g[0]), kv_a_g=_row(kv_a_g[0]),
             ln1_g=_row(ln1_g[0]), ln1_b=_row(ln1_b[0]), conv_w=conv_w[0], conv_b=_row(conv_b[0]),
             ln2_g=_row(ln2_g[0]), ln2_b=_row(ln2_b[0]))

    n_prompt = x_prompt.shape[0]
    conv0 = jnp.zeros((n_prompt, CONV_W - 1, D_FF), F32)
    y_p, s_p, lat_p, kr_p, conv_p = _encoder(x_prompt, None, None, None, conv0, p, w, blk=CHUNK)
    y_s, s_s, lat_s, kr_s, conv_s = _encoder(
        x_sample, state_hgrn[0], cache_kv_latent[0], cache_k_rope[0], cache_ffn_conv[0], p, w,
        blk=x_sample.shape[1])
    return (y_p, y_s, lat_p, kr_p, s_p, conv_p, lat_s, kr_s, s_s, conv_s)
```

```python
import functools

import numpy as np
import jax
import jax.numpy as jnp
from jax import lax
from jax.experimental import pallas as pl
from jax.experimental.pallas import tpu as pltpu

F32 = jnp.float32
BF16 = jnp.bfloat16

D_MODEL = 2048
CHUNK = 64
HGRN_HEADS = 8
HGRN_DK = 128
HGRN_DV = 128
D_HGRN = HGRN_HEADS * HGRN_DK
MLA_HEADS = 8
Q_LORA = 512
KV_LORA = 256
QK_NOPE = 128
QK_ROPE = 64
V_HEAD = 128
D_MLA = MLA_HEADS * V_HEAD
D_FF = 5632
CONV_W = 3
ROPE_THETA = 10000.0
LN_EPS = 1e-5
RMS_EPS = 1e-6
DEPTH = 1
ALPHA = (2.0 * DEPTH) ** 0.25

QK_PAD = 256
ROPE_PAD = 128
ROW_TILE = 512
BIG_ROW_TILE = 1024
FFN_ROW_TILE = 1024
HGRN_STEP_ROWS = 256
ABSORB_MAX_ROWS = 512
ATTN_TILE = 512
FFN_COL_TILE = 1408
KEY_ALIGN = 128
ROW_SUB = 256
ATTN_UNROLL = 4
FFN_SUB = 256
NEG_BIG = -1e30
LOG2_E = 1.4426950408889634
V7X_VMEM_LIMIT = 60 * 1024 * 1024

NT_DIMS = (((1,), (1,)), ((), ()))
TN_DIMS = (((0,), (0,)), ((), ()))


def _cparams(sem):
    return pltpu.CompilerParams(dimension_semantics=sem, vmem_limit_bytes=V7X_VMEM_LIMIT)


def _layer_norm(x, g, b):
    mu = jnp.mean(x, axis=-1, keepdims=True)
    xc = x - mu
    var = jnp.mean(xc * xc, axis=-1, keepdims=True)
    return xc * lax.rsqrt(var + LN_EPS) * g + b


def _rms_norm(x, g):
    return x * lax.rsqrt(jnp.mean(x * x, axis=-1, keepdims=True) + RMS_EPS) * g


N_HG_TILES = 4
W_MLA_COLS = Q_LORA + KV_LORA + 2 * ROPE_PAD


def _in_proj_kernel(x_ref, g_ref, b_ref, wh_ref, wm_ref, qg_ref, kvg_ref, cos_ref, sin_ref,
                    hg_ref, cq_ref, lat_ref, kr_ref, hb_ref):
    j = pl.program_id(1)

    @pl.when(j == 0)
    def _():
        for r0 in range(0, x_ref.shape[0], ROW_SUB):
            rows = slice(r0, r0 + ROW_SUB)
            hb = _layer_norm(x_ref[rows, :], g_ref[...], b_ref[...]).astype(BF16)
            hb_ref[rows, :] = hb
            hg_ref[rows, :] = jnp.dot(hb, wh_ref[...], preferred_element_type=F32)

    @pl.when(jnp.logical_and(j > 0, j < N_HG_TILES))
    def _():
        hg_ref[...] = jnp.dot(hb_ref[...], wh_ref[...], preferred_element_type=F32)

    @pl.when(j == N_HG_TILES)
    def _():
        p = jnp.dot(hb_ref[...], wm_ref[...], preferred_element_type=F32)
        cq = p[:, :Q_LORA]
        ckv = p[:, Q_LORA:Q_LORA + KV_LORA]
        kr = p[:, Q_LORA + KV_LORA:Q_LORA + KV_LORA + ROPE_PAD]
        kr_sw = p[:, Q_LORA + KV_LORA + ROPE_PAD:]
        cq_ref[...] = _rms_norm(cq, qg_ref[...]).astype(BF16)
        lat_ref[...] = _rms_norm(ckv, kvg_ref[...])
        kr_ref[...] = kr * cos_ref[...] + kr_sw * sin_ref[...]


def _in_proj(x, ln_g, ln_b, w_h, w_m, q_a_g, kv_a_g, cos_k, sin_k, tm):
    m = x.shape[0]
    n_pos_tiles = cos_k.shape[0] // tm
    grid = (m // tm, N_HG_TILES + 1)
    row = lambda i, j: (i, 0)
    const = lambda i, j: (0, 0)
    pos = lambda i, j: (i % n_pos_tiles, 0)
    hg_col = lambda i, j: (i, jnp.minimum(j, N_HG_TILES - 1))
    return pl.pallas_call(
        _in_proj_kernel,
        grid=grid,
        in_specs=[
            pl.BlockSpec((tm, D_MODEL), row),
            pl.BlockSpec((1, D_MODEL), const),
            pl.BlockSpec((1, D_MODEL), const),
            pl.BlockSpec((D_MODEL, D_HGRN), lambda i, j: (0, jnp.minimum(j, N_HG_TILES - 1))),
            pl.BlockSpec((D_MODEL, W_MLA_COLS), const),
            pl.BlockSpec((1, Q_LORA), const),
            pl.BlockSpec((1, KV_LORA), const),
            pl.BlockSpec((tm, ROPE_PAD), pos),
            pl.BlockSpec((tm, ROPE_PAD), pos),
        ],
        out_specs=[
            pl.BlockSpec((tm, D_HGRN), hg_col),
            pl.BlockSpec((tm, Q_LORA), row),
            pl.BlockSpec((tm, KV_LORA), row),
            pl.BlockSpec((tm, ROPE_PAD), row),
        ],
        out_shape=[
            jax.ShapeDtypeStruct((m, N_HG_TILES * D_HGRN), F32),
            jax.ShapeDtypeStruct((m, Q_LORA), BF16),
            jax.ShapeDtypeStruct((m, KV_LORA), F32),
            jax.ShapeDtypeStruct((m, ROPE_PAD), F32),
        ],
        scratch_shapes=[pltpu.VMEM((tm, D_MODEL), BF16)],
        compiler_params=_cparams(("arbitrary", "arbitrary")),
        name="in_proj",
    )(x, ln_g, ln_b, w_h, w_m, q_a_g, kv_a_g, cos_k, sin_k)


def _cumsum_rows(x):
    c = x.shape[0]
    row = lax.broadcasted_iota(jnp.int32, (c, 1), 0)
    shift = 1
    while shift < c:
        x = x + jnp.where(row >= shift, pltpu.roll(x, shift, 0), 0.0)
        shift *= 2
    return x


def _split_level(c):
    t = lax.broadcasted_iota(jnp.int32, (c, c), 0)
    s = lax.broadcasted_iota(jnp.int32, (c, c), 1)
    x = jnp.bitwise_xor(t, s)
    lvl = jnp.full((c, c), -1, jnp.int32)
    for bit in range(c.bit_length() - 1):
        lvl = jnp.where(x >= (1 << bit), bit, lvl)
    return jnp.where(t < s, -2, lvl)


def _hgrn_head_block(q, fpre, v, lb, st, lvl):
    c = q.shape[0]
    f = lb + (1.0 - lb) * jax.nn.sigmoid(fpre)
    g = jnp.log(f)
    k = 1.0 - f
    b = _cumsum_rows(g)
    row = lax.broadcasted_iota(jnp.int32, (c, 1), 0)

    a = jnp.where(lvl == -1, jnp.sum(q * k, axis=-1, keepdims=True), 0.0)
    for level in range(c.bit_length() - 1):
        half = 1 << level
        pos = row % (2 * half)
        is_query = pos >= half
        if level == 0:
            decay = jnp.where(is_query, f, 1.0)
        else:
            if 2 * half < 8:
                ref = b
                for p in range(2 * half):
                    d = half - 1 - p
                    if d != 0:
                        ref = jnp.where(pos == p, pltpu.roll(b, (-d) % c, 0), ref)
            else:
                ref = jnp.concatenate(
                    [jnp.broadcast_to(b[j + half - 1:j + half], (2 * half, b.shape[1]))
                     for j in range(0, c, 2 * half)], axis=0)
            decay = jnp.exp(-jnp.abs(b - ref))
        z = (jnp.where(is_query, q, k) * decay).astype(BF16)
        gram = lax.dot_general(z, z, NT_DIMS, preferred_element_type=F32)
        a = jnp.where(lvl == level, gram, a)

    vb = v.astype(BF16)
    o = jnp.dot(a.astype(BF16), vb, preferred_element_type=F32)
    o = o + lax.dot_general((q * jnp.exp(b)).astype(BF16), st.astype(BF16), NT_DIMS,
                            preferred_element_type=F32)
    b_last = b[c - 1:c]
    k_dec = (k * jnp.exp(b_last - b)).astype(BF16)
    st_new = jnp.exp(b_last) * st + lax.dot_general(vb, k_dec, TN_DIMS, preferred_element_type=F32)
    return o, st_new


def _hgrn_kernel(*refs, c, has_s0):
    if has_s0:
        q_ref, f_ref, i_ref, g_ref, lb_ref, ng_ref, s0_ref, o_ref, s_ref, st_ref = refs
    else:
        q_ref, f_ref, i_ref, g_ref, lb_ref, ng_ref, o_ref, s_ref, st_ref = refs
        s0_ref = None
    t = pl.program_id(1)

    @pl.when(t == 0)
    def _():
        for h in range(HGRN_HEADS):
            if has_s0:
                st_ref[h] = s0_ref[0, h].T
            else:
                st_ref[h] = jnp.zeros((HGRN_DV, HGRN_DK), F32)

    lvl = _split_level(c)

    def block(i, _):
        rows = pl.ds(pl.multiple_of(i * c, c), c)
        for h in range(HGRN_HEADS):
            cols = slice(h * HGRN_DK, (h + 1) * HGRN_DK)
            o, st_new = _hgrn_head_block(q_ref[rows, cols], f_ref[rows, cols], i_ref[rows, cols],
                                         lb_ref[:, cols], st_ref[h], lvl)
            st_ref[h] = st_new
            o = _rms_norm(o, ng_ref[:, cols]) * jax.nn.silu(g_ref[rows, cols])
            o_ref[rows, cols] = o.astype(BF16)
        return 0

    lax.fori_loop(0, q_ref.shape[0] // c, block, 0)

    @pl.when(t == pl.num_programs(1) - 1)
    def _():
        for h in range(HGRN_HEADS):
            s_ref[0, h] = st_ref[h].T


def _hgrn(hg, lb, norm_g, s0, batch, seq, c, rows):
    nt = seq // rows
    has_s0 = s0 is not None
    col = lambda n: (lambda b, t: (b * nt + t, n))
    const = lambda b, t: (0, 0)
    state = lambda b, t: (b, 0, 0, 0)
    in_specs = [pl.BlockSpec((rows, D_HGRN), col(n)) for n in range(4)]
    in_specs += [pl.BlockSpec((1, D_HGRN), const), pl.BlockSpec((1, D_HGRN), const)]
    args = [hg, hg, hg, hg, lb, norm_g]
    if has_s0:
        in_specs.append(pl.BlockSpec((1, HGRN_HEADS, HGRN_DK, HGRN_DV), state))
        args.append(s0)
    return pl.pallas_call(
        functools.partial(_hgrn_kernel, c=c, has_s0=has_s0),
        grid=(batch, nt),
        in_specs=in_specs,
        out_specs=[
            pl.BlockSpec((rows, D_HGRN), lambda b, t: (b * nt + t, 0)),
            pl.BlockSpec((1, HGRN_HEADS, HGRN_DK, HGRN_DV), state),
        ],
        out_shape=[
            jax.ShapeDtypeStruct((batch * seq, D_HGRN), BF16),
            jax.ShapeDtypeStruct((batch, HGRN_HEADS, HGRN_DK, HGRN_DV), F32),
        ],
        scratch_shapes=[pltpu.VMEM((HGRN_HEADS, HGRN_DV, HGRN_DK), F32)],
        compiler_params=_cparams(("arbitrary", "arbitrary")),
        name="hgrn",
    )(*args)


def _q_proj_kernel(cq_ref, wq_ref, wsw_ref, cos_ref, sin_ref, q_ref, *, scale):
    cq = cq_ref[...]
    q_lin = jnp.dot(cq, wq_ref[...], preferred_element_type=F32)
    q_sw = jnp.dot(cq, wsw_ref[...], preferred_element_type=F32)
    cos = cos_ref[...]
    sin = sin_ref[...]
    for h in range(MLA_HEADS):
        base = h * QK_PAD
        q_ref[:, base:base + QK_NOPE] = (q_lin[:, base:base + QK_NOPE] * scale).astype(BF16)
        rot = (q_lin[:, base + QK_NOPE:base + QK_PAD] * cos
               + q_sw[:, h * ROPE_PAD:(h + 1) * ROPE_PAD] * sin)
        q_ref[:, base + QK_NOPE:base + QK_PAD] = (rot * scale).astype(BF16)


def _q_proj(cq, w_q, w_q_sw, cos_k, sin_k, tm):
    m = cq.shape[0]
    n_pos_tiles = cos_k.shape[0] // tm
    scale = LOG2_E * (QK_NOPE + QK_ROPE) ** -0.5
    row = lambda i: (i, 0)
    const = lambda i: (0, 0)
    pos = lambda i: (i % n_pos_tiles, 0)
    return pl.pallas_call(
        functools.partial(_q_proj_kernel, scale=scale),
        grid=(m // tm,),
        in_specs=[
            pl.BlockSpec((tm, Q_LORA), row),
            pl.BlockSpec((Q_LORA, MLA_HEADS * QK_PAD), const),
            pl.BlockSpec((Q_LORA, MLA_HEADS * ROPE_PAD), const),
            pl.BlockSpec((tm, ROPE_PAD), pos),
            pl.BlockSpec((tm, ROPE_PAD), pos),
        ],
        out_specs=pl.BlockSpec((tm, MLA_HEADS * QK_PAD), row),
        out_shape=jax.ShapeDtypeStruct((m, MLA_HEADS * QK_PAD), BF16),
        compiler_params=_cparams(("arbitrary",)),
        name="q_proj",
    )(cq, w_q, w_q_sw, cos_k, sin_k)


def _kv_proj_kernel(lat_ref, kr_ref, wk_ref, wv_ref, k_ref, v_ref):
    lat = lat_ref[...].astype(BF16)
    kn = jnp.dot(lat, wk_ref[...], preferred_element_type=F32)
    v_ref[...] = jnp.dot(lat, wv_ref[...], preferred_element_type=F32).astype(BF16)
    kr = kr_ref[...].astype(BF16)
    for h in range(MLA_HEADS):
        base = h * QK_PAD
        k_ref[:, base:base + QK_NOPE] = kn[:, h * QK_NOPE:(h + 1) * QK_NOPE].astype(BF16)
        k_ref[:, base + QK_NOPE:base + QK_PAD] = kr


def _kv_proj(lat, kr_pad, w_k, w_v, tm):
    m = lat.shape[0]
    row = lambda i: (i, 0)
    const = lambda i: (0, 0)
    return pl.pallas_call(
        _kv_proj_kernel,
        grid=(m // tm,),
        in_specs=[
            pl.BlockSpec((tm, KV_LORA), row),
            pl.BlockSpec((tm, ROPE_PAD), row),
            pl.BlockSpec((KV_LORA, MLA_HEADS * QK_NOPE), const),
            pl.BlockSpec((KV_LORA, D_MLA), const),
        ],
        out_specs=[
            pl.BlockSpec((tm, MLA_HEADS * QK_PAD), row),
            pl.BlockSpec((tm, D_MLA), row),
        ],
        out_shape=[
            jax.ShapeDtypeStruct((m, MLA_HEADS * QK_PAD), BF16),
            jax.ShapeDtypeStruct((m, D_MLA), BF16),
        ],
        compiler_params=_cparams(("arbitrary",)),
        name="kv_proj",
    )(lat, kr_pad, w_k, w_v)


def _attn_kernel(tile_q_ref, tile_k_ref, q_ref, k_ref, v_ref, o_ref,
                 vx_ref, rel_ref, sa_ref, sb_ref, m0_ref, acc0_ref, m1_ref, acc1_ref, *,
                 tq, tk, past, n_keys, phases, n_tiles):
    state = ((m0_ref, acc0_ref), (m1_ref, acc1_ref))
    @pl.when(jnp.logical_and(pl.program_id(0) == 0, pl.program_id(1) == 0))
    def _():
        rel_ref[...] = (lax.broadcasted_iota(jnp.int32, (tq, tk), 1) // CHUNK
                        - lax.broadcasted_iota(jnp.int32, (tq, tk), 0) // CHUNK)

    vx_ref[:, :V_HEAD] = v_ref[0]
    vx_ref[:, V_HEAD:] = jnp.ones((vx_ref.shape[0], V_HEAD), BF16)
    for m_ref, acc_ref in state:
        m_ref[...] = jnp.full(m_ref.shape, NEG_BIG, F32)
        acc_ref[...] = jnp.zeros(acc_ref.shape, F32)
    k_lane = lax.broadcasted_iota(jnp.int32, (1, tk), 1)

    def origin(t):
        return pl.multiple_of(tile_q_ref[t] * tq, tq), pl.multiple_of(tile_k_ref[t] * tk, tk)

    def scores(t, s_ref):
        qs, ks = origin(t)
        s_ref[...] = lax.dot_general(q_ref[0, pl.ds(qs, tq), :], k_ref[0, pl.ds(ks, tk), :], NT_DIMS,
                                     preferred_element_type=F32)

    def softmax_pv(t, s_ref, masked, chain):
        qs, ks = origin(t)
        m_ref, acc_ref = state[chain]
        s = s_ref[...]
        if masked:
            pad = jnp.where(k_lane < n_keys - ks, 0, 1 << 20)
            visible = rel_ref[...] + pad <= (past + qs - ks) // CHUNK
            s = jnp.where(visible, s, NEG_BIG)
        rows = pl.ds(pl.multiple_of((tile_q_ref[t] // 2) * tq, tq), tq)
        m = m_ref[rows, :]
        m_new = jnp.maximum(m, jnp.max(s, axis=-1, keepdims=True))
        p = jnp.exp2(s - m_new).astype(BF16)
        pv = jnp.dot(p, vx_ref[pl.ds(ks, tk), :], preferred_element_type=F32)
        acc_ref[rows, :] = jnp.exp2(m - m_new) * acc_ref[rows, :] + pv
        m_ref[rows, :] = m_new

    bufs = (sa_ref, sb_ref)

    def step(t, parity, masked, chain):
        scores(t + 1, bufs[1 - parity])
        softmax_pv(t, bufs[parity], masked, chain)

    def run(t0, t1, masked, chains):
        t = t0
        if t < t1 and t % 2 == 1:
            step(t, 1, masked, chains[(t - t0) % 2])
            t += 1
        trips = (t1 - t) // ATTN_UNROLL
        if trips > 0:
            def body(i, _, base=t):
                tt = base + ATTN_UNROLL * i
                for u in range(ATTN_UNROLL):
                    step(tt + u, u % 2, masked, chains[(base - t0 + u) % 2])
                return 0
            lax.fori_loop(0, trips, body, 0)
            t += ATTN_UNROLL * trips
        while t < t1:
            step(t, t % 2, masked, chains[(t - t0) % 2])
            t += 1

    scores(0, sa_ref)
    last = n_tiles - 1
    for t0, t1, masked, chains in phases:
        run(t0, min(t1, last), masked, chains)
    t0, _, masked, chains = phases[-1]
    softmax_pv(last, bufs[last % 2], masked, chains[(last - t0) % 2])
    for qi in range(q_ref.shape[1] // tq):
        acc = state[qi % 2][1][(qi // 2) * tq:(qi // 2 + 1) * tq, :]
        o_ref[0, qi * tq:(qi + 1) * tq, :] = (acc[:, :V_HEAD] / acc[:, V_HEAD:]).astype(BF16)


def _attention_schedule(seq, n_keys_pad, n_keys, past, tq, tk):
    groups = {(c, msk): [] for c in (0, 1) for msk in (False, True)}
    for j in range(n_keys_pad // tk):
        for i in range(seq // tq):
            q_lo, k_lo = past + i * tq, j * tk
            q_hi, k_hi = q_lo + tq - 1, k_lo + tk - 1
            if k_lo // CHUNK > q_hi // CHUNK or k_lo >= n_keys:
                continue
            full = k_hi // CHUNK <= q_lo // CHUNK and k_hi < n_keys
            groups[(i % 2, not full)].append((i, j))
    tiles, phases = [], []
    for msk in (False, True):
        even, odd = groups[(0, msk)], groups[(1, msk)]
        n = min(len(even), len(odd))
        mixed = [pair for both in zip(even[:n], odd[:n]) for pair in both]
        rest, chain = (even[n:], 0) if len(even) > n else (odd[n:], 1)
        for part, chains in ((mixed, (0, 1)), (rest, (chain, chain))):
            if part:
                phases.append((len(tiles), len(tiles) + len(part), msk, chains))
                tiles += part
    return tiles, tuple(phases)


def _attention(q, k, v, batch, seq, n_keys_pad, n_keys, past, tq, tk):
    assert past % CHUNK == 0 and tk % CHUNK == 0 and (tq % CHUNK == 0 or tq == seq)
    tiles, phases = _attention_schedule(seq, n_keys_pad, n_keys, past, tq, tk)
    tile_q = jnp.asarray([t[0] for t in tiles], jnp.int32)
    tile_k = jnp.asarray([t[1] for t in tiles], jnp.int32)
    smem = pl.BlockSpec(memory_space=pltpu.SMEM)
    n_q = seq // tq
    rows_even, rows_odd = (n_q + 1) // 2 * tq, max(n_q // 2, 1) * tq
    return pl.pallas_call(
        functools.partial(_attn_kernel, tq=tq, tk=tk, past=past, n_keys=n_keys,
                          phases=phases, n_tiles=len(tiles)),
        grid=(batch, MLA_HEADS),
        in_specs=[
            smem, smem,
            pl.BlockSpec((1, seq, QK_PAD), lambda b, h: (b, 0, h)),
            pl.BlockSpec((1, n_keys_pad, QK_PAD), lambda b, h: (b, 0, h)),
            pl.BlockSpec((1, n_keys_pad, V_HEAD), lambda b, h: (b, 0, h)),
        ],
        out_specs=pl.BlockSpec((1, seq, V_HEAD), lambda b, h: (b, 0, h)),
        out_shape=jax.ShapeDtypeStruct((batch, seq, D_MLA), BF16),
        scratch_shapes=[
            pltpu.VMEM((n_keys_pad, 2 * V_HEAD), BF16),
            pltpu.VMEM((tq, tk), jnp.int32),
            pltpu.VMEM((tq, tk), F32),
            pltpu.VMEM((tq, tk), F32),
            pltpu.VMEM((rows_even, 1), F32),
            pltpu.VMEM((rows_even, 2 * V_HEAD), F32),
            pltpu.VMEM((rows_odd, 1), F32),
            pltpu.VMEM((rows_odd, 2 * V_HEAD), F32),
        ],
        compiler_params=_cparams(("arbitrary", "arbitrary")),
        name="attn",
    )(tile_q, tile_k, q, k, v)


LAT_KEY = KV_LORA + ROPE_PAD


def _absorbed_attn_kernel(q_ref, kx_ref, wuk_ref, wuv_ref, o_ref, qx_ref, vx_ref, *, seq, past, n_keys):
    for h in range(MLA_HEADS):
        rows = slice(h * seq, (h + 1) * seq)
        q_lat = jnp.dot(q_ref[:, h * QK_PAD:h * QK_PAD + QK_NOPE], wuk_ref[h], preferred_element_type=F32)
        qx_ref[rows, :KV_LORA] = q_lat.astype(BF16)
        qx_ref[rows, KV_LORA:] = q_ref[:, h * QK_PAD + QK_NOPE:(h + 1) * QK_PAD]
    kx = kx_ref[0]
    s = lax.dot_general(qx_ref[...], kx, NT_DIMS, preferred_element_type=F32)
    q_chunk = (past + lax.broadcasted_iota(jnp.int32, (s.shape[0], 1), 0) % seq) // CHUNK
    k_idx = lax.broadcasted_iota(jnp.int32, (1, s.shape[1]), 1)
    visible = jnp.logical_and(k_idx // CHUNK <= q_chunk, k_idx < n_keys)
    s = jnp.where(visible, s, NEG_BIG)
    p = jnp.exp2(s - jnp.max(s, axis=-1, keepdims=True)).astype(BF16)
    vx_ref[:, :KV_LORA] = kx[:, :KV_LORA]
    vx_ref[:, KV_LORA:] = jnp.ones((vx_ref.shape[0], ROPE_PAD), BF16)
    o_ext = jnp.dot(p, vx_ref[...], preferred_element_type=F32)
    denom = o_ext[:, KV_LORA:]
    o_lat = (o_ext[:, :KV_LORA] / jnp.concatenate([denom] * (KV_LORA // ROPE_PAD), axis=1)).astype(BF16)
    for h in range(MLA_HEADS):
        o_ref[:, h * V_HEAD:(h + 1) * V_HEAD] = jnp.dot(
            o_lat[h * seq:(h + 1) * seq], wuv_ref[h], preferred_element_type=F32).astype(BF16)


def _absorbed_attention(q, kx, w_uk_t, w_uv, batch, seq, n_keys, past):
    n_keys_pad = kx.shape[1]
    const3 = lambda b: (0, 0, 0)
    return pl.pallas_call(
        functools.partial(_absorbed_attn_kernel, seq=seq, past=past, n_keys=n_keys),
        grid=(batch,),
        in_specs=[
            pl.BlockSpec((seq, MLA_HEADS * QK_PAD), lambda b: (b, 0)),
            pl.BlockSpec((1, n_keys_pad, LAT_KEY), lambda b: (b, 0, 0)),
            pl.BlockSpec((MLA_HEADS, QK_NOPE, KV_LORA), const3),
            pl.BlockSpec((MLA_HEADS, KV_LORA, V_HEAD), const3),
        ],
        out_specs=pl.BlockSpec((seq, D_MLA), lambda b: (b, 0)),
        out_shape=jax.ShapeDtypeStruct((batch * seq, D_MLA), BF16),
        scratch_shapes=[
            pltpu.VMEM((MLA_HEADS * seq, LAT_KEY), BF16),
            pltpu.VMEM((n_keys_pad, LAT_KEY), BF16),
        ],
        compiler_params=_cparams(("arbitrary",)),
        name="absorbed_attn",
    )(q, kx, w_uk_t, w_uv)


def _out_proj_kernel(oh_ref, oa_ref, x_ref, g0_ref, b0_ref, w_ref, g1_ref, b1_ref, y_ref, yb_ref):
    for r0 in range(0, x_ref.shape[0], ROW_SUB):
        rows = slice(r0, r0 + ROW_SUB)
        mix = jnp.dot(oh_ref[rows, :], w_ref[:D_HGRN, :], preferred_element_type=F32)
        mix = mix + jnp.dot(oa_ref[rows, :], w_ref[D_HGRN:, :], preferred_element_type=F32)
        h = _layer_norm(x_ref[rows, :], g0_ref[...], b0_ref[...])
        y = _layer_norm(ALPHA * h + mix, g1_ref[...], b1_ref[...])
        y_ref[rows, :] = y
        yb_ref[rows, :] = y.astype(BF16)


def _out_proj(oh, oa, x, ln_in_g, ln_in_b, w_out, ln1_g, ln1_b, tm):
    m = x.shape[0]
    row = lambda i: (i, 0)
    const = lambda i: (0, 0)
    vec = pl.BlockSpec((1, D_MODEL), const)
    return pl.pallas_call(
        _out_proj_kernel,
        grid=(m // tm,),
        in_specs=[
            pl.BlockSpec((tm, D_HGRN), row),
            pl.BlockSpec((tm, D_MLA), row),
            pl.BlockSpec((tm, D_MODEL), row),
            vec, vec,
            pl.BlockSpec((D_HGRN + D_MLA, D_MODEL), const),
            vec, vec,
        ],
        out_specs=[pl.BlockSpec((tm, D_MODEL), row), pl.BlockSpec((tm, D_MODEL), row)],
        out_shape=[jax.ShapeDtypeStruct((m, D_MODEL), F32), jax.ShapeDtypeStruct((m, D_MODEL), BF16)],
        compiler_params=_cparams(("arbitrary",)),
        name="out_proj",
    )(oh, oa, x, ln_in_g, ln_in_b, w_out, ln1_g, ln1_b)


def _ffn_up_kernel(x_ref, wu_ref, wg_ref, cw_ref, cb_ref, buf_ref, h_ref, cnew_ref, carry_ref, *,
                   spt, tiles_per_seq):
    i = pl.program_id(1)
    tm = x_ref.shape[0]
    tf = wu_ref.shape[1]
    rows = tm // spt

    if tiles_per_seq > 1:
        @pl.when(i % tiles_per_seq == 0)
        def _():
            carry_ref[...] = buf_ref[...]
        halo_ref = carry_ref
    else:
        halo_ref = buf_ref

    xb = x_ref[...]
    for c0 in range(0, tf, FFN_SUB):
        width = min(FFN_SUB, tf - c0)
        cs = slice(c0, c0 + width)
        r = lax.broadcasted_iota(jnp.int32, (spt, rows, width), 1)
        if 2 * width <= FFN_SUB:
            both = jnp.dot(xb, jnp.concatenate([wu_ref[:, cs], wg_ref[:, cs]], axis=1),
                           preferred_element_type=F32)
            u, gate = both[:, :width], both[:, width:]
        else:
            u = jnp.dot(xb, wu_ref[:, cs], preferred_element_type=F32)
            gate = jnp.dot(xb, wg_ref[:, cs], preferred_element_type=F32)
        u = u.reshape(spt, rows, width)
        gate = gate.reshape(spt, rows, width)
        halo = halo_ref[:, :, cs]
        prev1 = jnp.where(r == 0, halo[:, 1:2, :], pltpu.roll(u, 1, 1))
        prev2 = jnp.where(r == 0, halo[:, 0:1, :],
                          jnp.where(r == 1, halo[:, 1:2, :], pltpu.roll(u, 2, 1)))
        cw = cw_ref[:, cs]
        a = cb_ref[:, cs] + (prev2 * cw[0:1] + prev1 * cw[1:2] + u * cw[2:3])
        h_ref[:, cs] = (jax.nn.silu(a) * gate).astype(BF16).reshape(tm, width)
        tail = u[:, rows - (CONV_W - 1):, :]
        cnew_ref[:, :, cs] = tail
        if tiles_per_seq > 1:
            carry_ref[:, :, cs] = tail


def _ffn_up(x, w_up, w_gate, conv_w, conv_b, conv_buf, seq, tm, tf):
    m = x.shape[0]
    n_seq = m // seq
    if tm >= seq:
        spt, tiles_per_seq = tm // seq, 1
        seq_of = lambda i: i
    else:
        spt, tiles_per_seq = 1, seq // tm
        seq_of = lambda i: i // tiles_per_seq
    n_f = D_FF // tf
    hid, tails = pl.pallas_call(
        functools.partial(_ffn_up_kernel, spt=spt, tiles_per_seq=tiles_per_seq),
        grid=(n_f, m // tm),
        in_specs=[
            pl.BlockSpec((tm, D_MODEL), lambda f, i: (i, 0)),
            pl.BlockSpec((D_MODEL, tf), lambda f, i: (0, f)),
            pl.BlockSpec((D_MODEL, tf), lambda f, i: (0, f)),
            pl.BlockSpec((CONV_W, tf), lambda f, i: (0, f)),
            pl.BlockSpec((1, tf), lambda f, i: (0, f)),
            pl.BlockSpec((spt, CONV_W - 1, tf), lambda f, i: (seq_of(i), 0, f)),
        ],
        out_specs=[
            pl.BlockSpec((tm, tf), lambda f, i: (i, f)),
            pl.BlockSpec((spt, CONV_W - 1, tf), lambda f, i: (i, 0, f)),
        ],
        out_shape=[
            jax.ShapeDtypeStruct((m, D_FF), BF16),
            jax.ShapeDtypeStruct((n_seq * tiles_per_seq, CONV_W - 1, D_FF), F32),
        ],
        scratch_shapes=[pltpu.VMEM((spt, CONV_W - 1, tf), F32)],
        compiler_params=_cparams(("arbitrary", "arbitrary")),
        name="ffn_up",
    )(x, w_up, w_gate, conv_w, conv_b, conv_buf)
    return hid, tails[tiles_per_seq - 1::tiles_per_seq]


def _ffn_down_kernel(h_ref, x_ref, w_ref, g_ref, b_ref, y_ref):
    for r0 in range(0, x_ref.shape[0], ROW_SUB):
        rows = slice(r0, r0 + ROW_SUB)
        ff = jnp.dot(h_ref[rows, :], w_ref[...], preferred_element_type=F32)
        y_ref[rows, :] = _layer_norm(ALPHA * x_ref[rows, :] + ff, g_ref[...], b_ref[...])


def _ffn_down(hid, x, w_down, ln2_g, ln2_b, tm):
    m = x.shape[0]
    row = lambda i: (i, 0)
    const = lambda i: (0, 0)
    return pl.pallas_call(
        _ffn_down_kernel,
        grid=(m // tm,),
        in_specs=[
            pl.BlockSpec((tm, D_FF), row),
            pl.BlockSpec((tm, D_MODEL), row),
            pl.BlockSpec((D_FF, D_MODEL), const, pipeline_mode=pl.Buffered(1)),
            pl.BlockSpec((1, D_MODEL), const),
            pl.BlockSpec((1, D_MODEL), const),
        ],
        out_specs=pl.BlockSpec((tm, D_MODEL), row),
        out_shape=jax.ShapeDtypeStruct((m, D_MODEL), F32),
        compiler_params=_cparams(("arbitrary",)),
        name="ffn_down",
    )(hid, x, w_down, ln2_g, ln2_b)


def _rope_tables(past, seq, rows):
    inv = ROPE_THETA ** (-np.arange(0, QK_ROPE, 2, dtype=np.float64) / QK_ROPE)
    ang = (past + np.arange(seq, dtype=np.float64))[:, None] * inv[None, :]
    cos, sin = np.cos(ang), np.sin(ang)
    zero = np.zeros((seq, ROPE_PAD - QK_ROPE))
    cos_t = np.concatenate([cos, cos, zero], axis=1).astype(np.float32)
    sin_t = np.concatenate([-sin, sin, zero], axis=1).astype(np.float32)
    reps = rows // seq
    return jnp.asarray(np.tile(cos_t, (reps, 1))), jnp.asarray(np.tile(sin_t, (reps, 1)))


def _swap_halves(w):
    half = QK_ROPE // 2
    return jnp.concatenate([w[..., half:], w[..., :half]], axis=-1)


def _prep_weights(w_in, w_q_b, w_kv_b, w_out, w_ffn_up, w_ffn_gate, w_ffn_down):
    w_in = w_in[0]
    c0 = 4 * D_HGRN
    w_h = w_in.astype(BF16)
    kr_cols = w_in[:, c0 + Q_LORA + KV_LORA:]
    zpad = jnp.zeros((D_MODEL, ROPE_PAD - QK_ROPE), F32)
    w_m = jnp.concatenate([w_in[:, c0:c0 + Q_LORA + KV_LORA], kr_cols, zpad,
                           _swap_halves(kr_cols), zpad], axis=1).astype(BF16)

    wq = w_q_b[0].reshape(Q_LORA, MLA_HEADS, QK_NOPE + QK_ROPE)
    zq = jnp.zeros((Q_LORA, MLA_HEADS, QK_PAD - QK_NOPE - QK_ROPE), F32)
    w_q = jnp.concatenate([wq, zq], axis=-1).reshape(Q_LORA, MLA_HEADS * QK_PAD).astype(BF16)
    zr = jnp.zeros((Q_LORA, MLA_HEADS, ROPE_PAD - QK_ROPE), F32)
    w_q_sw = jnp.concatenate([_swap_halves(wq[..., QK_NOPE:]), zr], axis=-1)
    w_q_sw = w_q_sw.reshape(Q_LORA, MLA_HEADS * ROPE_PAD).astype(BF16)

    wkv = w_kv_b[0].reshape(KV_LORA, MLA_HEADS, QK_NOPE + V_HEAD)
    w_k = wkv[..., :QK_NOPE].reshape(KV_LORA, MLA_HEADS * QK_NOPE).astype(BF16)
    w_v = wkv[..., QK_NOPE:].reshape(KV_LORA, D_MLA).astype(BF16)
    w_uk_t = wkv[..., :QK_NOPE].transpose(1, 2, 0).astype(BF16)
    w_uv = wkv[..., QK_NOPE:].transpose(1, 0, 2).astype(BF16)
    return dict(w_h=w_h, w_m=w_m, w_q=w_q, w_q_sw=w_q_sw, w_k=w_k, w_v=w_v, w_uk_t=w_uk_t, w_uv=w_uv,
                w_out=w_out[0].astype(BF16), w_up=w_ffn_up[0].astype(BF16),
                w_gate=w_ffn_gate[0].astype(BF16), w_down=w_ffn_down[0].astype(BF16))


def _row(a):
    return a.reshape(1, -1)


def _tile_plan(batch, seq, n_keys_pad):
    m = batch * seq
    tm = min(ROW_TILE, m)
    tm_big = min(BIG_ROW_TILE, m)
    tm_ffn = min(FFN_ROW_TILE, m)
    assert m % tm_big == 0 and m % tm_ffn == 0 and (tm_ffn % seq == 0 or seq % tm_ffn == 0)
    return dict(
        tm=tm,
        tm_in=tm_big,
        tm_ffn=tm_ffn,
        rows_hgrn=min(HGRN_STEP_ROWS, seq),
        tq=min(ATTN_TILE, seq),
        tk=ATTN_TILE if n_keys_pad % ATTN_TILE == 0 else n_keys_pad,
        tm_kv=ROW_TILE if n_keys_pad % ROW_TILE == 0 else n_keys_pad,
    )


def _encoder(x, s0, lat_past, kr_past, conv_buf, p, w, *, blk):
    batch, seq, _ = x.shape
    past = 0 if lat_past is None else lat_past.shape[1]
    m = batch * seq
    n_keys = past + seq
    key_pad = -n_keys % KEY_ALIGN
    n_keys_pad = n_keys + key_pad
    t = _tile_plan(batch, seq, n_keys_pad)
    x2 = x.reshape(m, D_MODEL)
    cos_k, sin_k = _rope_tables(past, seq, max(seq, t["tm_in"]))

    hg, cqn, lat_new, kr_new = _in_proj(x2, p["ln_in_g"], p["ln_in_b"], w["w_h"], w["w_m"],
                                        p["q_a_g"], p["kv_a_g"], cos_k, sin_k, t["tm_in"])
    o_h, s_new = _hgrn(hg, p["lb"], p["hgrn_norm_g"], s0, batch, seq, blk, t["rows_hgrn"])

    q = _q_proj(cqn, w["w_q"], w["w_q_sw"], cos_k, sin_k, t["tm"])
    lat3 = lat_new.reshape(batch, seq, KV_LORA)
    kr3 = kr_new.reshape(batch, seq, ROPE_PAD)
    if past or key_pad:
        parts_l, parts_r = [], []
        if past:
            parts_l.append(lat_past)
            parts_r.append(jnp.pad(kr_past, ((0, 0), (0, 0), (0, ROPE_PAD - QK_ROPE))))
        parts_l.append(lat3)
        parts_r.append(kr3)
        if key_pad:
            parts_l.append(jnp.zeros((batch, key_pad, KV_LORA), F32))
            parts_r.append(jnp.zeros((batch, key_pad, ROPE_PAD), F32))
        lat_all = jnp.concatenate(parts_l, axis=1)
        kr_all = jnp.concatenate(parts_r, axis=1)
    else:
        lat_all, kr_all = lat3, kr3
    if seq * MLA_HEADS <= ABSORB_MAX_ROWS:
        kx = jnp.concatenate([lat_all, kr_all], axis=-1).astype(BF16)
        o_a = _absorbed_attention(q, kx, w["w_uk_t"], w["w_uv"], batch, seq, n_keys, past)
    else:
        k_cat, v = _kv_proj(lat_all.reshape(batch * n_keys_pad, KV_LORA),
                            kr_all.reshape(batch * n_keys_pad, ROPE_PAD), w["w_k"], w["w_v"], t["tm_kv"])
        o_a = _attention(q.reshape(batch, seq, MLA_HEADS * QK_PAD),
                         k_cat.reshape(batch, n_keys_pad, MLA_HEADS * QK_PAD),
                         v.reshape(batch, n_keys_pad, D_MLA),
                         batch, seq, n_keys_pad, n_keys, past, t["tq"], t["tk"]).reshape(m, D_MLA)

    h1, h1_bf16 = _out_proj(o_h, o_a, x2, p["ln_in_g"], p["ln_in_b"], w["w_out"],
                            p["ln1_g"], p["ln1_b"], t["tm"])
    hid, conv_new = _ffn_up(h1_bf16, w["w_up"], w["w_gate"], p["conv_w"], p["conv_b"], conv_buf, seq, t["tm_ffn"],
                            FFN_COL_TILE)
    y = _ffn_down(hid, h1, w["w_down"], p["ln2_g"], p["ln2_b"], t["tm"])
    return (y.reshape(batch, seq, D_MODEL), s_new[None], lat3[None],
            kr3[:, :, :QK_ROPE][None], conv_new[None])


def kernel(x_prompt, x_sample, cache_kv_latent, cache_k_rope, state_hgrn, cache_ffn_conv, lb_param, ln_in_g, ln_in_b, w_in, hgrn_norm_g, q_a_g, w_q_b, kv_a_g, w_kv_b, w_out, ln1_g, ln1_b, w_ffn_up, w_ffn_gate, conv_w, conv_b, w_ffn_down, ln2_g, ln2_b):
    w = _prep_weights(w_in, w_q_b, w_kv_b, w_out, w_ffn_up, w_ffn_gate, w_ffn_down)
    lbs = jnp.cumsum(jax.nn.softmax(lb_param.astype(F32), axis=0), axis=0)
    p = dict(ln_in_g=_row(ln_in_g), ln_in_b=_row(ln_in_b), lb=_row(lbs[0]),
             hgrn_norm_g=_row(hgrn_norm_g[0]), q_a_g=_row(q_a_g[0]), kv_a_g=_row(kv_a_g[0]),
             ln1_g=_row(ln1_g[0]), ln1_b=_row(ln1_b[0]), conv_w=conv_w[0], conv_b=_row(conv_b[0]),
             ln2_g=_row(ln2_g[0]), ln2_b=_row(ln2_b[0]))

    n_prompt = x_prompt.shape[0]
    conv0 = jnp.zeros((n_prompt, CONV_W - 1, D_FF), F32)
    y_p, s_p, lat_p, kr_p, conv_p = _encoder(x_prompt, None, None, None, conv0, p, w, blk=CHUNK)
    y_s, s_s, lat_s, kr_s, conv_s = _encoder(
        x_sample, state_hgrn[0], cache_kv_latent[0], cache_k_rope[0], cache_ffn_conv[0], p, w,
        blk=x_sample.shape[1])
    return (y_p, y_s, lat_p, kr_p, s_p, conv_p, lat_s, kr_s, s_s, conv_s)
```

```python
import functools

import numpy as np
import jax
import jax.numpy as jnp
from jax import lax
from jax.experimental import pallas as pl
from jax.experimental.pallas import tpu as pltpu

F32 = jnp.float32
BF16 = jnp.bfloat16

D_MODEL = 2048
CHUNK = 64
HGRN_HEADS = 8
HGRN_DK = 128
HGRN_DV = 128
D_HGRN = HGRN_HEADS * HGRN_DK
MLA_HEADS = 8
Q_LORA = 512
KV_LORA = 256
QK_NOPE = 128
QK_ROPE = 64
V_HEAD = 128
D_MLA = MLA_HEADS * V_HEAD
D_FF = 5632
CONV_W = 3
ROPE_THETA = 10000.0
LN_EPS = 1e-5
RMS_EPS = 1e-6
DEPTH = 1
ALPHA = (2.0 * DEPTH) ** 0.25

QK_PAD = 256
ROPE_PAD = 128
ROW_TILE = 512
BIG_ROW_TILE = 512
FFN_ROW_TILE = 1024
HGRN_STEP_ROWS = 256
ABSORB_MAX_ROWS = 512
ATTN_TILE = 512
FFN_COL_TILE = 1408
KEY_ALIGN = 128
ROW_SUB = 256
ONES_ROWS = 16
ATTN_UNROLL = 4
FFN_SUB = 256
NEG_BIG = -1e30
LOG2_E = 1.4426950408889634
V7X_VMEM_LIMIT = 60 * 1024 * 1024

NT_DIMS = (((1,), (1,)), ((), ()))
TN_DIMS = (((0,), (0,)), ((), ()))


def _cparams(sem):
    return pltpu.CompilerParams(dimension_semantics=sem, vmem_limit_bytes=V7X_VMEM_LIMIT)


def _layer_norm(x, g, b):
    mu = jnp.mean(x, axis=-1, keepdims=True)
    xc = x - mu
    var = jnp.mean(xc * xc, axis=-1, keepdims=True)
    return xc * lax.rsqrt(var + LN_EPS) * g + b


def _rms_norm(x, g):
    return x * lax.rsqrt(jnp.mean(x * x, axis=-1, keepdims=True) + RMS_EPS) * g


HG_TILE = 2 * D_HGRN
N_HG_TILES = 4 * D_HGRN // HG_TILE
W_MLA_COLS = Q_LORA + KV_LORA + 2 * ROPE_PAD


def _in_proj_kernel(x_ref, g_ref, b_ref, wh_ref, wm_ref, qg_ref, kvg_ref, cos_ref, sin_ref,
                    hg_ref, cq_ref, lat_ref, kr_ref, h_ref, hb_ref):
    j = pl.program_id(1)

    @pl.when(j == 0)
    def _():
        for r0 in range(0, x_ref.shape[0], ROW_SUB):
            rows = slice(r0, r0 + ROW_SUB)
            h = _layer_norm(x_ref[rows, :], g_ref[...], b_ref[...])
            h_ref[rows, :] = h
            hb = h.astype(BF16)
            hb_ref[rows, :] = hb
            hg_ref[rows, :] = jnp.dot(hb, wh_ref[:, :HG_TILE], preferred_element_type=F32)

    @pl.when(jnp.logical_and(j > 0, j < N_HG_TILES))
    def _():
        cols = pl.ds(pl.multiple_of(j * HG_TILE, HG_TILE), HG_TILE)
        hg_ref[...] = jnp.dot(hb_ref[...], wh_ref[:, cols], preferred_element_type=F32)

    @pl.when(j == N_HG_TILES)
    def _():
        p = jnp.dot(hb_ref[...], wm_ref[...], preferred_element_type=F32)
        cq = p[:, :Q_LORA]
        ckv = p[:, Q_LORA:Q_LORA + KV_LORA]
        kr = p[:, Q_LORA + KV_LORA:Q_LORA + KV_LORA + ROPE_PAD]
        kr_sw = p[:, Q_LORA + KV_LORA + ROPE_PAD:]
        cq_ref[...] = _rms_norm(cq, qg_ref[...]).astype(BF16)
        lat_ref[...] = _rms_norm(ckv, kvg_ref[...])
        kr_ref[...] = kr * cos_ref[...] + kr_sw * sin_ref[...]


def _in_proj(x, ln_g, ln_b, w_h, w_m, q_a_g, kv_a_g, cos_k, sin_k, tm):
    m = x.shape[0]
    n_pos_tiles = cos_k.shape[0] // tm
    grid = (m // tm, N_HG_TILES + 1)
    row = lambda i, j: (i, 0)
    const = lambda i, j: (0, 0)
    pos = lambda i, j: (i % n_pos_tiles, 0)
    hg_col = lambda i, j: (i, jnp.minimum(j, N_HG_TILES - 1))
    return pl.pallas_call(
        _in_proj_kernel,
        grid=grid,
        in_specs=[
            pl.BlockSpec((tm, D_MODEL), row),
            pl.BlockSpec((1, D_MODEL), const),
            pl.BlockSpec((1, D_MODEL), const),
            pl.BlockSpec(w_h.shape, const, pipeline_mode=pl.Buffered(1)),
            pl.BlockSpec((D_MODEL, W_MLA_COLS), const, pipeline_mode=pl.Buffered(1)),
            pl.BlockSpec((1, Q_LORA), const),
            pl.BlockSpec((1, KV_LORA), const),
            pl.BlockSpec((tm, ROPE_PAD), pos),
            pl.BlockSpec((tm, ROPE_PAD), pos),
        ],
        out_specs=[
            pl.BlockSpec((tm, HG_TILE), hg_col),
            pl.BlockSpec((tm, Q_LORA), row),
            pl.BlockSpec((tm, KV_LORA), row),
            pl.BlockSpec((tm, ROPE_PAD), row),
            pl.BlockSpec((tm, D_MODEL), row),
        ],
        out_shape=[
            jax.ShapeDtypeStruct((m, N_HG_TILES * HG_TILE), F32),
            jax.ShapeDtypeStruct((m, Q_LORA), BF16),
            jax.ShapeDtypeStruct((m, KV_LORA), F32),
            jax.ShapeDtypeStruct((m, ROPE_PAD), F32),
            jax.ShapeDtypeStruct((m, D_MODEL), F32),
        ],
        scratch_shapes=[pltpu.VMEM((tm, D_MODEL), BF16)],
        compiler_params=_cparams(("arbitrary", "arbitrary")),
        name="in_proj",
    )(x, ln_g, ln_b, w_h, w_m, q_a_g, kv_a_g, cos_k, sin_k)


def _cumsum_rows(x):
    c = x.shape[0]
    row = lax.broadcasted_iota(jnp.int32, (c, 1), 0)
    shift = 1
    while shift < c:
        x = x + jnp.where(row >= shift, pltpu.roll(x, shift, 0), 0.0)
        shift *= 2
    return x


def _split_level(c):
    t = lax.broadcasted_iota(jnp.int32, (c, c), 0)
    s = lax.broadcasted_iota(jnp.int32, (c, c), 1)
    x = jnp.bitwise_xor(t, s)
    lvl = jnp.full((c, c), -1, jnp.int32)
    for bit in range(c.bit_length() - 1):
        lvl = jnp.where(x >= (1 << bit), bit, lvl)
    return jnp.where(t < s, -2, lvl)


def _hgrn_head_block(q, fpre, v, lb, st, lvl):
    c = q.shape[0]
    f = lb + (1.0 - lb) * jax.nn.sigmoid(fpre)
    g = jnp.log(f)
    k = 1.0 - f
    b = _cumsum_rows(g)
    row = lax.broadcasted_iota(jnp.int32, (c, 1), 0)

    a = jnp.where(lvl == -1, jnp.sum(q * k, axis=-1, keepdims=True), 0.0)
    for level in range(c.bit_length() - 1):
        half = 1 << level
        pos = row % (2 * half)
        is_query = pos >= half
        if level == 0:
            decay = jnp.where(is_query, f, 1.0)
        else:
            if 2 * half < 8:
                ref = b
                for p in range(2 * half):
                    d = half - 1 - p
                    if d != 0:
                        ref = jnp.where(pos == p, pltpu.roll(b, (-d) % c, 0), ref)
            else:
                ref = jnp.concatenate(
                    [jnp.broadcast_to(b[j + half - 1:j + half], (2 * half, b.shape[1]))
                     for j in range(0, c, 2 * half)], axis=0)
            decay = jnp.exp(-jnp.abs(b - ref))
        z = (jnp.where(is_query, q, k) * decay).astype(BF16)
        gram = lax.dot_general(z, z, NT_DIMS, preferred_element_type=F32)
        a = jnp.where(lvl == level, gram, a)

    vb = v.astype(BF16)
    o = jnp.dot(a.astype(BF16), vb, preferred_element_type=F32)
    o = o + lax.dot_general((q * jnp.exp(b)).astype(BF16), st.astype(BF16), NT_DIMS,
                            preferred_element_type=F32)
    b_last = b[c - 1:c]
    k_dec = (k * jnp.exp(b_last - b)).astype(BF16)
    st_new = jnp.exp(b_last) * st + lax.dot_general(vb, k_dec, TN_DIMS, preferred_element_type=F32)
    return o, st_new


def _hgrn_kernel(*refs, c, has_s0):
    if has_s0:
        q_ref, f_ref, i_ref, g_ref, lb_ref, ng_ref, s0_ref, o_ref, s_ref, st_ref = refs
    else:
        q_ref, f_ref, i_ref, g_ref, lb_ref, ng_ref, o_ref, s_ref, st_ref = refs
        s0_ref = None
    t = pl.program_id(1)

    @pl.when(t == 0)
    def _():
        for h in range(HGRN_HEADS):
            if has_s0:
                st_ref[h] = s0_ref[0, h].T
            else:
                st_ref[h] = jnp.zeros((HGRN_DV, HGRN_DK), F32)

    lvl = _split_level(c)

    def block(i, _):
        rows = pl.ds(pl.multiple_of(i * c, c), c)
        for h in range(HGRN_HEADS):
            cols = slice(h * HGRN_DK, (h + 1) * HGRN_DK)
            o, st_new = _hgrn_head_block(q_ref[rows, cols], f_ref[rows, cols], i_ref[rows, cols],
                                         lb_ref[:, cols], st_ref[h], lvl)
            st_ref[h] = st_new
            o = _rms_norm(o, ng_ref[:, cols]) * jax.nn.silu(g_ref[rows, cols])
            o_ref[rows, cols] = o.astype(BF16)
        return 0

    lax.fori_loop(0, q_ref.shape[0] // c, block, 0)

    @pl.when(t == pl.num_programs(1) - 1)
    def _():
        for h in range(HGRN_HEADS):
            s_ref[0, h] = st_ref[h].T


def _hgrn(hg, lb, norm_g, s0, batch, seq, c, rows):
    nt = seq // rows
    has_s0 = s0 is not None
    col = lambda n: (lambda b, t: (b * nt + t, n))
    const = lambda b, t: (0, 0)
    state = lambda b, t: (b, 0, 0, 0)
    in_specs = [pl.BlockSpec((rows, D_HGRN), col(n)) for n in range(4)]
    in_specs += [pl.BlockSpec((1, D_HGRN), const), pl.BlockSpec((1, D_HGRN), const)]
    args = [hg, hg, hg, hg, lb, norm_g]
    if has_s0:
        in_specs.append(pl.BlockSpec((1, HGRN_HEADS, HGRN_DK, HGRN_DV), state))
        args.append(s0)
    return pl.pallas_call(
        functools.partial(_hgrn_kernel, c=c, has_s0=has_s0),
        grid=(batch, nt),
        in_specs=in_specs,
        out_specs=[
            pl.BlockSpec((rows, D_HGRN), lambda b, t: (b * nt + t, 0)),
            pl.BlockSpec((1, HGRN_HEADS, HGRN_DK, HGRN_DV), state),
        ],
        out_shape=[
            jax.ShapeDtypeStruct((batch * seq, D_HGRN), BF16),
            jax.ShapeDtypeStruct((batch, HGRN_HEADS, HGRN_DK, HGRN_DV), F32),
        ],
        scratch_shapes=[pltpu.VMEM((HGRN_HEADS, HGRN_DV, HGRN_DK), F32)],
        compiler_params=_cparams(("arbitrary", "arbitrary")),
        name="hgrn",
    )(*args)


def _q_proj_kernel(cq_ref, wq_ref, wsw_ref, cos_ref, sin_ref, q_ref, *, scale):
    cq = cq_ref[...]
    q_lin = jnp.dot(cq, wq_ref[...], preferred_element_type=F32)
    q_sw = jnp.dot(cq, wsw_ref[...], preferred_element_type=F32)
    cos = cos_ref[...]
    sin = sin_ref[...]
    for h in range(MLA_HEADS):
        base = h * QK_PAD
        q_ref[:, base:base + QK_NOPE] = (q_lin[:, base:base + QK_NOPE] * scale).astype(BF16)
        rot = (q_lin[:, base + QK_NOPE:base + QK_PAD] * cos
               + q_sw[:, h * ROPE_PAD:(h + 1) * ROPE_PAD] * sin)
        q_ref[:, base + QK_NOPE:base + QK_PAD] = (rot * scale).astype(BF16)


def _q_proj(cq, w_q, w_q_sw, cos_k, sin_k, tm):
    m = cq.shape[0]
    n_pos_tiles = cos_k.shape[0] // tm
    scale = LOG2_E * (QK_NOPE + QK_ROPE) ** -0.5
    row = lambda i: (i, 0)
    const = lambda i: (0, 0)
    pos = lambda i: (i % n_pos_tiles, 0)
    return pl.pallas_call(
        functools.partial(_q_proj_kernel, scale=scale),
        grid=(m // tm,),
        in_specs=[
            pl.BlockSpec((tm, Q_LORA), row),
            pl.BlockSpec((Q_LORA, MLA_HEADS * QK_PAD), const),
            pl.BlockSpec((Q_LORA, MLA_HEADS * ROPE_PAD), const),
            pl.BlockSpec((tm, ROPE_PAD), pos),
            pl.BlockSpec((tm, ROPE_PAD), pos),
        ],
        out_specs=pl.BlockSpec((tm, MLA_HEADS * QK_PAD), row),
        out_shape=jax.ShapeDtypeStruct((m, MLA_HEADS * QK_PAD), BF16),
        compiler_params=_cparams(("arbitrary",)),
        name="q_proj",
    )(cq, w_q, w_q_sw, cos_k, sin_k)


def _kv_proj_kernel(lat_ref, kr_ref, wk_ref, wvt_ref, k_ref, vt_ref):
    lat = lat_ref[...].astype(BF16)
    kn = jnp.dot(lat, wk_ref[...], preferred_element_type=F32)
    vt_ref[0] = lax.dot_general(wvt_ref[...], lat, NT_DIMS, preferred_element_type=F32).astype(BF16)
    kr = kr_ref[...].astype(BF16)
    for h in range(MLA_HEADS):
        base = h * QK_PAD
        k_ref[:, base:base + QK_NOPE] = kn[:, h * QK_NOPE:(h + 1) * QK_NOPE].astype(BF16)
        k_ref[:, base + QK_NOPE:base + QK_PAD] = kr


def _kv_proj(lat, kr_pad, w_k, w_vt, batch, tm):
    m = lat.shape[0]
    tiles_per_batch = m // batch // tm
    row = lambda i: (i, 0)
    const = lambda i: (0, 0)
    return pl.pallas_call(
        _kv_proj_kernel,
        grid=(m // tm,),
        in_specs=[
            pl.BlockSpec((tm, KV_LORA), row),
            pl.BlockSpec((tm, ROPE_PAD), row),
            pl.BlockSpec((KV_LORA, MLA_HEADS * QK_NOPE), const),
            pl.BlockSpec((D_MLA, KV_LORA), const),
        ],
        out_specs=[
            pl.BlockSpec((tm, MLA_HEADS * QK_PAD), row),
            pl.BlockSpec((1, D_MLA, tm), lambda i: (i // tiles_per_batch, 0, i % tiles_per_batch)),
        ],
        out_shape=[
            jax.ShapeDtypeStruct((m, MLA_HEADS * QK_PAD), BF16),
            jax.ShapeDtypeStruct((batch, D_MLA, m // batch), BF16),
        ],
        compiler_params=_cparams(("arbitrary",)),
        name="kv_proj",
    )(lat, kr_pad, w_k, w_vt)


def _attn_kernel(tile_q_ref, tile_k_ref, q_ref, k_ref, vt_ref, o_ref,
                 vx_ref, rel_ref, sa_ref, sb_ref, m_ref, acc_ref, *, tq, tk, past, n_keys, n_plain, n_tiles):
    @pl.when(jnp.logical_and(pl.program_id(0) == 0, pl.program_id(1) == 0))
    def _():
        rel_ref[...] = (lax.broadcasted_iota(jnp.int32, (tk, tq), 0) // CHUNK
                        - lax.broadcasted_iota(jnp.int32, (tk, tq), 1) // CHUNK)

    vx_ref[:V_HEAD, :] = vt_ref[0]
    vx_ref[V_HEAD:, :] = jnp.ones((ONES_ROWS, vx_ref.shape[1]), BF16)
    m_ref[...] = jnp.full(m_ref.shape, NEG_BIG, F32)
    acc_ref[...] = jnp.zeros(acc_ref.shape, F32)
    k_row = lax.broadcasted_iota(jnp.int32, (tk, tq), 0)

    def origin(t):
        return pl.multiple_of(tile_q_ref[t] * tq, tq), pl.multiple_of(tile_k_ref[t] * tk, tk)

    def scores(t, s_ref):
        qs, ks = origin(t)
        s_ref[...] = lax.dot_general(k_ref[0, pl.ds(ks, tk), :], q_ref[0, pl.ds(qs, tq), :], NT_DIMS,
                                     preferred_element_type=F32)

    def softmax_pv(t, s_ref, masked):
        qs, ks = origin(t)
        s = s_ref[...]
        if masked:
            pad = jnp.where(k_row < n_keys - ks, 0, 1 << 20)
            visible = rel_ref[...] + pad <= (past + qs - ks) // CHUNK
            s = jnp.where(visible, s, NEG_BIG)
        cols = pl.ds(qs, tq)
        m = m_ref[:, cols]
        m_new = jnp.maximum(m, jnp.max(s, axis=0, keepdims=True))
        p = jnp.exp2(s - m_new).astype(BF16)
        pv = jnp.dot(vx_ref[:, pl.ds(ks, tk)], p, preferred_element_type=F32)
        acc_ref[:, cols] = jnp.exp2(m - m_new) * acc_ref[:, cols] + pv
        m_ref[:, cols] = m_new

    bufs = (sa_ref, sb_ref)

    def step(t, parity, masked):
        scores(t + 1, bufs[1 - parity])
        softmax_pv(t, bufs[parity], masked)

    def run(t0, t1, masked):
        t = t0
        if t < t1 and t % 2 == 1:
            step(t, 1, masked)
            t += 1
        trips = (t1 - t) // ATTN_UNROLL
        if trips > 0:
            def body(i, _, base=t):
                tt = base + ATTN_UNROLL * i
                for u in range(ATTN_UNROLL):
                    step(tt + u, u % 2, masked)
                return 0
            lax.fori_loop(0, trips, body, 0)
            t += ATTN_UNROLL * trips
        while t < t1:
            step(t, t % 2, masked)
            t += 1

    scores(0, sa_ref)
    run(0, min(n_plain, n_tiles - 1), False)
    run(min(n_plain, n_tiles - 1), n_tiles - 1, True)
    last = n_tiles - 1
    softmax_pv(last, bufs[last % 2], last >= n_plain)
    acc = acc_ref[...]
    o_ref[0] = (acc[:V_HEAD] / acc[V_HEAD:V_HEAD + 1]).T.astype(BF16)


def _attention_tiles(seq, n_keys_pad, n_keys, past, tq, tk):
    plain, masked = [], []
    for j in range(n_keys_pad // tk):
        for i in range(seq // tq):
            q_lo, k_lo = past + i * tq, j * tk
            q_hi, k_hi = q_lo + tq - 1, k_lo + tk - 1
            if k_lo // CHUNK > q_hi // CHUNK or k_lo >= n_keys:
                continue
            if k_hi // CHUNK <= q_lo // CHUNK and k_hi < n_keys:
                plain.append((i, j))
            else:
                masked.append((i, j))
    return plain, masked


def _attention(q, k, vt, batch, seq, n_keys_pad, n_keys, past, tq, tk):
    assert past % CHUNK == 0 and tk % CHUNK == 0 and (tq % CHUNK == 0 or tq == seq)
    plain, masked = _attention_tiles(seq, n_keys_pad, n_keys, past, tq, tk)
    tiles = plain + masked
    tile_q = jnp.asarray([t[0] for t in tiles], jnp.int32)
    tile_k = jnp.asarray([t[1] for t in tiles], jnp.int32)
    smem = pl.BlockSpec(memory_space=pltpu.SMEM)
    return pl.pallas_call(
        functools.partial(_attn_kernel, tq=tq, tk=tk, past=past, n_keys=n_keys,
                          n_plain=len(plain), n_tiles=len(tiles)),
        grid=(batch, MLA_HEADS),
        in_specs=[
            smem, smem,
            pl.BlockSpec((1, seq, QK_PAD), lambda b, h: (b, 0, h)),
            pl.BlockSpec((1, n_keys_pad, QK_PAD), lambda b, h: (b, 0, h)),
            pl.BlockSpec((1, V_HEAD, n_keys_pad), lambda b, h: (b, h, 0)),
        ],
        out_specs=pl.BlockSpec((1, seq, V_HEAD), lambda b, h: (b, 0, h)),
        out_shape=jax.ShapeDtypeStruct((batch, seq, D_MLA), BF16),
        scratch_shapes=[
            pltpu.VMEM((V_HEAD + ONES_ROWS, n_keys_pad), BF16),
            pltpu.VMEM((tk, tq), jnp.int32),
            pltpu.VMEM((tk, tq), F32),
            pltpu.VMEM((tk, tq), F32),
            pltpu.VMEM((1, seq), F32),
            pltpu.VMEM((V_HEAD + ONES_ROWS, seq), F32),
        ],
        compiler_params=_cparams(("arbitrary", "arbitrary")),
        name="attn",
    )(tile_q, tile_k, q, k, vt)


LAT_KEY = KV_LORA + ROPE_PAD


def _absorbed_attn_kernel(q_ref, kx_ref, wuk_ref, wuv_ref, o_ref, qx_ref, vx_ref, *, seq, past, n_keys):
    for h in range(MLA_HEADS):
        rows = slice(h * seq, (h + 1) * seq)
        q_lat = jnp.dot(q_ref[:, h * QK_PAD:h * QK_PAD + QK_NOPE], wuk_ref[h], preferred_element_type=F32)
        qx_ref[rows, :KV_LORA] = q_lat.astype(BF16)
        qx_ref[rows, KV_LORA:] = q_ref[:, h * QK_PAD + QK_NOPE:(h + 1) * QK_PAD]
    kx = kx_ref[0]
    s = lax.dot_general(qx_ref[...], kx, NT_DIMS, preferred_element_type=F32)
    q_chunk = (past + lax.broadcasted_iota(jnp.int32, (s.shape[0], 1), 0) % seq) // CHUNK
    k_idx = lax.broadcasted_iota(jnp.int32, (1, s.shape[1]), 1)
    visible = jnp.logical_and(k_idx // CHUNK <= q_chunk, k_idx < n_keys)
    s = jnp.where(visible, s, NEG_BIG)
    p = jnp.exp2(s - jnp.max(s, axis=-1, keepdims=True)).astype(BF16)
    vx_ref[:, :KV_LORA] = kx[:, :KV_LORA]
    vx_ref[:, KV_LORA:] = jnp.ones((vx_ref.shape[0], ROPE_PAD), BF16)
    o_ext = jnp.dot(p, vx_ref[...], preferred_element_type=F32)
    denom = o_ext[:, KV_LORA:]
    o_lat = (o_ext[:, :KV_LORA] / jnp.concatenate([denom] * (KV_LORA // ROPE_PAD), axis=1)).astype(BF16)
    for h in range(MLA_HEADS):
        o_ref[:, h * V_HEAD:(h + 1) * V_HEAD] = jnp.dot(
            o_lat[h * seq:(h + 1) * seq], wuv_ref[h], preferred_element_type=F32).astype(BF16)


def _absorbed_attention(q, kx, w_uk_t, w_uv, batch, seq, n_keys, past):
    n_keys_pad = kx.shape[1]
    const3 = lambda b: (0, 0, 0)
    return pl.pallas_call(
        functools.partial(_absorbed_attn_kernel, seq=seq, past=past, n_keys=n_keys),
        grid=(batch,),
        in_specs=[
            pl.BlockSpec((seq, MLA_HEADS * QK_PAD), lambda b: (b, 0)),
            pl.BlockSpec((1, n_keys_pad, LAT_KEY), lambda b: (b, 0, 0)),
            pl.BlockSpec((MLA_HEADS, QK_NOPE, KV_LORA), const3),
            pl.BlockSpec((MLA_HEADS, KV_LORA, V_HEAD), const3),
        ],
        out_specs=pl.BlockSpec((seq, D_MLA), lambda b: (b, 0)),
        out_shape=jax.ShapeDtypeStruct((batch * seq, D_MLA), BF16),
        scratch_shapes=[
            pltpu.VMEM((MLA_HEADS * seq, LAT_KEY), BF16),
            pltpu.VMEM((n_keys_pad, LAT_KEY), BF16),
        ],
        compiler_params=_cparams(("arbitrary",)),
        name="absorbed_attn",
    )(q, kx, w_uk_t, w_uv)


def _out_proj_kernel(oh_ref, oa_ref, h_ref, w_ref, g1_ref, b1_ref, y_ref, yb_ref):
    for r0 in range(0, h_ref.shape[0], ROW_SUB):
        rows = slice(r0, r0 + ROW_SUB)
        mix = jnp.dot(oh_ref[rows, :], w_ref[:D_HGRN, :], preferred_element_type=F32)
        mix = mix + jnp.dot(oa_ref[rows, :], w_ref[D_HGRN:, :], preferred_element_type=F32)
        y = _layer_norm(ALPHA * h_ref[rows, :] + mix, g1_ref[...], b1_ref[...])
        y_ref[rows, :] = y
        yb_ref[rows, :] = y.astype(BF16)


def _out_proj(oh, oa, h, w_out, ln1_g, ln1_b, tm):
    m = h.shape[0]
    row = lambda i: (i, 0)
    const = lambda i: (0, 0)
    vec = pl.BlockSpec((1, D_MODEL), const)
    return pl.pallas_call(
        _out_proj_kernel,
        grid=(m // tm,),
        in_specs=[
            pl.BlockSpec((tm, D_HGRN), row),
            pl.BlockSpec((tm, D_MLA), row),
            pl.BlockSpec((tm, D_MODEL), row),
            pl.BlockSpec((D_HGRN + D_MLA, D_MODEL), const),
            vec, vec,
        ],
        out_specs=[pl.BlockSpec((tm, D_MODEL), row), pl.BlockSpec((tm, D_MODEL), row)],
        out_shape=[jax.ShapeDtypeStruct((m, D_MODEL), F32), jax.ShapeDtypeStruct((m, D_MODEL), BF16)],
        compiler_params=_cparams(("arbitrary",)),
        name="out_proj",
    )(oh, oa, h, w_out, ln1_g, ln1_b)


def _ffn_up_kernel(x_ref, wu_ref, wg_ref, cw_ref, cb_ref, buf_ref, h_ref, cnew_ref, carry_ref, *,
                   spt, tiles_per_seq):
    i = pl.program_id(1)
    tm = x_ref.shape[0]
    tf = wu_ref.shape[1]
    rows = tm // spt

    if tiles_per_seq > 1:
        @pl.when(i % tiles_per_seq == 0)
        def _():
            carry_ref[...] = buf_ref[...]
        halo_ref = carry_ref
    else:
        halo_ref = buf_ref

    xb = x_ref[...]
    for c0 in range(0, tf, FFN_SUB):
        width = min(FFN_SUB, tf - c0)
        cs = slice(c0, c0 + width)
        r = lax.broadcasted_iota(jnp.int32, (spt, rows, width), 1)
        if 2 * width <= FFN_SUB:
            both = jnp.dot(xb, jnp.concatenate([wu_ref[:, cs], wg_ref[:, cs]], axis=1),
                           preferred_element_type=F32)
            u, gate = both[:, :width], both[:, width:]
        else:
            u = jnp.dot(xb, wu_ref[:, cs], preferred_element_type=F32)
            gate = jnp.dot(xb, wg_ref[:, cs], preferred_element_type=F32)
        u = u.reshape(spt, rows, width)
        gate = gate.reshape(spt, rows, width)
        halo = halo_ref[:, :, cs]
        prev1 = jnp.where(r == 0, halo[:, 1:2, :], pltpu.roll(u, 1, 1))
        prev2 = jnp.where(r == 0, halo[:, 0:1, :],
                          jnp.where(r == 1, halo[:, 1:2, :], pltpu.roll(u, 2, 1)))
        cw = cw_ref[:, cs]
        a = cb_ref[:, cs] + (prev2 * cw[0:1] + prev1 * cw[1:2] + u * cw[2:3])
        h_ref[:, cs] = (jax.nn.silu(a) * gate).astype(BF16).reshape(tm, width)
        tail = u[:, rows - (CONV_W - 1):, :]
        cnew_ref[:, :, cs] = tail
        if tiles_per_seq > 1:
            carry_ref[:, :, cs] = tail


def _ffn_up(x, w_up, w_gate, conv_w, conv_b, conv_buf, seq, tm, tf):
    m = x.shape[0]
    n_seq = m // seq
    if tm >= seq:
        spt, tiles_per_seq = tm // seq, 1
        seq_of = lambda i: i
    else:
        spt, tiles_per_seq = 1, seq // tm
        seq_of = lambda i: i // tiles_per_seq
    n_f = D_FF // tf
    hid, tails = pl.pallas_call(
        functools.partial(_ffn_up_kernel, spt=spt, tiles_per_seq=tiles_per_seq),
        grid=(n_f, m // tm),
        in_specs=[
            pl.BlockSpec((tm, D_MODEL), lambda f, i: (i, 0)),
            pl.BlockSpec((D_MODEL, tf), lambda f, i: (0, f)),
            pl.BlockSpec((D_MODEL, tf), lambda f, i: (0, f)),
            pl.BlockSpec((CONV_W, tf), lambda f, i: (0, f)),
            pl.BlockSpec((1, tf), lambda f, i: (0, f)),
            pl.BlockSpec((spt, CONV_W - 1, tf), lambda f, i: (seq_of(i), 0, f)),
        ],
        out_specs=[
            pl.BlockSpec((tm, tf), lambda f, i: (i, f)),
            pl.BlockSpec((spt, CONV_W - 1, tf), lambda f, i: (i, 0, f)),
        ],
        out_shape=[
            jax.ShapeDtypeStruct((m, D_FF), BF16),
            jax.ShapeDtypeStruct((n_seq * tiles_per_seq, CONV_W - 1, D_FF), F32),
        ],
        scratch_shapes=[pltpu.VMEM((spt, CONV_W - 1, tf), F32)],
        compiler_params=_cparams(("arbitrary", "arbitrary")),
        name="ffn_up",
    )(x, w_up, w_gate, conv_w, conv_b, conv_buf)
    return hid, tails[tiles_per_seq - 1::tiles_per_seq]


def _ffn_down_kernel(h_ref, x_ref, w_ref, g_ref, b_ref, y_ref):
    for r0 in range(0, x_ref.shape[0], ROW_SUB):
        rows = slice(r0, r0 + ROW_SUB)
        ff = jnp.dot(h_ref[rows, :], w_ref[...], preferred_element_type=F32)
        y_ref[rows, :] = _layer_norm(ALPHA * x_ref[rows, :] + ff, g_ref[...], b_ref[...])


def _ffn_down(hid, x, w_down, ln2_g, ln2_b, tm):
    m = x.shape[0]
    row = lambda i: (i, 0)
    const = lambda i: (0, 0)
    return pl.pallas_call(
        _ffn_down_kernel,
        grid=(m // tm,),
        in_specs=[
            pl.BlockSpec((tm, D_FF), row),
            pl.BlockSpec((tm, D_MODEL), row),
            pl.BlockSpec((D_FF, D_MODEL), const, pipeline_mode=pl.Buffered(1)),
            pl.BlockSpec((1, D_MODEL), const),
            pl.BlockSpec((1, D_MODEL), const),
        ],
        out_specs=pl.BlockSpec((tm, D_MODEL), row),
        out_shape=jax.ShapeDtypeStruct((m, D_MODEL), F32),
        compiler_params=_cparams(("arbitrary",)),
        name="ffn_down",
    )(hid, x, w_down, ln2_g, ln2_b)


def _rope_tables(past, seq, rows):
    inv = ROPE_THETA ** (-np.arange(0, QK_ROPE, 2, dtype=np.float64) / QK_ROPE)
    ang = (past + np.arange(seq, dtype=np.float64))[:, None] * inv[None, :]
    cos, sin = np.cos(ang), np.sin(ang)
    zero = np.zeros((seq, ROPE_PAD - QK_ROPE))
    cos_t = np.concatenate([cos, cos, zero], axis=1).astype(np.float32)
    sin_t = np.concatenate([-sin, sin, zero], axis=1).astype(np.float32)
    reps = rows // seq
    return jnp.asarray(np.tile(cos_t, (reps, 1))), jnp.asarray(np.tile(sin_t, (reps, 1)))


def _swap_halves(w):
    half = QK_ROPE // 2
    return jnp.concatenate([w[..., half:], w[..., :half]], axis=-1)


def _prep_weights(w_in, w_q_b, w_kv_b, w_out, w_ffn_up, w_ffn_gate, w_ffn_down):
    w_in = w_in[0]
    c0 = 4 * D_HGRN
    w_h = w_in.astype(BF16)
    kr_cols = w_in[:, c0 + Q_LORA + KV_LORA:]
    zpad = jnp.zeros((D_MODEL, ROPE_PAD - QK_ROPE), F32)
    w_m = jnp.concatenate([w_in[:, c0:c0 + Q_LORA + KV_LORA], kr_cols, zpad,
                           _swap_halves(kr_cols), zpad], axis=1).astype(BF16)

    wq = w_q_b[0].reshape(Q_LORA, MLA_HEADS, QK_NOPE + QK_ROPE)
    zq = jnp.zeros((Q_LORA, MLA_HEADS, QK_PAD - QK_NOPE - QK_ROPE), F32)
    w_q = jnp.concatenate([wq, zq], axis=-1).reshape(Q_LORA, MLA_HEADS * QK_PAD).astype(BF16)
    zr = jnp.zeros((Q_LORA, MLA_HEADS, ROPE_PAD - QK_ROPE), F32)
    w_q_sw = jnp.concatenate([_swap_halves(wq[..., QK_NOPE:]), zr], axis=-1)
    w_q_sw = w_q_sw.reshape(Q_LORA, MLA_HEADS * ROPE_PAD).astype(BF16)

    wkv = w_kv_b[0].reshape(KV_LORA, MLA_HEADS, QK_NOPE + V_HEAD)
    w_k = wkv[..., :QK_NOPE].reshape(KV_LORA, MLA_HEADS * QK_NOPE).astype(BF16)
    w_vt = wkv[..., QK_NOPE:].reshape(KV_LORA, D_MLA).T.astype(BF16)
    w_uk_t = wkv[..., :QK_NOPE].transpose(1, 2, 0).astype(BF16)
    w_uv = wkv[..., QK_NOPE:].transpose(1, 0, 2).astype(BF16)
    return dict(w_h=w_h, w_m=w_m, w_q=w_q, w_q_sw=w_q_sw, w_k=w_k, w_vt=w_vt, w_uk_t=w_uk_t, w_uv=w_uv,
                w_out=w_out[0].astype(BF16), w_up=w_ffn_up[0].astype(BF16),
                w_gate=w_ffn_gate[0].astype(BF16), w_down=w_ffn_down[0].astype(BF16))


def _row(a):
    return a.reshape(1, -1)


def _tile_plan(batch, seq, n_keys_pad):
    m = batch * seq
    tm = min(ROW_TILE, m)
    tm_big = min(BIG_ROW_TILE, m)
    tm_ffn = min(FFN_ROW_TILE, m)
    assert m % tm_big == 0 and m % tm_ffn == 0 and (tm_ffn % seq == 0 or seq % tm_ffn == 0)
    return dict(
        tm=tm,
        tm_in=tm_big,
        tm_ffn=tm_ffn,
        rows_hgrn=min(HGRN_STEP_ROWS, seq),
        tq=min(ATTN_TILE, seq),
        tk=ATTN_TILE if n_keys_pad % ATTN_TILE == 0 else n_keys_pad,
        tm_kv=ROW_TILE if n_keys_pad % ROW_TILE == 0 else n_keys_pad,
    )


def _encoder(x, s0, lat_past, kr_past, conv_buf, p, w, *, blk):
    batch, seq, _ = x.shape
    past = 0 if lat_past is None else lat_past.shape[1]
    m = batch * seq
    n_keys = past + seq
    key_pad = -n_keys % KEY_ALIGN
    n_keys_pad = n_keys + key_pad
    t = _tile_plan(batch, seq, n_keys_pad)
    x2 = x.reshape(m, D_MODEL)
    cos_k, sin_k = _rope_tables(past, seq, max(seq, t["tm_in"]))

    hg, cqn, lat_new, kr_new, h = _in_proj(x2, p["ln_in_g"], p["ln_in_b"], w["w_h"], w["w_m"],
                                        p["q_a_g"], p["kv_a_g"], cos_k, sin_k, t["tm_in"])
    o_h, s_new = _hgrn(hg, p["lb"], p["hgrn_norm_g"], s0, batch, seq, blk, t["rows_hgrn"])

    q = _q_proj(cqn, w["w_q"], w["w_q_sw"], cos_k, sin_k, t["tm"])
    lat3 = lat_new.reshape(batch, seq, KV_LORA)
    kr3 = kr_new.reshape(batch, seq, ROPE_PAD)
    if past or key_pad:
        parts_l, parts_r = [], []
        if past:
            parts_l.append(lat_past)
            parts_r.append(jnp.pad(kr_past, ((0, 0), (0, 0), (0, ROPE_PAD - QK_ROPE))))
        parts_l.append(lat3)
        parts_r.append(kr3)
        if key_pad:
            parts_l.append(jnp.zeros((batch, key_pad, KV_LORA), F32))
            parts_r.append(jnp.zeros((batch, key_pad, ROPE_PAD), F32))
        lat_all = jnp.concatenate(parts_l, axis=1)
        kr_all = jnp.concatenate(parts_r, axis=1)
    else:
        lat_all, kr_all = lat3, kr3
    if seq * MLA_HEADS <= ABSORB_MAX_ROWS:
        kx = jnp.concatenate([lat_all, kr_all], axis=-1).astype(BF16)
        o_a = _absorbed_attention(q, kx, w["w_uk_t"], w["w_uv"], batch, seq, n_keys, past)
    else:
        k_cat, vt = _kv_proj(lat_all.reshape(batch * n_keys_pad, KV_LORA),
                             kr_all.reshape(batch * n_keys_pad, ROPE_PAD), w["w_k"], w["w_vt"], batch, t["tm_kv"])
        o_a = _attention(q.reshape(batch, seq, MLA_HEADS * QK_PAD),
                         k_cat.reshape(batch, n_keys_pad, MLA_HEADS * QK_PAD), vt,
                         batch, seq, n_keys_pad, n_keys, past, t["tq"], t["tk"]).reshape(m, D_MLA)

    h1, h1_bf16 = _out_proj(o_h, o_a, h, w["w_out"], p["ln1_g"], p["ln1_b"], t["tm"])
    hid, conv_new = _ffn_up(h1_bf16, w["w_up"], w["w_gate"], p["conv_w"], p["conv_b"], conv_buf, seq, t["tm_ffn"],
                            FFN_COL_TILE)
    y = _ffn_down(hid, h1, w["w_down"], p["ln2_g"], p["ln2_b"], t["tm"])
    return (y.reshape(batch, seq, D_MODEL), s_new[None], lat3[None],
            kr3[:, :, :QK_ROPE][None], conv_new[None])


def kernel(x_prompt, x_sample, cache_kv_latent, cache_k_rope, state_hgrn, cache_ffn_conv, lb_param, ln_in_g, ln_in_b, w_in, hgrn_norm_g, q_a_g, w_q_b, kv_a_g, w_kv_b, w_out, ln1_g, ln1_b, w_ffn_up, w_ffn_gate, conv_w, conv_b, w_ffn_down, ln2_g, ln2_b):
    w = _prep_weights(w_in, w_q_b, w_kv_b, w_out, w_ffn_up, w_ffn_gate, w_ffn_down)
    lbs = jnp.cumsum(jax.nn.softmax(lb_param.astype(F32), axis=0), axis=0)
    p = dict(ln_in_g=_row(ln_in_g), ln_in_b=_row(ln_in_b), lb=_row(lbs[0]),
             hgrn_norm_g=_row(hgrn_norm_g[0]), q_a_g=_row(q_a_g[0]), kv_a_g=_row(kv_a_g[0]),
             ln1_g=_row(ln1_g[0]), ln1_b=_row(ln1_b[0]), conv_w=conv_w[0], conv_b=_row(conv_b[0]),
             ln2_g=_row(ln2_g[0]), ln2_b=_row(ln2_b[0]))

    n_prompt = x_prompt.shape[0]
    conv0 = jnp.zeros((n_prompt, CONV_W - 1, D_FF), F32)
    y_p, s_p, lat_p, kr_p, conv_p = _encoder(x_prompt, None, None, None, conv0, p, w, blk=CHUNK)
    y_s, s_s, lat_s, kr_s, conv_s = _encoder(
        x_sample, state_hgrn[0], cache_kv_latent[0], cache_k_rope[0], cache_ffn_conv[0], p, w,
        blk=x_sample.shape[1])
    return (y_p, y_s, lat_p, kr_p, s_p, conv_p, lat_s, kr_s, s_s, conv_s)
```

```python
import functools

import numpy as np
import jax
import jax.numpy as jnp
from jax import lax
from jax.experimental import pallas as pl
from jax.experimental.pallas import tpu as pltpu

F32 = jnp.float32
BF16 = jnp.bfloat16

D_MODEL = 2048
CHUNK = 64
HGRN_HEADS = 8
HGRN_DK = 128
HGRN_DV = 128
D_HGRN = HGRN_HEADS * HGRN_DK
MLA_HEADS = 8
Q_LORA = 512
KV_LORA = 256
QK_NOPE = 128
QK_ROPE = 64
V_HEAD = 128
D_MLA = MLA_HEADS * V_HEAD
D_FF = 5632
CONV_W = 3
ROPE_THETA = 10000.0
LN_EPS = 1e-5
RMS_EPS = 1e-6
DEPTH = 1
ALPHA = (2.0 * DEPTH) ** 0.25

V7X_LANES = 128
V7X_MXU_WIDTH = 256
V7X_BF16_SUBLANES = 16
V7X_VMEM_LIMIT = 56 * 1024 * 1024

QK_PAD = V7X_MXU_WIDTH
ROPE_PAD = V7X_LANES
KEY_ALIGN = V7X_LANES
ONES_ROWS = V7X_BF16_SUBLANES
FFN_SUB = V7X_MXU_WIDTH

ROW_TILE = 512
QKV_ROW_TILE = 1024
FFN_ROW_TILE = 1024
HGRN_STEP_ROWS = 512
ABSORB_MAX_ROWS = 512
ATTN_TILE = 512
FFN_COL_TILE = 1408
ROW_SUB = 256
ATTN_UNROLL = 4
NEG_BIG = -1e30
LOG2_E = 1.4426950408889634

NT_DIMS = (((1,), (1,)), ((), ()))
TN_DIMS = (((0,), (0,)), ((), ()))


def _cparams(sem):
    return pltpu.CompilerParams(dimension_semantics=sem, vmem_limit_bytes=V7X_VMEM_LIMIT)


def _layer_norm(x, g, b):
    mu = jnp.mean(x, axis=-1, keepdims=True)
    xc = x - mu
    var = jnp.mean(xc * xc, axis=-1, keepdims=True)
    return xc * lax.rsqrt(var + LN_EPS) * g + b


def _rms_norm(x, g):
    return x * lax.rsqrt(jnp.mean(x * x, axis=-1, keepdims=True) + RMS_EPS) * g


HG_TILE = 2 * D_HGRN
N_HG_TILES = 4 * D_HGRN // HG_TILE
W_MLA_COLS = Q_LORA + KV_LORA + 2 * ROPE_PAD


def _in_proj_kernel(x_ref, g_ref, b_ref, wh_ref, wm_ref, qg_ref, kvg_ref, cos_ref, sin_ref,
                    hg_ref, cq_ref, lat_ref, kr_ref, h_ref, hb_ref):
    j = pl.program_id(1)

    @pl.when(j == 0)
    def _():
        for r0 in range(0, x_ref.shape[0], ROW_SUB):
            rows = slice(r0, r0 + ROW_SUB)
            h = _layer_norm(x_ref[rows, :], g_ref[...], b_ref[...])
            h_ref[rows, :] = h
            hb = h.astype(BF16)
            hb_ref[rows, :] = hb
            hg_ref[rows, :] = jnp.dot(hb, wh_ref[:, :HG_TILE], preferred_element_type=F32)

    @pl.when(jnp.logical_and(j > 0, j < N_HG_TILES))
    def _():
        cols = pl.ds(pl.multiple_of(j * HG_TILE, HG_TILE), HG_TILE)
        hg_ref[...] = jnp.dot(hb_ref[...], wh_ref[:, cols], preferred_element_type=F32)

    @pl.when(j == N_HG_TILES)
    def _():
        p = jnp.dot(hb_ref[...], wm_ref[...], preferred_element_type=F32)
        cq = p[:, :Q_LORA]
        ckv = p[:, Q_LORA:Q_LORA + KV_LORA]
        kr = p[:, Q_LORA + KV_LORA:Q_LORA + KV_LORA + ROPE_PAD]
        kr_sw = p[:, Q_LORA + KV_LORA + ROPE_PAD:]
        cq_ref[...] = _rms_norm(cq, qg_ref[...]).astype(BF16)
        lat_ref[...] = _rms_norm(ckv, kvg_ref[...])
        kr_ref[...] = kr * cos_ref[...] + kr_sw * sin_ref[...]


def _in_proj(x, ln_g, ln_b, w_h, w_m, q_a_g, kv_a_g, cos_k, sin_k, tm):
    m = x.shape[0]
    n_pos_tiles = cos_k.shape[0] // tm
    grid = (m // tm, N_HG_TILES + 1)
    row = lambda i, j: (i, 0)
    const = lambda i, j: (0, 0)
    pos = lambda i, j: (i % n_pos_tiles, 0)
    hg_col = lambda i, j: (i, jnp.minimum(j, N_HG_TILES - 1))
    return pl.pallas_call(
        _in_proj_kernel,
        grid=grid,
        in_specs=[
            pl.BlockSpec((tm, D_MODEL), row),
            pl.BlockSpec((1, D_MODEL), const),
            pl.BlockSpec((1, D_MODEL), const),
            pl.BlockSpec(w_h.shape, const, pipeline_mode=pl.Buffered(1)),
            pl.BlockSpec((D_MODEL, W_MLA_COLS), const, pipeline_mode=pl.Buffered(1)),
            pl.BlockSpec((1, Q_LORA), const),
            pl.BlockSpec((1, KV_LORA), const),
            pl.BlockSpec((tm, ROPE_PAD), pos),
            pl.BlockSpec((tm, ROPE_PAD), pos),
        ],
        out_specs=[
            pl.BlockSpec((tm, HG_TILE), hg_col),
            pl.BlockSpec((tm, Q_LORA), row),
            pl.BlockSpec((tm, KV_LORA), row),
            pl.BlockSpec((tm, ROPE_PAD), row),
            pl.BlockSpec((tm, D_MODEL), row),
        ],
        out_shape=[
            jax.ShapeDtypeStruct((m, N_HG_TILES * HG_TILE), F32),
            jax.ShapeDtypeStruct((m, Q_LORA), BF16),
            jax.ShapeDtypeStruct((m, KV_LORA), F32),
            jax.ShapeDtypeStruct((m, ROPE_PAD), F32),
            jax.ShapeDtypeStruct((m, D_MODEL), F32),
        ],
        scratch_shapes=[pltpu.VMEM((tm, D_MODEL), BF16)],
        compiler_params=_cparams(("arbitrary", "arbitrary")),
        name="in_proj",
    )(x, ln_g, ln_b, w_h, w_m, q_a_g, kv_a_g, cos_k, sin_k)


def _cumsum_rows(x):
    c = x.shape[0]
    row = lax.broadcasted_iota(jnp.int32, (c, 1), 0)
    shift = 1
    while shift < c:
        x = x + jnp.where(row >= shift, pltpu.roll(x, shift, 0), 0.0)
        shift *= 2
    return x


def _split_level(c):
    t = lax.broadcasted_iota(jnp.int32, (c, c), 0)
    s = lax.broadcasted_iota(jnp.int32, (c, c), 1)
    x = jnp.bitwise_xor(t, s)
    lvl = jnp.full((c, c), -1, jnp.int32)
    for bit in range(c.bit_length() - 1):
        lvl = jnp.where(x >= (1 << bit), bit, lvl)
    return jnp.where(t < s, -2, lvl)


def _hgrn_head_block(q, fpre, v, lb, st, lvl):
    c = q.shape[0]
    f = lb + (1.0 - lb) * jax.nn.sigmoid(fpre)
    g = jnp.log(f)
    k = 1.0 - f
    b = _cumsum_rows(g)
    row = lax.broadcasted_iota(jnp.int32, (c, 1), 0)

    a = jnp.where(lvl == -1, jnp.sum(q * k, axis=-1, keepdims=True), 0.0)
    for level in range(c.bit_length() - 1):
        half = 1 << level
        pos = row % (2 * half)
        is_query = pos >= half
        if level == 0:
            decay = jnp.where(is_query, f, 1.0)
        else:
            if 2 * half < 8:
                ref = b
                for p in range(2 * half):
                    d = half - 1 - p
                    if d != 0:
                        ref = jnp.where(pos == p, pltpu.roll(b, (-d) % c, 0), ref)
            else:
                ref = jnp.concatenate(
                    [jnp.broadcast_to(b[j + half - 1:j + half], (2 * half, b.shape[1]))
                     for j in range(0, c, 2 * half)], axis=0)
            decay = jnp.exp(-jnp.abs(b - ref))
        z = (jnp.where(is_query, q, k) * decay).astype(BF16)
        gram = lax.dot_general(z, z, NT_DIMS, preferred_element_type=F32)
        a = jnp.where(lvl == level, gram, a)

    vb = v.astype(BF16)
    o = jnp.dot(a.astype(BF16), vb, preferred_element_type=F32)
    o = o + lax.dot_general((q * jnp.exp(b)).astype(BF16), st.astype(BF16), NT_DIMS,
                            preferred_element_type=F32)
    b_last = b[c - 1:c]
    k_dec = (k * jnp.exp(b_last - b)).astype(BF16)
    st_new = jnp.exp(b_last) * st + lax.dot_general(vb, k_dec, TN_DIMS, preferred_element_type=F32)
    return o, st_new


def _hgrn_kernel(*refs, c, has_s0):
    if has_s0:
        q_ref, f_ref, i_ref, g_ref, lb_ref, ng_ref, s0_ref, o_ref, s_ref, st_ref = refs
    else:
        q_ref, f_ref, i_ref, g_ref, lb_ref, ng_ref, o_ref, s_ref, st_ref = refs
        s0_ref = None
    t = pl.program_id(1)

    @pl.when(t == 0)
    def _():
        for h in range(HGRN_HEADS):
            if has_s0:
                st_ref[h] = s0_ref[0, h].T
            else:
                st_ref[h] = jnp.zeros((HGRN_DV, HGRN_DK), F32)

    lvl = _split_level(c)

    def block(i, _):
        rows = pl.ds(pl.multiple_of(i * c, c), c)
        for h in range(HGRN_HEADS):
            cols = slice(h * HGRN_DK, (h + 1) * HGRN_DK)
            o, st_new = _hgrn_head_block(q_ref[rows, cols], f_ref[rows, cols], i_ref[rows, cols],
                                         lb_ref[:, cols], st_ref[h], lvl)
            st_ref[h] = st_new
            o = _rms_norm(o, ng_ref[:, cols]) * jax.nn.silu(g_ref[rows, cols])
            o_ref[rows, cols] = o.astype(BF16)
        return 0

    lax.fori_loop(0, q_ref.shape[0] // c, block, 0)

    @pl.when(t == pl.num_programs(1) - 1)
    def _():
        for h in range(HGRN_HEADS):
            s_ref[0, h] = st_ref[h].T


def _hgrn(hg, lb, norm_g, s0, batch, seq, c, rows):
    nt = seq // rows
    has_s0 = s0 is not None
    col = lambda n: (lambda b, t: (b * nt + t, n))
    const = lambda b, t: (0, 0)
    state = lambda b, t: (b, 0, 0, 0)
    in_specs = [pl.BlockSpec((rows, D_HGRN), col(n)) for n in range(4)]
    in_specs += [pl.BlockSpec((1, D_HGRN), const), pl.BlockSpec((1, D_HGRN), const)]
    args = [hg, hg, hg, hg, lb, norm_g]
    if has_s0:
        in_specs.append(pl.BlockSpec((1, HGRN_HEADS, HGRN_DK, HGRN_DV), state))
        args.append(s0)
    return pl.pallas_call(
        functools.partial(_hgrn_kernel, c=c, has_s0=has_s0),
        grid=(batch, nt),
        in_specs=in_specs,
        out_specs=[
            pl.BlockSpec((rows, D_HGRN), lambda b, t: (b * nt + t, 0)),
            pl.BlockSpec((1, HGRN_HEADS, HGRN_DK, HGRN_DV), state),
        ],
        out_shape=[
            jax.ShapeDtypeStruct((batch * seq, D_HGRN), BF16),
            jax.ShapeDtypeStruct((batch, HGRN_HEADS, HGRN_DK, HGRN_DV), F32),
        ],
        scratch_shapes=[pltpu.VMEM((HGRN_HEADS, HGRN_DV, HGRN_DK), F32)],
        compiler_params=_cparams(("arbitrary", "arbitrary")),
        name="hgrn",
    )(*args)


def _q_proj_kernel(cq_ref, wq_ref, wsw_ref, cos_ref, sin_ref, q_ref, *, scale):
    cq = cq_ref[...]
    q_lin = jnp.dot(cq, wq_ref[...], preferred_element_type=F32)
    q_sw = jnp.dot(cq, wsw_ref[...], preferred_element_type=F32)
    cos = cos_ref[...]
    sin = sin_ref[...]
    for h in range(MLA_HEADS):
        base = h * QK_PAD
        q_ref[:, base:base + QK_NOPE] = (q_lin[:, base:base + QK_NOPE] * scale).astype(BF16)
        rot = (q_lin[:, base + QK_NOPE:base + QK_PAD] * cos
               + q_sw[:, h * ROPE_PAD:(h + 1) * ROPE_PAD] * sin)
        q_ref[:, base + QK_NOPE:base + QK_PAD] = (rot * scale).astype(BF16)


def _q_proj(cq, w_q, w_q_sw, cos_k, sin_k, tm):
    m = cq.shape[0]
    n_pos_tiles = cos_k.shape[0] // tm
    scale = LOG2_E * (QK_NOPE + QK_ROPE) ** -0.5
    row = lambda i: (i, 0)
    const = lambda i: (0, 0)
    pos = lambda i: (i % n_pos_tiles, 0)
    return pl.pallas_call(
        functools.partial(_q_proj_kernel, scale=scale),
        grid=(m // tm,),
        in_specs=[
            pl.BlockSpec((tm, Q_LORA), row),
            pl.BlockSpec((Q_LORA, MLA_HEADS * QK_PAD), const),
            pl.BlockSpec((Q_LORA, MLA_HEADS * ROPE_PAD), const),
            pl.BlockSpec((tm, ROPE_PAD), pos),
            pl.BlockSpec((tm, ROPE_PAD), pos),
        ],
        out_specs=pl.BlockSpec((tm, MLA_HEADS * QK_PAD), row),
        out_shape=jax.ShapeDtypeStruct((m, MLA_HEADS * QK_PAD), BF16),
        compiler_params=_cparams(("arbitrary",)),
        name="q_proj",
    )(cq, w_q, w_q_sw, cos_k, sin_k)


def _kv_proj_kernel(lat_ref, kr_ref, wk_ref, wvt_ref, k_ref, vt_ref):
    lat = lat_ref[...].astype(BF16)
    kn = jnp.dot(lat, wk_ref[...], preferred_element_type=F32)
    vt_ref[0] = lax.dot_general(wvt_ref[...], lat, NT_DIMS, preferred_element_type=F32).astype(BF16)
    kr = kr_ref[...].astype(BF16)
    for h in range(MLA_HEADS):
        base = h * QK_PAD
        k_ref[:, base:base + QK_NOPE] = kn[:, h * QK_NOPE:(h + 1) * QK_NOPE].astype(BF16)
        k_ref[:, base + QK_NOPE:base + QK_PAD] = kr


def _kv_proj(lat, kr_pad, w_k, w_vt, batch, tm):
    m = lat.shape[0]
    tiles_per_batch = m // batch // tm
    row = lambda i: (i, 0)
    const = lambda i: (0, 0)
    return pl.pallas_call(
        _kv_proj_kernel,
        grid=(m // tm,),
        in_specs=[
            pl.BlockSpec((tm, KV_LORA), row),
            pl.BlockSpec((tm, ROPE_PAD), row),
            pl.BlockSpec((KV_LORA, MLA_HEADS * QK_NOPE), const),
            pl.BlockSpec((D_MLA, KV_LORA), const),
        ],
        out_specs=[
            pl.BlockSpec((tm, MLA_HEADS * QK_PAD), row),
            pl.BlockSpec((1, D_MLA, tm), lambda i: (i // tiles_per_batch, 0, i % tiles_per_batch)),
        ],
        out_shape=[
            jax.ShapeDtypeStruct((m, MLA_HEADS * QK_PAD), BF16),
            jax.ShapeDtypeStruct((batch, D_MLA, m // batch), BF16),
        ],
        compiler_params=_cparams(("arbitrary",)),
        name="kv_proj",
    )(lat, kr_pad, w_k, w_vt)


def _attn_kernel(tile_q_ref, tile_k_ref, q_ref, k_ref, vt_ref, o_ref,
                 vx_ref, rel_ref, sa_ref, sb_ref, m_ref, acc_ref, *, tq, tk, past, n_keys, n_plain, n_tiles):
    @pl.when(jnp.logical_and(pl.program_id(0) == 0, pl.program_id(1) == 0))
    def _():
        rel_ref[...] = (lax.broadcasted_iota(jnp.int32, (tk, tq), 0) // CHUNK
                        - lax.broadcasted_iota(jnp.int32, (tk, tq), 1) // CHUNK)

    vx_ref[:V_HEAD, :] = vt_ref[0]
    vx_ref[V_HEAD:, :] = jnp.ones((ONES_ROWS, vx_ref.shape[1]), BF16)
    m_ref[...] = jnp.full(m_ref.shape, NEG_BIG, F32)
    acc_ref[...] = jnp.zeros(acc_ref.shape, F32)
    k_row = lax.broadcasted_iota(jnp.int32, (tk, tq), 0)

    def origin(t):
        return pl.multiple_of(tile_q_ref[t] * tq, tq), pl.multiple_of(tile_k_ref[t] * tk, tk)

    def scores(t, s_ref):
        qs, ks = origin(t)
        s_ref[...] = lax.dot_general(k_ref[0, pl.ds(ks, tk), :], q_ref[0, pl.ds(qs, tq), :], NT_DIMS,
                                     preferred_element_type=F32)

    def softmax_pv(t, s_ref, masked):
        qs, ks = origin(t)
        s = s_ref[...]
        if masked:
            pad = jnp.where(k_row < n_keys - ks, 0, 1 << 20)
            visible = rel_ref[...] + pad <= (past + qs - ks) // CHUNK
            s = jnp.where(visible, s, NEG_BIG)
        cols = pl.ds(qs, tq)
        m = m_ref[:, cols]
        m_new = jnp.maximum(m, jnp.max(s, axis=0, keepdims=True))
        p = jnp.exp2(s - m_new).astype(BF16)
        pv = jnp.dot(vx_ref[:, pl.ds(ks, tk)], p, preferred_element_type=F32)
        acc_ref[:, cols] = jnp.exp2(m - m_new) * acc_ref[:, cols] + pv
        m_ref[:, cols] = m_new

    bufs = (sa_ref, sb_ref)

    def step(t, parity, masked):
        scores(t + 1, bufs[1 - parity])
        softmax_pv(t, bufs[parity], masked)

    def run(t0, t1, masked):
        t = t0
        if t < t1 and t % 2 == 1:
            step(t, 1, masked)
            t += 1
        trips = (t1 - t) // ATTN_UNROLL
        if trips > 0:
            def body(i, _, base=t):
                tt = base + ATTN_UNROLL * i
                for u in range(ATTN_UNROLL):
                    step(tt + u, u % 2, masked)
                return 0
            lax.fori_loop(0, trips, body, 0)
            t += ATTN_UNROLL * trips
        while t < t1:
            step(t, t % 2, masked)
            t += 1

    scores(0, sa_ref)
    run(0, min(n_plain, n_tiles - 1), False)
    run(min(n_plain, n_tiles - 1), n_tiles - 1, True)
    last = n_tiles - 1
    softmax_pv(last, bufs[last % 2], last >= n_plain)
    acc = acc_ref[...]
    o_ref[0] = (acc[:V_HEAD] / acc[V_HEAD:V_HEAD + 1]).T.astype(BF16)


def _attention_tiles(seq, n_keys_pad, n_keys, past, tq, tk):
    plain, masked = [], []
    for j in range(n_keys_pad // tk):
        for i in range(seq // tq):
            q_lo, k_lo = past + i * tq, j * tk
            q_hi, k_hi = q_lo + tq - 1, k_lo + tk - 1
            if k_lo // CHUNK > q_hi // CHUNK or k_lo >= n_keys:
                continue
            if k_hi // CHUNK <= q_lo // CHUNK and k_hi < n_keys:
                plain.append((i, j))
            else:
                masked.append((i, j))
    return plain, masked


def _attention(q, k, vt, batch, seq, n_keys_pad, n_keys, past, tq, tk):
    assert past % CHUNK == 0 and tk % CHUNK == 0 and (tq % CHUNK == 0 or tq == seq)
    plain, masked = _attention_tiles(seq, n_keys_pad, n_keys, past, tq, tk)
    tiles = plain + masked
    tile_q = jnp.asarray([t[0] for t in tiles], jnp.int32)
    tile_k = jnp.asarray([t[1] for t in tiles], jnp.int32)
    smem = pl.BlockSpec(memory_space=pltpu.SMEM)
    return pl.pallas_call(
        functools.partial(_attn_kernel, tq=tq, tk=tk, past=past, n_keys=n_keys,
                          n_plain=len(plain), n_tiles=len(tiles)),
        grid=(batch, MLA_HEADS),
        in_specs=[
            smem, smem,
            pl.BlockSpec((1, seq, QK_PAD), lambda b, h: (b, 0, h)),
            pl.BlockSpec((1, n_keys_pad, QK_PAD), lambda b, h: (b, 0, h)),
            pl.BlockSpec((1, V_HEAD, n_keys_pad), lambda b, h: (b, h, 0)),
        ],
        out_specs=pl.BlockSpec((1, seq, V_HEAD), lambda b, h: (b, 0, h)),
        out_shape=jax.ShapeDtypeStruct((batch, seq, D_MLA), BF16),
        scratch_shapes=[
            pltpu.VMEM((V_HEAD + ONES_ROWS, n_keys_pad), BF16),
            pltpu.VMEM((tk, tq), jnp.int32),
            pltpu.VMEM((tk, tq), F32),
            pltpu.VMEM((tk, tq), F32),
            pltpu.VMEM((1, seq), F32),
            pltpu.VMEM((V_HEAD + ONES_ROWS, seq), F32),
        ],
        compiler_params=_cparams(("arbitrary", "arbitrary")),
        name="attn",
    )(tile_q, tile_k, q, k, vt)


LAT_KEY = KV_LORA + ROPE_PAD


def _absorbed_attn_kernel(q_ref, kx_ref, wuk_ref, wuv_ref, o_ref, qx_ref, vx_ref, *, seq, past, n_keys):
    for h in range(MLA_HEADS):
        rows = slice(h * seq, (h + 1) * seq)
        q_lat = jnp.dot(q_ref[:, h * QK_PAD:h * QK_PAD + QK_NOPE], wuk_ref[h], preferred_element_type=F32)
        qx_ref[rows, :KV_LORA] = q_lat.astype(BF16)
        qx_ref[rows, KV_LORA:] = q_ref[:, h * QK_PAD + QK_NOPE:(h + 1) * QK_PAD]
    kx = kx_ref[0]
    s = lax.dot_general(qx_ref[...], kx, NT_DIMS, preferred_element_type=F32)
    q_chunk = (past + lax.broadcasted_iota(jnp.int32, (s.shape[0], 1), 0) % seq) // CHUNK
    k_idx = lax.broadcasted_iota(jnp.int32, (1, s.shape[1]), 1)
    visible = jnp.logical_and(k_idx // CHUNK <= q_chunk, k_idx < n_keys)
    s = jnp.where(visible, s, NEG_BIG)
    p = jnp.exp2(s - jnp.max(s, axis=-1, keepdims=True)).astype(BF16)
    vx_ref[:, :KV_LORA] = kx[:, :KV_LORA]
    vx_ref[:, KV_LORA:] = jnp.ones((vx_ref.shape[0], ROPE_PAD), BF16)
    o_ext = jnp.dot(p, vx_ref[...], preferred_element_type=F32)
    denom = o_ext[:, KV_LORA:]
    o_lat = (o_ext[:, :KV_LORA] / jnp.concatenate([denom] * (KV_LORA // ROPE_PAD), axis=1)).astype(BF16)
    for h in range(MLA_HEADS):
        o_ref[:, h * V_HEAD:(h + 1) * V_HEAD] = jnp.dot(
            o_lat[h * seq:(h + 1) * seq], wuv_ref[h], preferred_element_type=F32).astype(BF16)


def _absorbed_attention(q, kx, w_uk_t, w_uv, batch, seq, n_keys, past):
    n_keys_pad = kx.shape[1]
    const3 = lambda b: (0, 0, 0)
    return pl.pallas_call(
        functools.partial(_absorbed_attn_kernel, seq=seq, past=past, n_keys=n_keys),
        grid=(batch,),
        in_specs=[
            pl.BlockSpec((seq, MLA_HEADS * QK_PAD), lambda b: (b, 0)),
            pl.BlockSpec((1, n_keys_pad, LAT_KEY), lambda b: (b, 0, 0)),
            pl.BlockSpec((MLA_HEADS, QK_NOPE, KV_LORA), const3),
            pl.BlockSpec((MLA_HEADS, KV_LORA, V_HEAD), const3),
        ],
        out_specs=pl.BlockSpec((seq, D_MLA), lambda b: (b, 0)),
        out_shape=jax.ShapeDtypeStruct((batch * seq, D_MLA), BF16),
        scratch_shapes=[
            pltpu.VMEM((MLA_HEADS * seq, LAT_KEY), BF16),
            pltpu.VMEM((n_keys_pad, LAT_KEY), BF16),
        ],
        compiler_params=_cparams(("arbitrary",)),
        name="absorbed_attn",
    )(q, kx, w_uk_t, w_uv)


def _out_proj_kernel(oh_ref, oa_ref, h_ref, w_ref, g1_ref, b1_ref, y_ref, yb_ref):
    for r0 in range(0, h_ref.shape[0], ROW_SUB):
        rows = slice(r0, r0 + ROW_SUB)
        mix = jnp.dot(oh_ref[rows, :], w_ref[:D_HGRN, :], preferred_element_type=F32)
        mix = mix + jnp.dot(oa_ref[rows, :], w_ref[D_HGRN:, :], preferred_element_type=F32)
        y = _layer_norm(ALPHA * h_ref[rows, :] + mix, g1_ref[...], b1_ref[...])
        y_ref[rows, :] = y
        yb_ref[rows, :] = y.astype(BF16)


def _out_proj(oh, oa, h, w_out, ln1_g, ln1_b, tm):
    m = h.shape[0]
    row = lambda i: (i, 0)
    const = lambda i: (0, 0)
    vec = pl.BlockSpec((1, D_MODEL), const)
    return pl.pallas_call(
        _out_proj_kernel,
        grid=(m // tm,),
        in_specs=[
            pl.BlockSpec((tm, D_HGRN), row),
            pl.BlockSpec((tm, D_MLA), row),
            pl.BlockSpec((tm, D_MODEL), row),
            pl.BlockSpec((D_HGRN + D_MLA, D_MODEL), const),
            vec, vec,
        ],
        out_specs=[pl.BlockSpec((tm, D_MODEL), row), pl.BlockSpec((tm, D_MODEL), row)],
        out_shape=[jax.ShapeDtypeStruct((m, D_MODEL), F32), jax.ShapeDtypeStruct((m, D_MODEL), BF16)],
        compiler_params=_cparams(("arbitrary",)),
        name="out_proj",
    )(oh, oa, h, w_out, ln1_g, ln1_b)


def _ffn_up_kernel(x_ref, wu_ref, wg_ref, cw_ref, cb_ref, buf_ref, h_ref, cnew_ref, carry_ref, *,
                   spt, tiles_per_seq):
    i = pl.program_id(1)
    tm = x_ref.shape[0]
    tf = wu_ref.shape[1]
    rows = tm // spt

    if tiles_per_seq > 1:
        @pl.when(i % tiles_per_seq == 0)
        def _():
            carry_ref[...] = buf_ref[...]
        halo_ref = carry_ref
    else:
        halo_ref = buf_ref

    xb = x_ref[...]
    for c0 in range(0, tf, FFN_SUB):
        width = min(FFN_SUB, tf - c0)
        cs = slice(c0, c0 + width)
        r = lax.broadcasted_iota(jnp.int32, (spt, rows, width), 1)
        if 2 * width <= FFN_SUB:
            both = jnp.dot(xb, jnp.concatenate([wu_ref[:, cs], wg_ref[:, cs]], axis=1),
                           preferred_element_type=F32)
            u, gate = both[:, :width], both[:, width:]
        else:
            u = jnp.dot(xb, wu_ref[:, cs], preferred_element_type=F32)
            gate = jnp.dot(xb, wg_ref[:, cs], preferred_element_type=F32)
        u = u.reshape(spt, rows, width)
        gate = gate.reshape(spt, rows, width)
        halo = halo_ref[:, :, cs]
        prev1 = jnp.where(r == 0, halo[:, 1:2, :], pltpu.roll(u, 1, 1))
        prev2 = jnp.where(r == 0, halo[:, 0:1, :],
                          jnp.where(r == 1, halo[:, 1:2, :], pltpu.roll(u, 2, 1)))
        cw = cw_ref[:, cs]
        a = cb_ref[:, cs] + (prev2 * cw[0:1] + prev1 * cw[1:2] + u * cw[2:3])
        h_ref[:, cs] = (jax.nn.silu(a) * gate).astype(BF16).reshape(tm, width)
        tail = u[:, rows - (CONV_W - 1):, :]
        cnew_ref[:, :, cs] = tail
        if tiles_per_seq > 1:
            carry_ref[:, :, cs] = tail


def _ffn_up(x, w_up, w_gate, conv_w, conv_b, conv_buf, seq, tm, tf):
    m = x.shape[0]
    n_seq = m // seq
    if tm >= seq:
        spt, tiles_per_seq = tm // seq, 1
        seq_of = lambda i: i
    else:
        spt, tiles_per_seq = 1, seq // tm
        seq_of = lambda i: i // tiles_per_seq
    n_f = D_FF // tf
    hid, tails = pl.pallas_call(
        functools.partial(_ffn_up_kernel, spt=spt, tiles_per_seq=tiles_per_seq),
        grid=(n_f, m // tm),
        in_specs=[
            pl.BlockSpec((tm, D_MODEL), lambda f, i: (i, 0)),
            pl.BlockSpec((D_MODEL, tf), lambda f, i: (0, f)),
            pl.BlockSpec((D_MODEL, tf), lambda f, i: (0, f)),
            pl.BlockSpec((CONV_W, tf), lambda f, i: (0, f)),
            pl.BlockSpec((1, tf), lambda f, i: (0, f)),
            pl.BlockSpec((spt, CONV_W - 1, tf), lambda f, i: (seq_of(i), 0, f)),
        ],
        out_specs=[
            pl.BlockSpec((tm, tf), lambda f, i: (i, f)),
            pl.BlockSpec((spt, CONV_W - 1, tf), lambda f, i: (i, 0, f)),
        ],
        out_shape=[
            jax.ShapeDtypeStruct((m, D_FF), BF16),
            jax.ShapeDtypeStruct((n_seq * tiles_per_seq, CONV_W - 1, D_FF), F32),
        ],
        scratch_shapes=[pltpu.VMEM((spt, CONV_W - 1, tf), F32)],
        compiler_params=_cparams(("arbitrary", "arbitrary")),
        name="ffn_up",
    )(x, w_up, w_gate, conv_w, conv_b, conv_buf)
    return hid, tails[tiles_per_seq - 1::tiles_per_seq]


def _ffn_down_kernel(h_ref, x_ref, w_ref, g_ref, b_ref, y_ref):
    for r0 in range(0, x_ref.shape[0], ROW_SUB):
        rows = slice(r0, r0 + ROW_SUB)
        ff = jnp.dot(h_ref[rows, :], w_ref[...], preferred_element_type=F32)
        y_ref[rows, :] = _layer_norm(ALPHA * x_ref[rows, :] + ff, g_ref[...], b_ref[...])


def _ffn_down(hid, x, w_down, ln2_g, ln2_b, tm):
    m = x.shape[0]
    row = lambda i: (i, 0)
    const = lambda i: (0, 0)
    return pl.pallas_call(
        _ffn_down_kernel,
        grid=(m // tm,),
        in_specs=[
            pl.BlockSpec((tm, D_FF), row),
            pl.BlockSpec((tm, D_MODEL), row),
            pl.BlockSpec((D_FF, D_MODEL), const, pipeline_mode=pl.Buffered(1)),
            pl.BlockSpec((1, D_MODEL), const),
            pl.BlockSpec((1, D_MODEL), const),
        ],
        out_specs=pl.BlockSpec((tm, D_MODEL), row),
        out_shape=jax.ShapeDtypeStruct((m, D_MODEL), F32),
        compiler_params=_cparams(("arbitrary",)),
        name="ffn_down",
    )(hid, x, w_down, ln2_g, ln2_b)


def _rope_tables(past, seq, rows):
    inv = ROPE_THETA ** (-np.arange(0, QK_ROPE, 2, dtype=np.float64) / QK_ROPE)
    ang = (past + np.arange(seq, dtype=np.float64))[:, None] * inv[None, :]
    cos, sin = np.cos(ang), np.sin(ang)
    zero = np.zeros((seq, ROPE_PAD - QK_ROPE))
    cos_t = np.concatenate([cos, cos, zero], axis=1).astype(np.float32)
    sin_t = np.concatenate([-sin, sin, zero], axis=1).astype(np.float32)
    reps = rows // seq
    return jnp.asarray(np.tile(cos_t, (reps, 1))), jnp.asarray(np.tile(sin_t, (reps, 1)))


def _swap_halves(w):
    half = QK_ROPE // 2
    return jnp.concatenate([w[..., half:], w[..., :half]], axis=-1)


def _prep_weights(w_in, w_q_b, w_kv_b, w_out, w_ffn_up, w_ffn_gate, w_ffn_down):
    w_in = w_in[0]
    c0 = 4 * D_HGRN
    w_h = w_in.astype(BF16)
    kr_cols = w_in[:, c0 + Q_LORA + KV_LORA:]
    zpad = jnp.zeros((D_MODEL, ROPE_PAD - QK_ROPE), F32)
    w_m = jnp.concatenate([w_in[:, c0:c0 + Q_LORA + KV_LORA], kr_cols, zpad,
                           _swap_halves(kr_cols), zpad], axis=1).astype(BF16)

    wq = w_q_b[0].reshape(Q_LORA, MLA_HEADS, QK_NOPE + QK_ROPE)
    zq = jnp.zeros((Q_LORA, MLA_HEADS, QK_PAD - QK_NOPE - QK_ROPE), F32)
    w_q = jnp.concatenate([wq, zq], axis=-1).reshape(Q_LORA, MLA_HEADS * QK_PAD).astype(BF16)
    zr = jnp.zeros((Q_LORA, MLA_HEADS, ROPE_PAD - QK_ROPE), F32)
    w_q_sw = jnp.concatenate([_swap_halves(wq[..., QK_NOPE:]), zr], axis=-1)
    w_q_sw = w_q_sw.reshape(Q_LORA, MLA_HEADS * ROPE_PAD).astype(BF16)

    wkv = w_kv_b[0].reshape(KV_LORA, MLA_HEADS, QK_NOPE + V_HEAD)
    w_k = wkv[..., :QK_NOPE].reshape(KV_LORA, MLA_HEADS * QK_NOPE).astype(BF16)
    w_vt = wkv[..., QK_NOPE:].reshape(KV_LORA, D_MLA).T.astype(BF16)
    w_uk_t = wkv[..., :QK_NOPE].transpose(1, 2, 0).astype(BF16)
    w_uv = wkv[..., QK_NOPE:].transpose(1, 0, 2).astype(BF16)
    return dict(w_h=w_h, w_m=w_m, w_q=w_q, w_q_sw=w_q_sw, w_k=w_k, w_vt=w_vt, w_uk_t=w_uk_t, w_uv=w_uv,
                w_out=w_out[0].astype(BF16), w_up=w_ffn_up[0].astype(BF16),
                w_gate=w_ffn_gate[0].astype(BF16), w_down=w_ffn_down[0].astype(BF16))


def _row(a):
    return a.reshape(1, -1)


def _tile_plan(batch, seq, n_keys_pad):
    m = batch * seq
    tm = min(ROW_TILE, m)
    tm_q = min(QKV_ROW_TILE, m)
    tm_ffn = min(FFN_ROW_TILE, m)
    assert m % tm == 0 and m % tm_q == 0 and m % tm_ffn == 0 and (tm_ffn % seq == 0 or seq % tm_ffn == 0)
    return dict(
        tm=tm,
        tm_q=tm_q,
        tm_ffn=tm_ffn,
        rows_hgrn=min(HGRN_STEP_ROWS, seq),
        tq=min(ATTN_TILE, seq),
        tk=ATTN_TILE if n_keys_pad % ATTN_TILE == 0 else n_keys_pad,
        tm_kv=QKV_ROW_TILE if n_keys_pad % QKV_ROW_TILE == 0 else n_keys_pad,
    )


def _encoder(x, s0, lat_past, kr_past, conv_buf, p, w, *, blk):
    batch, seq, _ = x.shape
    past = 0 if lat_past is None else lat_past.shape[1]
    m = batch * seq
    n_keys = past + seq
    key_pad = -n_keys % KEY_ALIGN
    n_keys_pad = n_keys + key_pad
    t = _tile_plan(batch, seq, n_keys_pad)
    x2 = x.reshape(m, D_MODEL)
    cos_k, sin_k = _rope_tables(past, seq, max(seq, t["tm_q"]))

    hg, cqn, lat_new, kr_new, h = _in_proj(x2, p["ln_in_g"], p["ln_in_b"], w["w_h"], w["w_m"],
                                           p["q_a_g"], p["kv_a_g"], cos_k, sin_k, t["tm"])
    o_h, s_new = _hgrn(hg, p["lb"], p["hgrn_norm_g"], s0, batch, seq, blk, t["rows_hgrn"])

    q = _q_proj(cqn, w["w_q"], w["w_q_sw"], cos_k, sin_k, t["tm_q"])
    lat3 = lat_new.reshape(batch, seq, KV_LORA)
    kr3 = kr_new.reshape(batch, seq, ROPE_PAD)
    if past or key_pad:
        parts_l, parts_r = [], []
        if past:
            parts_l.append(lat_past)
            parts_r.append(jnp.pad(kr_past, ((0, 0), (0, 0), (0, ROPE_PAD - QK_ROPE))))
        parts_l.append(lat3)
        parts_r.append(kr3)
        if key_pad:
            parts_l.append(jnp.zeros((batch, key_pad, KV_LORA), F32))
            parts_r.append(jnp.zeros((batch, key_pad, ROPE_PAD), F32))
        lat_all = jnp.concatenate(parts_l, axis=1)
        kr_all = jnp.concatenate(parts_r, axis=1)
    else:
        lat_all, kr_all = lat3, kr3
    if seq * MLA_HEADS <= ABSORB_MAX_ROWS:
        kx = jnp.concatenate([lat_all, kr_all], axis=-1).astype(BF16)
        o_a = _absorbed_attention(q, kx, w["w_uk_t"], w["w_uv"], batch, seq, n_keys, past)
    else:
        k_cat, vt = _kv_proj(lat_all.reshape(batch * n_keys_pad, KV_LORA),
                             kr_all.reshape(batch * n_keys_pad, ROPE_PAD), w["w_k"], w["w_vt"], batch, t["tm_kv"])
        o_a = _attention(q.reshape(batch, seq, MLA_HEADS * QK_PAD),
                         k_cat.reshape(batch, n_keys_pad, MLA_HEADS * QK_PAD), vt,
                         batch, seq, n_keys_pad, n_keys, past, t["tq"], t["tk"]).reshape(m, D_MLA)

    h1, h1_bf16 = _out_proj(o_h, o_a, h, w["w_out"], p["ln1_g"], p["ln1_b"], t["tm"])
    hid, conv_new = _ffn_up(h1_bf16, w["w_up"], w["w_gate"], p["conv_w"], p["conv_b"], conv_buf, seq, t["tm_ffn"],
                            FFN_COL_TILE)
    y = _ffn_down(hid, h1, w["w_down"], p["ln2_g"], p["ln2_b"], t["tm"])
    return (y.reshape(batch, seq, D_MODEL), s_new[None], lat3[None],
            kr3[:, :, :QK_ROPE][None], conv_new[None])


def kernel(x_prompt, x_sample, cache_kv_latent, cache_k_rope, state_hgrn, cache_ffn_conv, lb_param, ln_in_g, ln_in_b, w_in, hgrn_norm_g, q_a_g, w_q_b, kv_a_g, w_kv_b, w_out, ln1_g, ln1_b, w_ffn_up, w_ffn_gate, conv_w, conv_b, w_ffn_down, ln2_g, ln2_b):
    w = _prep_weights(w_in, w_q_b, w_kv_b, w_out, w_ffn_up, w_ffn_gate, w_ffn_down)
    lbs = jnp.cumsum(jax.nn.softmax(lb_param.astype(F32), axis=0), axis=0)
    p = dict(ln_in_g=_row(ln_in_g), ln_in_b=_row(ln_in_b), lb=_row(lbs[0]),
             hgrn_norm_g=_row(hgrn_norm_g[0]), q_a_g=_row(q_a_g[0]), kv_a_g=_row(kv_a_g[0]),
             ln1_g=_row(ln1_g[0]), ln1_b=_row(ln1_b[0]), conv_w=conv_w[0], conv_b=_row(conv_b[0]),
             ln2_g=_row(ln2_g[0]), ln2_b=_row(ln2_b[0]))

    n_prompt = x_prompt.shape[0]
    conv0 = jnp.zeros((n_prompt, CONV_W - 1, D_FF), F32)
    y_p, s_p, lat_p, kr_p, conv_p = _encoder(x_prompt, None, None, None, conv0, p, w, blk=CHUNK)
    y_s, s_s, lat_s, kr_s, conv_s = _encoder(
        x_sample, state_hgrn[0], cache_kv_latent[0], cache_k_rope[0], cache_ffn_conv[0], p, w,
        blk=x_sample.shape[1])
    return (y_p, y_s, lat_p, kr_p, s_p, conv_p, lat_s, kr_s, s_s, conv_s)
```

```python
import functools

import numpy as np
import jax
import jax.numpy as jnp
from jax import lax
from jax.experimental import pallas as pl
from jax.experimental.pallas import tpu as pltpu

F32 = jnp.float32
BF16 = jnp.bfloat16

D_MODEL = 2048
CHUNK = 64
HGRN_HEADS = 8
HGRN_DK = 128
HGRN_DV = 128
D_HGRN = HGRN_HEADS * HGRN_DK
MLA_HEADS = 8
Q_LORA = 512
KV_LORA = 256
QK_NOPE = 128
QK_ROPE = 64
V_HEAD = 128
D_MLA = MLA_HEADS * V_HEAD
D_FF = 5632
CONV_W = 3
ROPE_THETA = 10000.0
LN_EPS = 1e-5
RMS_EPS = 1e-6
DEPTH = 1
ALPHA = (2.0 * DEPTH) ** 0.25

V7X_LANES = 128
V7X_MXU_WIDTH = 256
V7X_BF16_SUBLANES = 16
V7X_VMEM_LIMIT = 56 * 1024 * 1024

QK_PAD = V7X_MXU_WIDTH
ROPE_PAD = V7X_LANES
KEY_ALIGN = V7X_LANES
ONES_ROWS = V7X_BF16_SUBLANES
FFN_SUB = V7X_MXU_WIDTH

ROW_TILE = 512
QKV_ROW_TILE = 1024
FFN_ROW_TILE = 1024
HGRN_STEP_ROWS = 512
ABSORB_MAX_ROWS = 512
ATTN_TILE = 512
FFN_COL_TILE = 1408
ROW_SUB = 256
ATTN_UNROLL = 4
NEG_BIG = -1e30
LOG2_E = 1.4426950408889634

NT_DIMS = (((1,), (1,)), ((), ()))
TN_DIMS = (((0,), (0,)), ((), ()))


def _cparams(sem):
    return pltpu.CompilerParams(dimension_semantics=sem, vmem_limit_bytes=V7X_VMEM_LIMIT)


def _layer_norm(x, g, b):
    mu = jnp.mean(x, axis=-1, keepdims=True)
    xc = x - mu
    var = jnp.mean(xc * xc, axis=-1, keepdims=True)
    return xc * lax.rsqrt(var + LN_EPS) * g + b


def _rms_norm(x, g):
    return x * lax.rsqrt(jnp.mean(x * x, axis=-1, keepdims=True) + RMS_EPS) * g


HG_TILE = 2 * D_HGRN
N_HG_TILES = 4 * D_HGRN // HG_TILE
W_MLA_COLS = 2 * ROPE_PAD


def _in_proj_kernel(x_ref, g_ref, b_ref, wh_ref, wm_ref, qg_ref, kvg_ref, cos_ref, sin_ref,
                    hg_ref, cq_ref, lat_ref, kr_ref, h_ref, hb_ref):
    j = pl.program_id(1)

    @pl.when(j == 0)
    def _():
        for r0 in range(0, x_ref.shape[0], ROW_SUB):
            rows = slice(r0, r0 + ROW_SUB)
            h = _layer_norm(x_ref[rows, :], g_ref[...], b_ref[...])
            h_ref[rows, :] = h
            hb = h.astype(BF16)
            hb_ref[rows, :] = hb
            hg_ref[rows, :] = jnp.dot(hb, wh_ref[:, :HG_TILE], preferred_element_type=F32)

    @pl.when(jnp.logical_and(j > 0, j < N_HG_TILES))
    def _():
        cols = pl.ds(pl.multiple_of(j * HG_TILE, HG_TILE), HG_TILE)
        hg_ref[...] = jnp.dot(hb_ref[...], wh_ref[:, cols], preferred_element_type=F32)

    @pl.when(j == N_HG_TILES)
    def _():
        hb = hb_ref[...]
        lat_cols = pl.ds(N_HG_TILES * HG_TILE, Q_LORA + KV_LORA)
        p = jnp.dot(hb, wh_ref[:, lat_cols], preferred_element_type=F32)
        r = jnp.dot(hb, wm_ref[...], preferred_element_type=F32)
        cq = p[:, :Q_LORA]
        ckv = p[:, Q_LORA:]
        kr = r[:, :ROPE_PAD]
        kr_sw = r[:, ROPE_PAD:]
        cq_ref[...] = _rms_norm(cq, qg_ref[...]).astype(BF16)
        lat_ref[...] = _rms_norm(ckv, kvg_ref[...])
        kr_ref[...] = kr * cos_ref[...] + kr_sw * sin_ref[...]


def _in_proj(x, ln_g, ln_b, w_h, w_m, q_a_g, kv_a_g, cos_k, sin_k, tm):
    m = x.shape[0]
    n_pos_tiles = cos_k.shape[0] // tm
    grid = (m // tm, N_HG_TILES + 1)
    row = lambda i, j: (i, 0)
    const = lambda i, j: (0, 0)
    pos = lambda i, j: (i % n_pos_tiles, 0)
    hg_col = lambda i, j: (i, jnp.minimum(j, N_HG_TILES - 1))
    return pl.pallas_call(
        _in_proj_kernel,
        grid=grid,
        in_specs=[
            pl.BlockSpec((tm, D_MODEL), row),
            pl.BlockSpec((1, D_MODEL), const),
            pl.BlockSpec((1, D_MODEL), const),
            pl.BlockSpec(w_h.shape, const, pipeline_mode=pl.Buffered(1)),
            pl.BlockSpec((D_MODEL, W_MLA_COLS), const, pipeline_mode=pl.Buffered(1)),
            pl.BlockSpec((1, Q_LORA), const),
            pl.BlockSpec((1, KV_LORA), const),
            pl.BlockSpec((tm, ROPE_PAD), pos),
            pl.BlockSpec((tm, ROPE_PAD), pos),
        ],
        out_specs=[
            pl.BlockSpec((tm, HG_TILE), hg_col),
            pl.BlockSpec((tm, Q_LORA), row),
            pl.BlockSpec((tm, KV_LORA), row),
            pl.BlockSpec((tm, ROPE_PAD), row),
            pl.BlockSpec((tm, D_MODEL), row),
        ],
        out_shape=[
            jax.ShapeDtypeStruct((m, N_HG_TILES * HG_TILE), F32),
            jax.ShapeDtypeStruct((m, Q_LORA), BF16),
            jax.ShapeDtypeStruct((m, KV_LORA), F32),
            jax.ShapeDtypeStruct((m, ROPE_PAD), F32),
            jax.ShapeDtypeStruct((m, D_MODEL), F32),
        ],
        scratch_shapes=[pltpu.VMEM((tm, D_MODEL), BF16)],
        compiler_params=_cparams(("arbitrary", "arbitrary")),
        name="in_proj",
    )(x, ln_g, ln_b, w_h, w_m, q_a_g, kv_a_g, cos_k, sin_k)


def _cumsum_rows(x):
    c = x.shape[0]
    row = lax.broadcasted_iota(jnp.int32, (c, 1), 0)
    shift = 1
    while shift < c:
        x = x + jnp.where(row >= shift, pltpu.roll(x, shift, 0), 0.0)
        shift *= 2
    return x


def _split_level(c):
    t = lax.broadcasted_iota(jnp.int32, (c, c), 0)
    s = lax.broadcasted_iota(jnp.int32, (c, c), 1)
    x = jnp.bitwise_xor(t, s)
    lvl = jnp.full((c, c), -1, jnp.int32)
    for bit in range(c.bit_length() - 1):
        lvl = jnp.where(x >= (1 << bit), bit, lvl)
    return jnp.where(t < s, -2, lvl)


def _hgrn_head_block(q, fpre, v, lb, st, lvl):
    c = q.shape[0]
    f = lb + (1.0 - lb) * jax.nn.sigmoid(fpre)
    g = jnp.log(f)
    k = 1.0 - f
    b = _cumsum_rows(g)
    row = lax.broadcasted_iota(jnp.int32, (c, 1), 0)

    a = jnp.where(lvl == -1, jnp.sum(q * k, axis=-1, keepdims=True), 0.0)
    for level in range(c.bit_length() - 1):
        half = 1 << level
        pos = row % (2 * half)
        is_query = pos >= half
        if level == 0:
            decay = jnp.where(is_query, f, 1.0)
        else:
            if 2 * half < 8:
                ref = b
                for p in range(2 * half):
                    d = half - 1 - p
                    if d != 0:
                        ref = jnp.where(pos == p, pltpu.roll(b, (-d) % c, 0), ref)
            else:
                ref = jnp.concatenate(
                    [jnp.broadcast_to(b[j + half - 1:j + half], (2 * half, b.shape[1]))
                     for j in range(0, c, 2 * half)], axis=0)
            decay = jnp.exp(-jnp.abs(b - ref))
        z = (jnp.where(is_query, q, k) * decay).astype(BF16)
        gram = lax.dot_general(z, z, NT_DIMS, preferred_element_type=F32)
        a = jnp.where(lvl == level, gram, a)

    vb = v.astype(BF16)
    o = jnp.dot(a.astype(BF16), vb, preferred_element_type=F32)
    o = o + lax.dot_general((q * jnp.exp(b)).astype(BF16), st.astype(BF16), NT_DIMS,
                            preferred_element_type=F32)
    b_last = b[c - 1:c]
    k_dec = (k * jnp.exp(b_last - b)).astype(BF16)
    st_new = jnp.exp(b_last) * st + lax.dot_general(vb, k_dec, TN_DIMS, preferred_element_type=F32)
    return o, st_new


def _hgrn_kernel(*refs, c, has_s0):
    if has_s0:
        q_ref, f_ref, i_ref, g_ref, lb_ref, ng_ref, s0_ref, o_ref, s_ref, st_ref = refs
    else:
        q_ref, f_ref, i_ref, g_ref, lb_ref, ng_ref, o_ref, s_ref, st_ref = refs
        s0_ref = None
    t = pl.program_id(1)

    @pl.when(t == 0)
    def _():
        for h in range(HGRN_HEADS):
            if has_s0:
                st_ref[h] = s0_ref[0, h].T
            else:
                st_ref[h] = jnp.zeros((HGRN_DV, HGRN_DK), F32)

    lvl = _split_level(c)

    def block(i, _):
        rows = pl.ds(pl.multiple_of(i * c, c), c)
        for h in range(HGRN_HEADS):
            cols = slice(h * HGRN_DK, (h + 1) * HGRN_DK)
            o, st_new = _hgrn_head_block(q_ref[rows, cols], f_ref[rows, cols], i_ref[rows, cols],
                                         lb_ref[:, cols], st_ref[h], lvl)
            st_ref[h] = st_new
            o = _rms_norm(o, ng_ref[:, cols]) * jax.nn.silu(g_ref[rows, cols])
            o_ref[rows, cols] = o.astype(BF16)
        return 0

    lax.fori_loop(0, q_ref.shape[0] // c, block, 0)

    @pl.when(t == pl.num_programs(1) - 1)
    def _():
        for h in range(HGRN_HEADS):
            s_ref[0, h] = st_ref[h].T


def _hgrn(hg, lb, norm_g, s0, batch, seq, c, rows):
    nt = seq // rows
    has_s0 = s0 is not None
    col = lambda n: (lambda b, t: (b * nt + t, n))
    const = lambda b, t: (0, 0)
    state = lambda b, t: (b, 0, 0, 0)
    in_specs = [pl.BlockSpec((rows, D_HGRN), col(n)) for n in range(4)]
    in_specs += [pl.BlockSpec((1, D_HGRN), const), pl.BlockSpec((1, D_HGRN), const)]
    args = [hg, hg, hg, hg, lb, norm_g]
    if has_s0:
        in_specs.append(pl.BlockSpec((1, HGRN_HEADS, HGRN_DK, HGRN_DV), state))
        args.append(s0)
    return pl.pallas_call(
        functools.partial(_hgrn_kernel, c=c, has_s0=has_s0),
        grid=(batch, nt),
        in_specs=in_specs,
        out_specs=[
            pl.BlockSpec((rows, D_HGRN), lambda b, t: (b * nt + t, 0)),
            pl.BlockSpec((1, HGRN_HEADS, HGRN_DK, HGRN_DV), state),
        ],
        out_shape=[
            jax.ShapeDtypeStruct((batch * seq, D_HGRN), BF16),
            jax.ShapeDtypeStruct((batch, HGRN_HEADS, HGRN_DK, HGRN_DV), F32),
        ],
        scratch_shapes=[pltpu.VMEM((HGRN_HEADS, HGRN_DV, HGRN_DK), F32)],
        compiler_params=_cparams(("arbitrary", "arbitrary")),
        name="hgrn",
    )(*args)


def _q_proj_kernel(cq_ref, wq_ref, wsw_ref, cos_ref, sin_ref, q_ref, *, scale):
    cq = cq_ref[...]
    q_lin = jnp.dot(cq, wq_ref[...], preferred_element_type=F32)
    q_sw = jnp.dot(cq, wsw_ref[...], preferred_element_type=F32)
    cos = cos_ref[...]
    sin = sin_ref[...]
    for h in range(MLA_HEADS):
        base = h * QK_PAD
        q_ref[:, base:base + QK_NOPE] = (q_lin[:, base:base + QK_NOPE] * scale).astype(BF16)
        rot = (q_lin[:, base + QK_NOPE:base + QK_PAD] * cos
               + q_sw[:, h * ROPE_PAD:(h + 1) * ROPE_PAD] * sin)
        q_ref[:, base + QK_NOPE:base + QK_PAD] = (rot * scale).astype(BF16)


def _q_proj(cq, w_q, w_q_sw, cos_k, sin_k, tm):
    m = cq.shape[0]
    n_pos_tiles = cos_k.shape[0] // tm
    scale = LOG2_E * (QK_NOPE + QK_ROPE) ** -0.5
    row = lambda i: (i, 0)
    const = lambda i: (0, 0)
    pos = lambda i: (i % n_pos_tiles, 0)
    return pl.pallas_call(
        functools.partial(_q_proj_kernel, scale=scale),
        grid=(m // tm,),
        in_specs=[
            pl.BlockSpec((tm, Q_LORA), row),
            pl.BlockSpec((Q_LORA, MLA_HEADS * QK_PAD), const),
            pl.BlockSpec((Q_LORA, MLA_HEADS * ROPE_PAD), const),
            pl.BlockSpec((tm, ROPE_PAD), pos),
            pl.BlockSpec((tm, ROPE_PAD), pos),
        ],
        out_specs=pl.BlockSpec((tm, MLA_HEADS * QK_PAD), row),
        out_shape=jax.ShapeDtypeStruct((m, MLA_HEADS * QK_PAD), BF16),
        compiler_params=_cparams(("arbitrary",)),
        name="q_proj",
    )(cq, w_q, w_q_sw, cos_k, sin_k)


def _kv_proj_kernel(lat_ref, kr_ref, wk_ref, wvt_ref, k_ref, vt_ref):
    lat = lat_ref[...].astype(BF16)
    kn = jnp.dot(lat, wk_ref[...], preferred_element_type=F32)
    vt_ref[0] = lax.dot_general(wvt_ref[...], lat, NT_DIMS, preferred_element_type=F32).astype(BF16)
    kr = kr_ref[...].astype(BF16)
    for h in range(MLA_HEADS):
        base = h * QK_PAD
        k_ref[:, base:base + QK_NOPE] = kn[:, h * QK_NOPE:(h + 1) * QK_NOPE].astype(BF16)
        k_ref[:, base + QK_NOPE:base + QK_PAD] = kr


def _kv_proj(lat, kr_pad, w_k, w_vt, batch, tm):
    m = lat.shape[0]
    tiles_per_batch = m // batch // tm
    row = lambda i: (i, 0)
    const = lambda i: (0, 0)
    return pl.pallas_call(
        _kv_proj_kernel,
        grid=(m // tm,),
        in_specs=[
            pl.BlockSpec((tm, KV_LORA), row),
            pl.BlockSpec((tm, ROPE_PAD), row),
            pl.BlockSpec((KV_LORA, MLA_HEADS * QK_NOPE), const),
            pl.BlockSpec((D_MLA, KV_LORA), const),
        ],
        out_specs=[
            pl.BlockSpec((tm, MLA_HEADS * QK_PAD), row),
            pl.BlockSpec((1, D_MLA, tm), lambda i: (i // tiles_per_batch, 0, i % tiles_per_batch)),
        ],
        out_shape=[
            jax.ShapeDtypeStruct((m, MLA_HEADS * QK_PAD), BF16),
            jax.ShapeDtypeStruct((batch, D_MLA, m // batch), BF16),
        ],
        compiler_params=_cparams(("arbitrary",)),
        name="kv_proj",
    )(lat, kr_pad, w_k, w_vt)


def _attn_kernel(tile_q_ref, tile_k_ref, q_ref, k_ref, vt_ref, o_ref,
                 vx_ref, rel_ref, sa_ref, sb_ref, m_ref, acc_ref, *, tq, tk, past, n_keys, n_plain, n_tiles):
    @pl.when(jnp.logical_and(pl.program_id(0) == 0, pl.program_id(1) == 0))
    def _():
        rel_ref[...] = (lax.broadcasted_iota(jnp.int32, (tk, tq), 0) // CHUNK
                        - lax.broadcasted_iota(jnp.int32, (tk, tq), 1) // CHUNK)

    vx_ref[:V_HEAD, :] = vt_ref[0]
    vx_ref[V_HEAD:, :] = jnp.ones((ONES_ROWS, vx_ref.shape[1]), BF16)
    m_ref[...] = jnp.full(m_ref.shape, NEG_BIG, F32)
    acc_ref[...] = jnp.zeros(acc_ref.shape, F32)
    k_row = lax.broadcasted_iota(jnp.int32, (tk, tq), 0)

    def origin(t):
        return pl.multiple_of(tile_q_ref[t] * tq, tq), pl.multiple_of(tile_k_ref[t] * tk, tk)

    def scores(t, s_ref):
        qs, ks = origin(t)
        s_ref[...] = lax.dot_general(k_ref[0, pl.ds(ks, tk), :], q_ref[0, pl.ds(qs, tq), :], NT_DIMS,
                                     preferred_element_type=F32)

    def softmax_pv(t, s_ref, masked):
        qs, ks = origin(t)
        s = s_ref[...]
        if masked:
            pad = jnp.where(k_row < n_keys - ks, 0, 1 << 20)
            visible = rel_ref[...] + pad <= (past + qs - ks) // CHUNK
            s = jnp.where(visible, s, NEG_BIG)
        cols = pl.ds(qs, tq)
        m = m_ref[:, cols]
        m_new = jnp.maximum(m, jnp.max(s, axis=0, keepdims=True))
        p = jnp.exp2(s - m_new).astype(BF16)
        pv = jnp.dot(vx_ref[:, pl.ds(ks, tk)], p, preferred_element_type=F32)
        acc_ref[:, cols] = jnp.exp2(m - m_new) * acc_ref[:, cols] + pv
        m_ref[:, cols] = m_new

    bufs = (sa_ref, sb_ref)

    def step(t, parity, masked):
        scores(t + 1, bufs[1 - parity])
        softmax_pv(t, bufs[parity], masked)

    def run(t0, t1, masked):
        t = t0
        if t < t1 and t % 2 == 1:
            step(t, 1, masked)
            t += 1
        trips = (t1 - t) // ATTN_UNROLL
        if trips > 0:
            def body(i, _, base=t):
                tt = base + ATTN_UNROLL * i
                for u in range(ATTN_UNROLL):
                    step(tt + u, u % 2, masked)
                return 0
            lax.fori_loop(0, trips, body, 0)
            t += ATTN_UNROLL * trips
        while t < t1:
            step(t, t % 2, masked)
            t += 1

    scores(0, sa_ref)
    run(0, min(n_plain, n_tiles - 1), False)
    run(min(n_plain, n_tiles - 1), n_tiles - 1, True)
    last = n_tiles - 1
    softmax_pv(last, bufs[last % 2], last >= n_plain)
    acc = acc_ref[...]
    o_ref[0] = (acc[:V_HEAD] / acc[V_HEAD:V_HEAD + 1]).T.astype(BF16)


def _attention_tiles(seq, n_keys_pad, n_keys, past, tq, tk):
    plain, masked = [], []
    for j in range(n_keys_pad // tk):
        for i in range(seq // tq):
            q_lo, k_lo = past + i * tq, j * tk
            q_hi, k_hi = q_lo + tq - 1, k_lo + tk - 1
            if k_lo // CHUNK > q_hi // CHUNK or k_lo >= n_keys:
                continue
            if k_hi // CHUNK <= q_lo // CHUNK and k_hi < n_keys:
                plain.append((i, j))
            else:
                masked.append((i, j))
    return plain, masked


def _attention(q, k, vt, batch, seq, n_keys_pad, n_keys, past, tq, tk):
    assert past % CHUNK == 0 and tk % CHUNK == 0 and (tq % CHUNK == 0 or tq == seq)
    plain, masked = _attention_tiles(seq, n_keys_pad, n_keys, past, tq, tk)
    tiles = plain + masked
    tile_q = jnp.asarray([t[0] for t in tiles], jnp.int32)
    tile_k = jnp.asarray([t[1] for t in tiles], jnp.int32)
    smem = pl.BlockSpec(memory_space=pltpu.SMEM)
    return pl.pallas_call(
        functools.partial(_attn_kernel, tq=tq, tk=tk, past=past, n_keys=n_keys,
                          n_plain=len(plain), n_tiles=len(tiles)),
        grid=(batch, MLA_HEADS),
        in_specs=[
            smem, smem,
            pl.BlockSpec((1, seq, QK_PAD), lambda b, h: (b, 0, h)),
            pl.BlockSpec((1, n_keys_pad, QK_PAD), lambda b, h: (b, 0, h)),
            pl.BlockSpec((1, V_HEAD, n_keys_pad), lambda b, h: (b, h, 0)),
        ],
        out_specs=pl.BlockSpec((1, seq, V_HEAD), lambda b, h: (b, 0, h)),
        out_shape=jax.ShapeDtypeStruct((batch, seq, D_MLA), BF16),
        scratch_shapes=[
            pltpu.VMEM((V_HEAD + ONES_ROWS, n_keys_pad), BF16),
            pltpu.VMEM((tk, tq), jnp.int32),
            pltpu.VMEM((tk, tq), F32),
            pltpu.VMEM((tk, tq), F32),
            pltpu.VMEM((1, seq), F32),
            pltpu.VMEM((V_HEAD + ONES_ROWS, seq), F32),
        ],
        compiler_params=_cparams(("arbitrary", "arbitrary")),
        name="attn",
    )(tile_q, tile_k, q, k, vt)


LAT_KEY = KV_LORA + ROPE_PAD


def _absorbed_attn_kernel(q_ref, latp_ref, krp_ref, latn_ref, krn_ref, wuk_ref, wuv_ref, o_ref,
                          qx_ref, kx_ref, vx_ref, *, seq, past, n_keys):
    for h in range(MLA_HEADS):
        rows = slice(h * seq, (h + 1) * seq)
        q_lat = jnp.dot(q_ref[:, h * QK_PAD:h * QK_PAD + QK_NOPE], wuk_ref[h], preferred_element_type=F32)
        qx_ref[rows, :KV_LORA] = q_lat.astype(BF16)
        qx_ref[rows, KV_LORA:] = q_ref[:, h * QK_PAD + QK_NOPE:(h + 1) * QK_PAD]
    kx_ref[:past, :KV_LORA] = latp_ref[0].astype(BF16)
    kx_ref[:past, KV_LORA:] = krp_ref[0].astype(BF16)
    kx_ref[past:n_keys, :KV_LORA] = latn_ref[...].astype(BF16)
    kx_ref[past:n_keys, KV_LORA:] = krn_ref[...].astype(BF16)
    kx_ref[n_keys:, :] = jnp.zeros((kx_ref.shape[0] - n_keys, LAT_KEY), BF16)
    kx = kx_ref[...]
    s = lax.dot_general(qx_ref[...], kx, NT_DIMS, preferred_element_type=F32)
    q_chunk = (past + lax.broadcasted_iota(jnp.int32, (s.shape[0], 1), 0) % seq) // CHUNK
    k_idx = lax.broadcasted_iota(jnp.int32, (1, s.shape[1]), 1)
    visible = jnp.logical_and(k_idx // CHUNK <= q_chunk, k_idx < n_keys)
    s = jnp.where(visible, s, NEG_BIG)
    p = jnp.exp2(s - jnp.max(s, axis=-1, keepdims=True)).astype(BF16)
    vx_ref[:, :KV_LORA] = kx[:, :KV_LORA]
    vx_ref[:, KV_LORA:] = jnp.ones((vx_ref.shape[0], ROPE_PAD), BF16)
    o_ext = jnp.dot(p, vx_ref[...], preferred_element_type=F32)
    denom = o_ext[:, KV_LORA:]
    o_lat = (o_ext[:, :KV_LORA] / jnp.concatenate([denom] * (KV_LORA // ROPE_PAD), axis=1)).astype(BF16)
    for h in range(MLA_HEADS):
        o_ref[:, h * V_HEAD:(h + 1) * V_HEAD] = jnp.dot(
            o_lat[h * seq:(h + 1) * seq], wuv_ref[h], preferred_element_type=F32).astype(BF16)


def _absorbed_attention(q, lat_past, kr_past, lat_new, kr_new, w_uk_t, w_uv, batch, seq, n_keys_pad):
    past = lat_past.shape[1]
    n_keys = past + seq
    const3 = lambda b: (0, 0, 0)
    return pl.pallas_call(
        functools.partial(_absorbed_attn_kernel, seq=seq, past=past, n_keys=n_keys),
        grid=(batch,),
        in_specs=[
            pl.BlockSpec((seq, MLA_HEADS * QK_PAD), lambda b: (b, 0)),
            pl.BlockSpec((1, past, KV_LORA), lambda b: (b, 0, 0)),
            pl.BlockSpec((1, past, ROPE_PAD), lambda b: (b, 0, 0)),
            pl.BlockSpec((seq, KV_LORA), lambda b: (b, 0)),
            pl.BlockSpec((seq, ROPE_PAD), lambda b: (b, 0)),
            pl.BlockSpec((MLA_HEADS, QK_NOPE, KV_LORA), const3),
            pl.BlockSpec((MLA_HEADS, KV_LORA, V_HEAD), const3),
        ],
        out_specs=pl.BlockSpec((seq, D_MLA), lambda b: (b, 0)),
        out_shape=jax.ShapeDtypeStruct((batch * seq, D_MLA), BF16),
        scratch_shapes=[
            pltpu.VMEM((MLA_HEADS * seq, LAT_KEY), BF16),
            pltpu.VMEM((n_keys_pad, LAT_KEY), BF16),
            pltpu.VMEM((n_keys_pad, LAT_KEY), BF16),
        ],
        compiler_params=_cparams(("arbitrary",)),
        name="absorbed_attn",
    )(q, lat_past, kr_past, lat_new, kr_new, w_uk_t, w_uv)


def _out_proj_kernel(oh_ref, oa_ref, h_ref, w_ref, g1_ref, b1_ref, y_ref, yb_ref):
    for r0 in range(0, h_ref.shape[0], ROW_SUB):
        rows = slice(r0, r0 + ROW_SUB)
        mix = jnp.dot(oh_ref[rows, :], w_ref[:D_HGRN, :], preferred_element_type=F32)
        mix = mix + jnp.dot(oa_ref[rows, :], w_ref[D_HGRN:, :], preferred_element_type=F32)
        y = _layer_norm(ALPHA * h_ref[rows, :] + mix, g1_ref[...], b1_ref[...])
        y_ref[rows, :] = y
        yb_ref[rows, :] = y.astype(BF16)


def _out_proj(oh, oa, h, w_out, ln1_g, ln1_b, tm):
    m = h.shape[0]
    row = lambda i: (i, 0)
    const = lambda i: (0, 0)
    vec = pl.BlockSpec((1, D_MODEL), const)
    return pl.pallas_call(
        _out_proj_kernel,
        grid=(m // tm,),
        in_specs=[
            pl.BlockSpec((tm, D_HGRN), row),
            pl.BlockSpec((tm, D_MLA), row),
            pl.BlockSpec((tm, D_MODEL), row),
            pl.BlockSpec((D_HGRN + D_MLA, D_MODEL), const),
            vec, vec,
        ],
        out_specs=[pl.BlockSpec((tm, D_MODEL), row), pl.BlockSpec((tm, D_MODEL), row)],
        out_shape=[jax.ShapeDtypeStruct((m, D_MODEL), F32), jax.ShapeDtypeStruct((m, D_MODEL), BF16)],
        compiler_params=_cparams(("arbitrary",)),
        name="out_proj",
    )(oh, oa, h, w_out, ln1_g, ln1_b)


def _ffn_up_kernel(x_ref, wu_ref, wg_ref, cw_ref, cb_ref, buf_ref, h_ref, cnew_ref, carry_ref, *,
                   spt, tiles_per_seq):
    i = pl.program_id(1)
    tm = x_ref.shape[0]
    tf = wu_ref.shape[1]
    rows = tm // spt

    if tiles_per_seq > 1:
        @pl.when(i % tiles_per_seq == 0)
        def _():
            carry_ref[...] = buf_ref[...]
        halo_ref = carry_ref
    else:
        halo_ref = buf_ref

    xb = x_ref[...]
    for c0 in range(0, tf, FFN_SUB):
        width = min(FFN_SUB, tf - c0)
        cs = slice(c0, c0 + width)
        r = lax.broadcasted_iota(jnp.int32, (spt, rows, width), 1)
        if 2 * width <= FFN_SUB:
            both = jnp.dot(xb, jnp.concatenate([wu_ref[:, cs], wg_ref[:, cs]], axis=1),
                           preferred_element_type=F32)
            u, gate = both[:, :width], both[:, width:]
        else:
            u = jnp.dot(xb, wu_ref[:, cs], preferred_element_type=F32)
            gate = jnp.dot(xb, wg_ref[:, cs], preferred_element_type=F32)
        u = u.reshape(spt, rows, width)
        gate = gate.reshape(spt, rows, width)
        halo = halo_ref[:, :, cs]
        prev1 = jnp.where(r == 0, halo[:, 1:2, :], pltpu.roll(u, 1, 1))
        prev2 = jnp.where(r == 0, halo[:, 0:1, :],
                          jnp.where(r == 1, halo[:, 1:2, :], pltpu.roll(u, 2, 1)))
        cw = cw_ref[:, cs]
        a = cb_ref[:, cs] + (prev2 * cw[0:1] + prev1 * cw[1:2] + u * cw[2:3])
        h_ref[:, cs] = (jax.nn.silu(a) * gate).astype(BF16).reshape(tm, width)
        tail = u[:, rows - (CONV_W - 1):, :]
        cnew_ref[:, :, cs] = tail
        if tiles_per_seq > 1:
            carry_ref[:, :, cs] = tail


def _ffn_up(x, w_up, w_gate, conv_w, conv_b, conv_buf, seq, tm, tf):
    m = x.shape[0]
    n_seq = m // seq
    if tm >= seq:
        spt, tiles_per_seq = tm // seq, 1
        seq_of = lambda i: i
    else:
        spt, tiles_per_seq = 1, seq // tm
        seq_of = lambda i: i // tiles_per_seq
    n_f = D_FF // tf
    hid, tails = pl.pallas_call(
        functools.partial(_ffn_up_kernel, spt=spt, tiles_per_seq=tiles_per_seq),
        grid=(n_f, m // tm),
        in_specs=[
            pl.BlockSpec((tm, D_MODEL), lambda f, i: (i, 0)),
            pl.BlockSpec((D_MODEL, tf), lambda f, i: (0, f)),
            pl.BlockSpec((D_MODEL, tf), lambda f, i: (0, f)),
            pl.BlockSpec((CONV_W, tf), lambda f, i: (0, f)),
            pl.BlockSpec((1, tf), lambda f, i: (0, f)),
            pl.BlockSpec((spt, CONV_W - 1, tf), lambda f, i: (seq_of(i), 0, f)),
        ],
        out_specs=[
            pl.BlockSpec((tm, tf), lambda f, i: (i, f)),
            pl.BlockSpec((spt, CONV_W - 1, tf), lambda f, i: (i, 0, f)),
        ],
        out_shape=[
            jax.ShapeDtypeStruct((m, D_FF), BF16),
            jax.ShapeDtypeStruct((n_seq * tiles_per_seq, CONV_W - 1, D_FF), F32),
        ],
        scratch_shapes=[pltpu.VMEM((spt, CONV_W - 1, tf), F32)],
        compiler_params=_cparams(("arbitrary", "arbitrary")),
        name="ffn_up",
    )(x, w_up, w_gate, conv_w, conv_b, conv_buf)
    return hid, tails[tiles_per_seq - 1::tiles_per_seq]


def _ffn_down_kernel(h_ref, x_ref, w_ref, g_ref, b_ref, y_ref):
    for r0 in range(0, x_ref.shape[0], ROW_SUB):
        rows = slice(r0, r0 + ROW_SUB)
        ff = jnp.dot(h_ref[rows, :], w_ref[...], preferred_element_type=F32)
        y_ref[rows, :] = _layer_norm(ALPHA * x_ref[rows, :] + ff, g_ref[...], b_ref[...])


def _ffn_down(hid, x, w_down, ln2_g, ln2_b, tm):
    m = x.shape[0]
    row = lambda i: (i, 0)
    const = lambda i: (0, 0)
    return pl.pallas_call(
        _ffn_down_kernel,
        grid=(m // tm,),
        in_specs=[
            pl.BlockSpec((tm, D_FF), row),
            pl.BlockSpec((tm, D_MODEL), row),
            pl.BlockSpec((D_FF, D_MODEL), const, pipeline_mode=pl.Buffered(1)),
            pl.BlockSpec((1, D_MODEL), const),
            pl.BlockSpec((1, D_MODEL), const),
        ],
        out_specs=pl.BlockSpec((tm, D_MODEL), row),
        out_shape=jax.ShapeDtypeStruct((m, D_MODEL), F32),
        compiler_params=_cparams(("arbitrary",)),
        name="ffn_down",
    )(hid, x, w_down, ln2_g, ln2_b)


def _rope_tables(past, seq, rows):
    inv = ROPE_THETA ** (-np.arange(0, QK_ROPE, 2, dtype=np.float64) / QK_ROPE)
    ang = (past + np.arange(seq, dtype=np.float64))[:, None] * inv[None, :]
    cos, sin = np.cos(ang), np.sin(ang)
    zero = np.zeros((seq, ROPE_PAD - QK_ROPE))
    cos_t = np.concatenate([cos, cos, zero], axis=1).astype(np.float32)
    sin_t = np.concatenate([-sin, sin, zero], axis=1).astype(np.float32)
    reps = rows // seq
    return jnp.asarray(np.tile(cos_t, (reps, 1))), jnp.asarray(np.tile(sin_t, (reps, 1)))


def _swap_halves(w):
    half = QK_ROPE // 2
    return jnp.concatenate([w[..., half:], w[..., :half]], axis=-1)


def _prep_weights(w_in, w_q_b, w_kv_b, w_out, w_ffn_up, w_ffn_gate, w_ffn_down):
    w_in = w_in[0]
    c0 = 4 * D_HGRN
    w_h = w_in.astype(BF16)
    kr_cols = w_in[:, c0 + Q_LORA + KV_LORA:]
    zpad = jnp.zeros((D_MODEL, ROPE_PAD - QK_ROPE), F32)
    w_m = jnp.concatenate([kr_cols, zpad, _swap_halves(kr_cols), zpad], axis=1).astype(BF16)

    wq = w_q_b[0].reshape(Q_LORA, MLA_HEADS, QK_NOPE + QK_ROPE)
    zq = jnp.zeros((Q_LORA, MLA_HEADS, QK_PAD - QK_NOPE - QK_ROPE), F32)
    w_q = jnp.concatenate([wq, zq], axis=-1).reshape(Q_LORA, MLA_HEADS * QK_PAD).astype(BF16)
    zr = jnp.zeros((Q_LORA, MLA_HEADS, ROPE_PAD - QK_ROPE), F32)
    w_q_sw = jnp.concatenate([_swap_halves(wq[..., QK_NOPE:]), zr], axis=-1)
    w_q_sw = w_q_sw.reshape(Q_LORA, MLA_HEADS * ROPE_PAD).astype(BF16)

    wkv = w_kv_b[0].reshape(KV_LORA, MLA_HEADS, QK_NOPE + V_HEAD)
    w_k = wkv[..., :QK_NOPE].reshape(KV_LORA, MLA_HEADS * QK_NOPE).astype(BF16)
    w_vt = wkv[..., QK_NOPE:].reshape(KV_LORA, D_MLA).T.astype(BF16)
    w_uk_t = wkv[..., :QK_NOPE].transpose(1, 2, 0).astype(BF16)
    w_uv = wkv[..., QK_NOPE:].transpose(1, 0, 2).astype(BF16)
    return dict(w_h=w_h, w_m=w_m, w_q=w_q, w_q_sw=w_q_sw, w_k=w_k, w_vt=w_vt, w_uk_t=w_uk_t, w_uv=w_uv,
                w_out=w_out[0].astype(BF16), w_up=w_ffn_up[0].astype(BF16),
                w_gate=w_ffn_gate[0].astype(BF16), w_down=w_ffn_down[0].astype(BF16))


def _row(a):
    return a.reshape(1, -1)


def _tile_plan(batch, seq, n_keys_pad):
    m = batch * seq
    tm = min(ROW_TILE, m)
    tm_q = min(QKV_ROW_TILE, m)
    tm_ffn = min(FFN_ROW_TILE, m)
    assert m % tm == 0 and m % tm_q == 0 and m % tm_ffn == 0 and (tm_ffn % seq == 0 or seq % tm_ffn == 0)
    return dict(
        tm=tm,
        tm_q=tm_q,
        tm_ffn=tm_ffn,
        rows_hgrn=min(HGRN_STEP_ROWS, seq),
        tq=min(ATTN_TILE, seq),
        tk=ATTN_TILE if n_keys_pad % ATTN_TILE == 0 else n_keys_pad,
        tm_kv=QKV_ROW_TILE if n_keys_pad % QKV_ROW_TILE == 0 else n_keys_pad,
    )


def _encoder(x, s0, lat_past, kr_past, conv_buf, p, w, *, blk):
    batch, seq, _ = x.shape
    past = 0 if lat_past is None else lat_past.shape[1]
    m = batch * seq
    n_keys = past + seq
    key_pad = -n_keys % KEY_ALIGN
    n_keys_pad = n_keys + key_pad
    t = _tile_plan(batch, seq, n_keys_pad)
    x2 = x.reshape(m, D_MODEL)
    cos_k, sin_k = _rope_tables(past, seq, max(seq, t["tm_q"]))

    hg, cqn, lat_new, kr_new, h = _in_proj(x2, p["ln_in_g"], p["ln_in_b"], w["w_h"], w["w_m"],
                                           p["q_a_g"], p["kv_a_g"], cos_k, sin_k, t["tm"])
    o_h, s_new = _hgrn(hg, p["lb"], p["hgrn_norm_g"], s0, batch, seq, blk, t["rows_hgrn"])

    q = _q_proj(cqn, w["w_q"], w["w_q_sw"], cos_k, sin_k, t["tm_q"])
    lat3 = lat_new.reshape(batch, seq, KV_LORA)
    kr3 = kr_new.reshape(batch, seq, ROPE_PAD)
    if past and seq * MLA_HEADS <= ABSORB_MAX_ROWS:
        kr_past_pad = jnp.pad(kr_past, ((0, 0), (0, 0), (0, ROPE_PAD - QK_ROPE)))
        o_a = _absorbed_attention(q, lat_past, kr_past_pad, lat_new, kr_new, w["w_uk_t"], w["w_uv"],
                                  batch, seq, n_keys_pad)
    else:
        parts_l, parts_r = [lat3], [kr3]
        if past:
            parts_l.insert(0, lat_past)
            parts_r.insert(0, jnp.pad(kr_past, ((0, 0), (0, 0), (0, ROPE_PAD - QK_ROPE))))
        if key_pad:
            parts_l.append(jnp.zeros((batch, key_pad, KV_LORA), F32))
            parts_r.append(jnp.zeros((batch, key_pad, ROPE_PAD), F32))
        lat_all = jnp.concatenate(parts_l, axis=1) if len(parts_l) > 1 else lat3
        kr_all = jnp.concatenate(parts_r, axis=1) if len(parts_r) > 1 else kr3
        k_cat, vt = _kv_proj(lat_all.reshape(batch * n_keys_pad, KV_LORA),
                             kr_all.reshape(batch * n_keys_pad, ROPE_PAD), w["w_k"], w["w_vt"], batch, t["tm_kv"])
        o_a = _attention(q.reshape(batch, seq, MLA_HEADS * QK_PAD),
                         k_cat.reshape(batch, n_keys_pad, MLA_HEADS * QK_PAD), vt,
                         batch, seq, n_keys_pad, n_keys, past, t["tq"], t["tk"]).reshape(m, D_MLA)

    h1, h1_bf16 = _out_proj(o_h, o_a, h, w["w_out"], p["ln1_g"], p["ln1_b"], t["tm"])
    hid, conv_new = _ffn_up(h1_bf16, w["w_up"], w["w_gate"], p["conv_w"], p["conv_b"], conv_buf, seq, t["tm_ffn"],
                            FFN_COL_TILE)
    y = _ffn_down(hid, h1, w["w_down"], p["ln2_g"], p["ln2_b"], t["tm"])
    return (y.reshape(batch, seq, D_MODEL), s_new[None], lat3[None],
            kr3[:, :, :QK_ROPE][None], conv_new[None])


def kernel(x_prompt, x_sample, cache_kv_latent, cache_k_rope, state_hgrn, cache_ffn_conv, lb_param, ln_in_g, ln_in_b, w_in, hgrn_norm_g, q_a_g, w_q_b, kv_a_g, w_kv_b, w_out, ln1_g, ln1_b, w_ffn_up, w_ffn_gate, conv_w, conv_b, w_ffn_down, ln2_g, ln2_b):
    w = _prep_weights(w_in, w_q_b, w_kv_b, w_out, w_ffn_up, w_ffn_gate, w_ffn_down)
    lbs = jnp.cumsum(jax.nn.softmax(lb_param.astype(F32), axis=0), axis=0)
    p = dict(ln_in_g=_row(ln_in_g), ln_in_b=_row(ln_in_b), lb=_row(lbs[0]),
             hgrn_norm_g=_row(hgrn_norm_g[0]), q_a_g=_row(q_a_g[0]), kv_a_g=_row(kv_a_g[0]),
             ln1_g=_row(ln1_g[0]), ln1_b=_row(ln1_b[0]), conv_w=conv_w[0], conv_b=_row(conv_b[0]),
             ln2_g=_row(ln2_g[0]), ln2_b=_row(ln2_b[0]))

    n_prompt = x_prompt.shape[0]
    conv0 = jnp.zeros((n_prompt, CONV_W - 1, D_FF), F32)
    y_p, s_p, lat_p, kr_p, conv_p = _encoder(x_prompt, None, None, None, conv0, p, w, blk=CHUNK)
    y_s, s_s, lat_s, kr_s, conv_s = _encoder(
        x_sample, state_hgrn[0], cache_kv_latent[0], cache_k_rope[0], cache_ffn_conv[0], p, w,
        blk=x_sample.shape[1])
    return (y_p, y_s, lat_p, kr_p, s_p, conv_p, lat_s, kr_s, s_s, conv_s)
```

```python
import functools

import numpy as np
import jax
import jax.numpy as jnp
from jax import lax
from jax.experimental import pallas as pl
from jax.experimental.pallas import tpu as pltpu

F32 = jnp.float32
BF16 = jnp.bfloat16

D_MODEL = 2048
CHUNK = 64
HGRN_HEADS = 8
HGRN_DK = 128
HGRN_DV = 128
D_HGRN = HGRN_HEADS * HGRN_DK
MLA_HEADS = 8
Q_LORA = 512
KV_LORA = 256
QK_NOPE = 128
QK_ROPE = 64
V_HEAD = 128
D_MLA = MLA_HEADS * V_HEAD
D_FF = 5632
CONV_W = 3
ROPE_THETA = 10000.0
LN_EPS = 1e-5
RMS_EPS = 1e-6
DEPTH = 1
ALPHA = (2.0 * DEPTH) ** 0.25

V7X_LANES = 128
V7X_MXU_WIDTH = 256
V7X_BF16_SUBLANES = 16
V7X_VMEM_LIMIT = 56 * 1024 * 1024

QK_PAD = V7X_MXU_WIDTH
ROPE_PAD = V7X_LANES
KEY_ALIGN = V7X_LANES
ONES_ROWS = V7X_BF16_SUBLANES
FFN_SUB = V7X_MXU_WIDTH

ROW_TILE = 512
QKV_ROW_TILE = 1024
FFN_ROW_TILE = 1024
HGRN_STEP_ROWS = 512
ABSORB_MAX_ROWS = 512
ATTN_TILE = 512
FFN_COL_TILE = 1408
ROW_SUB = 256
ATTN_UNROLL = 4
NEG_BIG = -1e30
LOG2_E = 1.4426950408889634

NT_DIMS = (((1,), (1,)), ((), ()))
TN_DIMS = (((0,), (0,)), ((), ()))


def _cparams(sem):
    return pltpu.CompilerParams(dimension_semantics=sem, vmem_limit_bytes=V7X_VMEM_LIMIT)


def _layer_norm(x, g, b):
    mu = jnp.mean(x, axis=-1, keepdims=True)
    xc = x - mu
    var = jnp.mean(xc * xc, axis=-1, keepdims=True)
    return xc * lax.rsqrt(var + LN_EPS) * g + b


def _rms_norm(x, g):
    return x * lax.rsqrt(jnp.mean(x * x, axis=-1, keepdims=True) + RMS_EPS) * g


HG_TILE = 2 * D_HGRN
N_HG_TILES = 4 * D_HGRN // HG_TILE
W_MLA_COLS = 2 * ROPE_PAD


def _in_proj_kernel(x_ref, g_ref, b_ref, wh_ref, wm_ref, qg_ref, kvg_ref, cos_ref, sin_ref,
                    hg_ref, cq_ref, lat_ref, kr_ref, h_ref, hb_ref):
    j = pl.program_id(1)

    @pl.when(j == 0)
    def _():
        for r0 in range(0, x_ref.shape[0], ROW_SUB):
            rows = slice(r0, r0 + ROW_SUB)
            h = _layer_norm(x_ref[rows, :], g_ref[...], b_ref[...])
            h_ref[rows, :] = h
            hb = h.astype(BF16)
            hb_ref[rows, :] = hb
            hg_ref[rows, :] = jnp.dot(hb, wh_ref[:, :HG_TILE], preferred_element_type=F32)

    @pl.when(jnp.logical_and(j > 0, j < N_HG_TILES))
    def _():
        cols = pl.ds(pl.multiple_of(j * HG_TILE, HG_TILE), HG_TILE)
        hg_ref[...] = jnp.dot(hb_ref[...], wh_ref[:, cols], preferred_element_type=F32)

    @pl.when(j == N_HG_TILES)
    def _():
        hb = hb_ref[...]
        lat_cols = pl.ds(N_HG_TILES * HG_TILE, Q_LORA + KV_LORA)
        p = jnp.dot(hb, wh_ref[:, lat_cols], preferred_element_type=F32)
        r = jnp.dot(hb, wm_ref[...], preferred_element_type=F32)
        cq = p[:, :Q_LORA]
        ckv = p[:, Q_LORA:]
        kr = r[:, :ROPE_PAD]
        kr_sw = r[:, ROPE_PAD:]
        cq_ref[...] = _rms_norm(cq, qg_ref[...]).astype(BF16)
        lat_ref[...] = _rms_norm(ckv, kvg_ref[...])
        kr_ref[...] = kr * cos_ref[...] + kr_sw * sin_ref[...]


def _in_proj(x, ln_g, ln_b, w_h, w_m, q_a_g, kv_a_g, cos_k, sin_k, tm):
    m = x.shape[0]
    n_pos_tiles = cos_k.shape[0] // tm
    grid = (m // tm, N_HG_TILES + 1)
    row = lambda i, j: (i, 0)
    const = lambda i, j: (0, 0)
    pos = lambda i, j: (i % n_pos_tiles, 0)
    hg_col = lambda i, j: (i, jnp.minimum(j, N_HG_TILES - 1))
    return pl.pallas_call(
        _in_proj_kernel,
        grid=grid,
        in_specs=[
            pl.BlockSpec((tm, D_MODEL), row),
            pl.BlockSpec((1, D_MODEL), const),
            pl.BlockSpec((1, D_MODEL), const),
            pl.BlockSpec(w_h.shape, const, pipeline_mode=pl.Buffered(1)),
            pl.BlockSpec((D_MODEL, W_MLA_COLS), const, pipeline_mode=pl.Buffered(1)),
            pl.BlockSpec((1, Q_LORA), const),
            pl.BlockSpec((1, KV_LORA), const),
            pl.BlockSpec((tm, ROPE_PAD), pos),
            pl.BlockSpec((tm, ROPE_PAD), pos),
        ],
        out_specs=[
            pl.BlockSpec((tm, HG_TILE), hg_col),
            pl.BlockSpec((tm, Q_LORA), row),
            pl.BlockSpec((tm, KV_LORA), row),
            pl.BlockSpec((tm, ROPE_PAD), row),
            pl.BlockSpec((tm, D_MODEL), row),
        ],
        out_shape=[
            jax.ShapeDtypeStruct((m, N_HG_TILES * HG_TILE), F32),
            jax.ShapeDtypeStruct((m, Q_LORA), BF16),
            jax.ShapeDtypeStruct((m, KV_LORA), F32),
            jax.ShapeDtypeStruct((m, ROPE_PAD), F32),
            jax.ShapeDtypeStruct((m, D_MODEL), F32),
        ],
        scratch_shapes=[pltpu.VMEM((tm, D_MODEL), BF16)],
        compiler_params=_cparams(("arbitrary", "arbitrary")),
        name="in_proj",
    )(x, ln_g, ln_b, w_h, w_m, q_a_g, kv_a_g, cos_k, sin_k)


def _cumsum_rows(x):
    c = x.shape[0]
    row = lax.broadcasted_iota(jnp.int32, (c, 1), 0)
    shift = 1
    while shift < c:
        x = x + jnp.where(row >= shift, pltpu.roll(x, shift, 0), 0.0)
        shift *= 2
    return x


def _split_level(c):
    t = lax.broadcasted_iota(jnp.int32, (c, c), 0)
    s = lax.broadcasted_iota(jnp.int32, (c, c), 1)
    x = jnp.bitwise_xor(t, s)
    lvl = jnp.full((c, c), -1, jnp.int32)
    for bit in range(c.bit_length() - 1):
        lvl = jnp.where(x >= (1 << bit), bit, lvl)
    return jnp.where(t < s, -2, lvl)


def _hgrn_head_block(q, fpre, v, lb, st, lvl):
    c = q.shape[0]
    f = lb + (1.0 - lb) * jax.nn.sigmoid(fpre)
    g = jnp.log(f)
    k = 1.0 - f
    b = _cumsum_rows(g)
    row = lax.broadcasted_iota(jnp.int32, (c, 1), 0)

    a = jnp.where(lvl == -1, jnp.sum(q * k, axis=-1, keepdims=True), 0.0)
    for level in range(c.bit_length() - 1):
        half = 1 << level
        pos = row % (2 * half)
        is_query = pos >= half
        if level == 0:
            decay = jnp.where(is_query, f, 1.0)
        else:
            if 2 * half < 8:
                ref = b
                for p in range(2 * half):
                    d = half - 1 - p
                    if d != 0:
                        ref = jnp.where(pos == p, pltpu.roll(b, (-d) % c, 0), ref)
            else:
                ref = jnp.concatenate(
                    [jnp.broadcast_to(b[j + half - 1:j + half], (2 * half, b.shape[1]))
                     for j in range(0, c, 2 * half)], axis=0)
            decay = jnp.exp(-jnp.abs(b - ref))
        z = (jnp.where(is_query, q, k) * decay).astype(BF16)
        gram = lax.dot_general(z, z, NT_DIMS, preferred_element_type=F32)
        a = jnp.where(lvl == level, gram, a)

    vb = v.astype(BF16)
    o = jnp.dot(a.astype(BF16), vb, preferred_element_type=F32)
    o = o + lax.dot_general((q * jnp.exp(b)).astype(BF16), st.astype(BF16), NT_DIMS,
                            preferred_element_type=F32)
    b_last = b[c - 1:c]
    k_dec = (k * jnp.exp(b_last - b)).astype(BF16)
    st_new = jnp.exp(b_last) * st + lax.dot_general(vb, k_dec, TN_DIMS, preferred_element_type=F32)
    return o, st_new


def _hgrn_kernel(*refs, c, has_s0):
    if has_s0:
        q_ref, f_ref, i_ref, g_ref, lb_ref, ng_ref, s0_ref, o_ref, s_ref, st_ref = refs
    else:
        q_ref, f_ref, i_ref, g_ref, lb_ref, ng_ref, o_ref, s_ref, st_ref = refs
        s0_ref = None
    t = pl.program_id(1)

    @pl.when(t == 0)
    def _():
        for h in range(HGRN_HEADS):
            if has_s0:
                st_ref[h] = s0_ref[0, h].T
            else:
                st_ref[h] = jnp.zeros((HGRN_DV, HGRN_DK), F32)

    lvl = _split_level(c)

    def block(i, _):
        rows = pl.ds(pl.multiple_of(i * c, c), c)
        for h in range(HGRN_HEADS):
            cols = slice(h * HGRN_DK, (h + 1) * HGRN_DK)
            o, st_new = _hgrn_head_block(q_ref[rows, cols], f_ref[rows, cols], i_ref[rows, cols],
                                         lb_ref[:, cols], st_ref[h], lvl)
            st_ref[h] = st_new
            o = _rms_norm(o, ng_ref[:, cols]) * jax.nn.silu(g_ref[rows, cols])
            o_ref[rows, cols] = o.astype(BF16)
        return 0

    lax.fori_loop(0, q_ref.shape[0] // c, block, 0)

    @pl.when(t == pl.num_programs(1) - 1)
    def _():
        for h in range(HGRN_HEADS):
            s_ref[0, h] = st_ref[h].T


def _hgrn(hg, lb, norm_g, s0, batch, seq, c, rows):
    nt = seq // rows
    has_s0 = s0 is not None
    col = lambda n: (lambda b, t: (b * nt + t, n))
    const = lambda b, t: (0, 0)
    state = lambda b, t: (b, 0, 0, 0)
    in_specs = [pl.BlockSpec((rows, D_HGRN), col(n)) for n in range(4)]
    in_specs += [pl.BlockSpec((1, D_HGRN), const), pl.BlockSpec((1, D_HGRN), const)]
    args = [hg, hg, hg, hg, lb, norm_g]
    if has_s0:
        in_specs.append(pl.BlockSpec((1, HGRN_HEADS, HGRN_DK, HGRN_DV), state))
        args.append(s0)
    return pl.pallas_call(
        functools.partial(_hgrn_kernel, c=c, has_s0=has_s0),
        grid=(batch, nt),
        in_specs=in_specs,
        out_specs=[
            pl.BlockSpec((rows, D_HGRN), lambda b, t: (b * nt + t, 0)),
            pl.BlockSpec((1, HGRN_HEADS, HGRN_DK, HGRN_DV), state),
        ],
        out_shape=[
            jax.ShapeDtypeStruct((batch * seq, D_HGRN), BF16),
            jax.ShapeDtypeStruct((batch, HGRN_HEADS, HGRN_DK, HGRN_DV), F32),
        ],
        scratch_shapes=[pltpu.VMEM((HGRN_HEADS, HGRN_DV, HGRN_DK), F32)],
        compiler_params=_cparams(("arbitrary", "arbitrary")),
        name="hgrn",
    )(*args)


def _q_proj_kernel(cq_ref, wq_ref, wsw_ref, cos_ref, sin_ref, q_ref, *, scale):
    cq = cq_ref[...]
    q_lin = jnp.dot(cq, wq_ref[...], preferred_element_type=F32)
    q_sw = jnp.dot(cq, wsw_ref[...], preferred_element_type=F32)
    cos = cos_ref[...]
    sin = sin_ref[...]
    for h in range(MLA_HEADS):
        base = h * QK_PAD
        q_ref[:, base:base + QK_NOPE] = (q_lin[:, base:base + QK_NOPE] * scale).astype(BF16)
        rot = (q_lin[:, base + QK_NOPE:base + QK_PAD] * cos
               + q_sw[:, h * ROPE_PAD:(h + 1) * ROPE_PAD] * sin)
        q_ref[:, base + QK_NOPE:base + QK_PAD] = (rot * scale).astype(BF16)


def _q_proj(cq, w_q, w_q_sw, cos_k, sin_k, tm):
    m = cq.shape[0]
    n_pos_tiles = cos_k.shape[0] // tm
    scale = LOG2_E * (QK_NOPE + QK_ROPE) ** -0.5
    row = lambda i: (i, 0)
    const = lambda i: (0, 0)
    pos = lambda i: (i % n_pos_tiles, 0)
    return pl.pallas_call(
        functools.partial(_q_proj_kernel, scale=scale),
        grid=(m // tm,),
        in_specs=[
            pl.BlockSpec((tm, Q_LORA), row),
            pl.BlockSpec((Q_LORA, MLA_HEADS * QK_PAD), const),
            pl.BlockSpec((Q_LORA, MLA_HEADS * ROPE_PAD), const),
            pl.BlockSpec((tm, ROPE_PAD), pos),
            pl.BlockSpec((tm, ROPE_PAD), pos),
        ],
        out_specs=pl.BlockSpec((tm, MLA_HEADS * QK_PAD), row),
        out_shape=jax.ShapeDtypeStruct((m, MLA_HEADS * QK_PAD), BF16),
        compiler_params=_cparams(("arbitrary",)),
        name="q_proj",
    )(cq, w_q, w_q_sw, cos_k, sin_k)


def _kv_proj_kernel(lat_ref, kr_ref, wk_ref, wvt_ref, k_ref, vt_ref):
    lat = lat_ref[...].astype(BF16)
    kn = jnp.dot(lat, wk_ref[...], preferred_element_type=F32)
    vt_ref[0] = lax.dot_general(wvt_ref[...], lat, NT_DIMS, preferred_element_type=F32).astype(BF16)
    kr = kr_ref[...].astype(BF16)
    for h in range(MLA_HEADS):
        base = h * QK_PAD
        k_ref[:, base:base + QK_NOPE] = kn[:, h * QK_NOPE:(h + 1) * QK_NOPE].astype(BF16)
        k_ref[:, base + QK_NOPE:base + QK_PAD] = kr


def _kv_proj(lat, kr_pad, w_k, w_vt, batch, tm):
    m = lat.shape[0]
    tiles_per_batch = m // batch // tm
    row = lambda i: (i, 0)
    const = lambda i: (0, 0)
    return pl.pallas_call(
        _kv_proj_kernel,
        grid=(m // tm,),
        in_specs=[
            pl.BlockSpec((tm, KV_LORA), row),
            pl.BlockSpec((tm, ROPE_PAD), row),
            pl.BlockSpec((KV_LORA, MLA_HEADS * QK_NOPE), const),
            pl.BlockSpec((D_MLA, KV_LORA), const),
        ],
        out_specs=[
            pl.BlockSpec((tm, MLA_HEADS * QK_PAD), row),
            pl.BlockSpec((1, D_MLA, tm), lambda i: (i // tiles_per_batch, 0, i % tiles_per_batch)),
        ],
        out_shape=[
            jax.ShapeDtypeStruct((m, MLA_HEADS * QK_PAD), BF16),
            jax.ShapeDtypeStruct((batch, D_MLA, m // batch), BF16),
        ],
        compiler_params=_cparams(("arbitrary",)),
        name="kv_proj",
    )(lat, kr_pad, w_k, w_vt)


def _attn_kernel(tile_q_ref, tile_k_ref, q_ref, k_ref, vt_ref, o_ref,
                 vx_ref, rel_ref, sa_ref, sb_ref, m_ref, acc_ref, *,
                 tq, tk, past, n_keys, n_plain, n_tiles, skip_quarter):
    @pl.when(jnp.logical_and(pl.program_id(0) == 0, pl.program_id(1) == 0))
    def _():
        rel_ref[...] = (lax.broadcasted_iota(jnp.int32, (tk, tq), 0) // CHUNK
                        - lax.broadcasted_iota(jnp.int32, (tk, tq), 1) // CHUNK)

    vx_ref[:V_HEAD, :] = vt_ref[0]
    vx_ref[V_HEAD:, :] = jnp.ones((ONES_ROWS, vx_ref.shape[1]), BF16)
    m_ref[...] = jnp.full(m_ref.shape, NEG_BIG, F32)
    acc_ref[...] = jnp.zeros(acc_ref.shape, F32)

    def origin(t):
        return pl.multiple_of(tile_q_ref[t] * tq, tq), pl.multiple_of(tile_k_ref[t] * tk, tk)

    full = ((0, tk, 0, tq),)
    masked_parts = ((0, tk // 2, 0, tq), (tk // 2, tk // 2, tq // 2, tq // 2)) if skip_quarter else full

    def scores(t, s_ref, parts):
        qs, ks = origin(t)
        for r0, rk, c0, cq in parts:
            s_ref[r0:r0 + rk, c0:c0 + cq] = lax.dot_general(
                k_ref[0, pl.ds(pl.multiple_of(ks + r0, rk), rk), :],
                q_ref[0, pl.ds(pl.multiple_of(qs + c0, cq), cq), :], NT_DIMS, preferred_element_type=F32)

    def softmax_pv(t, s_ref, masked):
        qs, ks = origin(t)
        for r0, rk, c0, cq in (masked_parts if masked else full):
            s = s_ref[r0:r0 + rk, c0:c0 + cq]
            if masked:
                k_row = r0 + lax.broadcasted_iota(jnp.int32, (rk, cq), 0)
                pad = jnp.where(k_row < n_keys - ks, 0, 1 << 20)
                visible = rel_ref[r0:r0 + rk, c0:c0 + cq] + pad <= (past + qs - ks) // CHUNK
                s = jnp.where(visible, s, NEG_BIG)
            cols = pl.ds(pl.multiple_of(qs + c0, cq), cq)
            m = m_ref[:, cols]
            m_new = jnp.maximum(m, jnp.max(s, axis=0, keepdims=True))
            p = jnp.exp2(s - m_new).astype(BF16)
            pv = jnp.dot(vx_ref[:, pl.ds(pl.multiple_of(ks + r0, rk), rk)], p,
                         preferred_element_type=F32)
            acc_ref[:, cols] = jnp.exp2(m - m_new) * acc_ref[:, cols] + pv
            m_ref[:, cols] = m_new

    bufs = (sa_ref, sb_ref)

    def step(t, parity, masked, next_masked):
        scores(t + 1, bufs[1 - parity], masked_parts if next_masked else full)
        softmax_pv(t, bufs[parity], masked)

    def run(t0, t1, masked, next_masked):
        t = t0
        if t < t1 and t % 2 == 1:
            step(t, 1, masked, next_masked)
            t += 1
        trips = (t1 - t) // ATTN_UNROLL
        if trips > 0:
            def body(i, _, base=t):
                tt = base + ATTN_UNROLL * i
                for u in range(ATTN_UNROLL):
                    step(tt + u, u % 2, masked, next_masked)
                return 0
            lax.fori_loop(0, trips, body, 0)
            t += ATTN_UNROLL * trips
        while t < t1:
            step(t, t % 2, masked, next_masked)
            t += 1

    last = n_tiles - 1
    plain_end = min(n_plain, last)
    scores(0, sa_ref, full if n_plain else masked_parts)
    run(0, plain_end - 1, False, False)
    if plain_end >= 1:
        step(plain_end - 1, (plain_end - 1) % 2, False, plain_end >= n_plain)
    run(plain_end, last, True, True)
    softmax_pv(last, bufs[last % 2], last >= n_plain)
    acc = acc_ref[...]
    o_ref[0] = (acc[:V_HEAD] / acc[V_HEAD:V_HEAD + 1]).T.astype(BF16)


def _attention_tiles(seq, n_keys_pad, n_keys, past, tq, tk):
    plain, masked = [], []
    for j in range(n_keys_pad // tk):
        for i in range(seq // tq):
            q_lo, k_lo = past + i * tq, j * tk
            q_hi, k_hi = q_lo + tq - 1, k_lo + tk - 1
            if k_lo // CHUNK > q_hi // CHUNK or k_lo >= n_keys:
                continue
            if k_hi // CHUNK <= q_lo // CHUNK and k_hi < n_keys:
                plain.append((i, j))
            else:
                masked.append((i, j))
    return plain, masked


def _attention(q, k, vt, batch, seq, n_keys_pad, n_keys, past, tq, tk):
    assert past % CHUNK == 0 and tk % CHUNK == 0 and (tq % CHUNK == 0 or tq == seq)
    plain, masked = _attention_tiles(seq, n_keys_pad, n_keys, past, tq, tk)
    tiles = plain + masked
    skip_quarter = (tq // 2) % CHUNK == 0 and (tk // 2) % CHUNK == 0 and n_keys == n_keys_pad and all(
        (j * tk + tk // 2) // CHUNK > (past + i * tq + tq // 2 - 1) // CHUNK for i, j in masked)
    tile_q = jnp.asarray([t[0] for t in tiles], jnp.int32)
    tile_k = jnp.asarray([t[1] for t in tiles], jnp.int32)
    smem = pl.BlockSpec(memory_space=pltpu.SMEM)
    return pl.pallas_call(
        functools.partial(_attn_kernel, tq=tq, tk=tk, past=past, n_keys=n_keys,
                          n_plain=len(plain), n_tiles=len(tiles), skip_quarter=skip_quarter),
        grid=(batch, MLA_HEADS),
        in_specs=[
            smem, smem,
            pl.BlockSpec((1, seq, QK_PAD), lambda b, h: (b, 0, h)),
            pl.BlockSpec((1, n_keys_pad, QK_PAD), lambda b, h: (b, 0, h)),
            pl.BlockSpec((1, V_HEAD, n_keys_pad), lambda b, h: (b, h, 0)),
        ],
        out_specs=pl.BlockSpec((1, seq, V_HEAD), lambda b, h: (b, 0, h)),
        out_shape=jax.ShapeDtypeStruct((batch, seq, D_MLA), BF16),
        scratch_shapes=[
            pltpu.VMEM((V_HEAD + ONES_ROWS, n_keys_pad), BF16),
            pltpu.VMEM((tk, tq), jnp.int32),
            pltpu.VMEM((tk, tq), F32),
            pltpu.VMEM((tk, tq), F32),
            pltpu.VMEM((1, seq), F32),
            pltpu.VMEM((V_HEAD + ONES_ROWS, seq), F32),
        ],
        compiler_params=_cparams(("arbitrary", "arbitrary")),
        name="attn",
    )(tile_q, tile_k, q, k, vt)


LAT_KEY = KV_LORA + ROPE_PAD


def _absorbed_attn_kernel(q_ref, latp_ref, krp_ref, latn_ref, krn_ref, wuk_ref, wuv_ref, o_ref,
                          qx_ref, kx_ref, vx_ref, *, seq, past, n_keys):
    for h in range(MLA_HEADS):
        rows = slice(h * seq, (h + 1) * seq)
        q_lat = jnp.dot(q_ref[:, h * QK_PAD:h * QK_PAD + QK_NOPE], wuk_ref[h], preferred_element_type=F32)
        qx_ref[rows, :KV_LORA] = q_lat.astype(BF16)
        qx_ref[rows, KV_LORA:] = q_ref[:, h * QK_PAD + QK_NOPE:(h + 1) * QK_PAD]
    kx_ref[:past, :KV_LORA] = latp_ref[0].astype(BF16)
    kx_ref[:past, KV_LORA:] = krp_ref[0].astype(BF16)
    kx_ref[past:n_keys, :KV_LORA] = latn_ref[...].astype(BF16)
    kx_ref[past:n_keys, KV_LORA:] = krn_ref[...].astype(BF16)
    kx_ref[n_keys:, :] = jnp.zeros((kx_ref.shape[0] - n_keys, LAT_KEY), BF16)
    kx = kx_ref[...]
    s = lax.dot_general(qx_ref[...], kx, NT_DIMS, preferred_element_type=F32)
    q_chunk = (past + lax.broadcasted_iota(jnp.int32, (s.shape[0], 1), 0) % seq) // CHUNK
    k_idx = lax.broadcasted_iota(jnp.int32, (1, s.shape[1]), 1)
    visible = jnp.logical_and(k_idx // CHUNK <= q_chunk, k_idx < n_keys)
    s = jnp.where(visible, s, NEG_BIG)
    p = jnp.exp2(s - jnp.max(s, axis=-1, keepdims=True)).astype(BF16)
    vx_ref[:, :KV_LORA] = kx[:, :KV_LORA]
    vx_ref[:, KV_LORA:] = jnp.ones((vx_ref.shape[0], ROPE_PAD), BF16)
    o_ext = jnp.dot(p, vx_ref[...], preferred_element_type=F32)
    denom = o_ext[:, KV_LORA:]
    o_lat = (o_ext[:, :KV_LORA] / jnp.concatenate([denom] * (KV_LORA // ROPE_PAD), axis=1)).astype(BF16)
    for h in range(MLA_HEADS):
        o_ref[:, h * V_HEAD:(h + 1) * V_HEAD] = jnp.dot(
            o_lat[h * seq:(h + 1) * seq], wuv_ref[h], preferred_element_type=F32).astype(BF16)


def _absorbed_attention(q, lat_past, kr_past, lat_new, kr_new, w_uk_t, w_uv, batch, seq, n_keys_pad):
    past = lat_past.shape[1]
    n_keys = past + seq
    const3 = lambda b: (0, 0, 0)
    return pl.pallas_call(
        functools.partial(_absorbed_attn_kernel, seq=seq, past=past, n_keys=n_keys),
        grid=(batch,),
        in_specs=[
            pl.BlockSpec((seq, MLA_HEADS * QK_PAD), lambda b: (b, 0)),
            pl.BlockSpec((1, past, KV_LORA), lambda b: (b, 0, 0)),
            pl.BlockSpec((1, past, ROPE_PAD), lambda b: (b, 0, 0)),
            pl.BlockSpec((seq, KV_LORA), lambda b: (b, 0)),
            pl.BlockSpec((seq, ROPE_PAD), lambda b: (b, 0)),
            pl.BlockSpec((MLA_HEADS, QK_NOPE, KV_LORA), const3),
            pl.BlockSpec((MLA_HEADS, KV_LORA, V_HEAD), const3),
        ],
        out_specs=pl.BlockSpec((seq, D_MLA), lambda b: (b, 0)),
        out_shape=jax.ShapeDtypeStruct((batch * seq, D_MLA), BF16),
        scratch_shapes=[
            pltpu.VMEM((MLA_HEADS * seq, LAT_KEY), BF16),
            pltpu.VMEM((n_keys_pad, LAT_KEY), BF16),
            pltpu.VMEM((n_keys_pad, LAT_KEY), BF16),
        ],
        compiler_params=_cparams(("arbitrary",)),
        name="absorbed_attn",
    )(q, lat_past, kr_past, lat_new, kr_new, w_uk_t, w_uv)


def _out_proj_kernel(oh_ref, oa_ref, h_ref, w_ref, g1_ref, b1_ref, y_ref, yb_ref):
    for r0 in range(0, h_ref.shape[0], ROW_SUB):
        rows = slice(r0, r0 + ROW_SUB)
        mix = jnp.dot(oh_ref[rows, :], w_ref[:D_HGRN, :], preferred_element_type=F32)
        mix = mix + jnp.dot(oa_ref[rows, :], w_ref[D_HGRN:, :], preferred_element_type=F32)
        y = _layer_norm(ALPHA * h_ref[rows, :] + mix, g1_ref[...], b1_ref[...])
        y_ref[rows, :] = y
        yb_ref[rows, :] = y.astype(BF16)


def _out_proj(oh, oa, h, w_out, ln1_g, ln1_b, tm):
    m = h.shape[0]
    row = lambda i: (i, 0)
    const = lambda i: (0, 0)
    vec = pl.BlockSpec((1, D_MODEL), const)
    return pl.pallas_call(
        _out_proj_kernel,
        grid=(m // tm,),
        in_specs=[
            pl.BlockSpec((tm, D_HGRN), row),
            pl.BlockSpec((tm, D_MLA), row),
            pl.BlockSpec((tm, D_MODEL), row),
            pl.BlockSpec((D_HGRN + D_MLA, D_MODEL), const),
            vec, vec,
        ],
        out_specs=[pl.BlockSpec((tm, D_MODEL), row), pl.BlockSpec((tm, D_MODEL), row)],
        out_shape=[jax.ShapeDtypeStruct((m, D_MODEL), F32), jax.ShapeDtypeStruct((m, D_MODEL), BF16)],
        compiler_params=_cparams(("arbitrary",)),
        name="out_proj",
    )(oh, oa, h, w_out, ln1_g, ln1_b)


def _ffn_up_kernel(x_ref, wu_ref, wg_ref, cw_ref, cb_ref, buf_ref, h_ref, cnew_ref, carry_ref, *,
                   spt, tiles_per_seq):
    i = pl.program_id(1)
    tm = x_ref.shape[0]
    tf = wu_ref.shape[1]
    rows = tm // spt

    if tiles_per_seq > 1:
        @pl.when(i % tiles_per_seq == 0)
        def _():
            carry_ref[...] = buf_ref[...]
        halo_ref = carry_ref
    else:
        halo_ref = buf_ref

    xb = x_ref[...]
    for c0 in range(0, tf, FFN_SUB):
        width = min(FFN_SUB, tf - c0)
        cs = slice(c0, c0 + width)
        r = lax.broadcasted_iota(jnp.int32, (spt, rows, width), 1)
        if 2 * width <= FFN_SUB:
            both = jnp.dot(xb, jnp.concatenate([wu_ref[:, cs], wg_ref[:, cs]], axis=1),
                           preferred_element_type=F32)
            u, gate = both[:, :width], both[:, width:]
        else:
            u = jnp.dot(xb, wu_ref[:, cs], preferred_element_type=F32)
            gate = jnp.dot(xb, wg_ref[:, cs], preferred_element_type=F32)
        u = u.reshape(spt, rows, width)
        gate = gate.reshape(spt, rows, width)
        halo = halo_ref[:, :, cs]
        prev1 = jnp.where(r == 0, halo[:, 1:2, :], pltpu.roll(u, 1, 1))
        prev2 = jnp.where(r == 0, halo[:, 0:1, :],
                          jnp.where(r == 1, halo[:, 1:2, :], pltpu.roll(u, 2, 1)))
        cw = cw_ref[:, cs]
        a = cb_ref[:, cs] + (prev2 * cw[0:1] + prev1 * cw[1:2] + u * cw[2:3])
        h_ref[:, cs] = (jax.nn.silu(a) * gate).astype(BF16).reshape(tm, width)
        tail = u[:, rows - (CONV_W - 1):, :]
        cnew_ref[:, :, cs] = tail
        if tiles_per_seq > 1:
            carry_ref[:, :, cs] = tail


def _ffn_up(x, w_up, w_gate, conv_w, conv_b, conv_buf, seq, tm, tf):
    m = x.shape[0]
    n_seq = m // seq
    if tm >= seq:
        spt, tiles_per_seq = tm // seq, 1
        seq_of = lambda i: i
    else:
        spt, tiles_per_seq = 1, seq // tm
        seq_of = lambda i: i // tiles_per_seq
    n_f = D_FF // tf
    hid, tails = pl.pallas_call(
        functools.partial(_ffn_up_kernel, spt=spt, tiles_per_seq=tiles_per_seq),
        grid=(n_f, m // tm),
        in_specs=[
            pl.BlockSpec((tm, D_MODEL), lambda f, i: (i, 0)),
            pl.BlockSpec((D_MODEL, tf), lambda f, i: (0, f)),
            pl.BlockSpec((D_MODEL, tf), lambda f, i: (0, f)),
            pl.BlockSpec((CONV_W, tf), lambda f, i: (0, f)),
            pl.BlockSpec((1, tf), lambda f, i: (0, f)),
            pl.BlockSpec((spt, CONV_W - 1, tf), lambda f, i: (seq_of(i), 0, f)),
        ],
        out_specs=[
            pl.BlockSpec((tm, tf), lambda f, i: (i, f)),
            pl.BlockSpec((spt, CONV_W - 1, tf), lambda f, i: (i, 0, f)),
        ],
        out_shape=[
            jax.ShapeDtypeStruct((m, D_FF), BF16),
            jax.ShapeDtypeStruct((n_seq * tiles_per_seq, CONV_W - 1, D_FF), F32),
        ],
        scratch_shapes=[pltpu.VMEM((spt, CONV_W - 1, tf), F32)],
        compiler_params=_cparams(("arbitrary", "arbitrary")),
        name="ffn_up",
    )(x, w_up, w_gate, conv_w, conv_b, conv_buf)
    return hid, tails[tiles_per_seq - 1::tiles_per_seq]


def _ffn_down_kernel(h_ref, x_ref, w_ref, g_ref, b_ref, y_ref):
    for r0 in range(0, x_ref.shape[0], ROW_SUB):
        rows = slice(r0, r0 + ROW_SUB)
        ff = jnp.dot(h_ref[rows, :], w_ref[...], preferred_element_type=F32)
        y_ref[rows, :] = _layer_norm(ALPHA * x_ref[rows, :] + ff, g_ref[...], b_ref[...])


def _ffn_down(hid, x, w_down, ln2_g, ln2_b, tm):
    m = x.shape[0]
    row = lambda i: (i, 0)
    const = lambda i: (0, 0)
    return pl.pallas_call(
        _ffn_down_kernel,
        grid=(m // tm,),
        in_specs=[
            pl.BlockSpec((tm, D_FF), row),
            pl.BlockSpec((tm, D_MODEL), row),
            pl.BlockSpec((D_FF, D_MODEL), const, pipeline_mode=pl.Buffered(1)),
            pl.BlockSpec((1, D_MODEL), const),
            pl.BlockSpec((1, D_MODEL), const),
        ],
        out_specs=pl.BlockSpec((tm, D_MODEL), row),
        out_shape=jax.ShapeDtypeStruct((m, D_MODEL), F32),
        compiler_params=_cparams(("arbitrary",)),
        name="ffn_down",
    )(hid, x, w_down, ln2_g, ln2_b)


def _rope_tables(past, seq, rows):
    inv = ROPE_THETA ** (-np.arange(0, QK_ROPE, 2, dtype=np.float64) / QK_ROPE)
    ang = (past + np.arange(seq, dtype=np.float64))[:, None] * inv[None, :]
    cos, sin = np.cos(ang), np.sin(ang)
    zero = np.zeros((seq, ROPE_PAD - QK_ROPE))
    cos_t = np.concatenate([cos, cos, zero], axis=1).astype(np.float32)
    sin_t = np.concatenate([-sin, sin, zero], axis=1).astype(np.float32)
    reps = rows // seq
    return jnp.asarray(np.tile(cos_t, (reps, 1))), jnp.asarray(np.tile(sin_t, (reps, 1)))


def _swap_halves(w):
    half = QK_ROPE // 2
    return jnp.concatenate([w[..., half:], w[..., :half]], axis=-1)


def _prep_weights(w_in, w_q_b, w_kv_b, w_out, w_ffn_up, w_ffn_gate, w_ffn_down):
    w_in = w_in[0]
    c0 = 4 * D_HGRN
    w_h = w_in.astype(BF16)
    kr_cols = w_in[:, c0 + Q_LORA + KV_LORA:]
    zpad = jnp.zeros((D_MODEL, ROPE_PAD - QK_ROPE), F32)
    w_m = jnp.concatenate([kr_cols, zpad, _swap_halves(kr_cols), zpad], axis=1).astype(BF16)

    wq = w_q_b[0].reshape(Q_LORA, MLA_HEADS, QK_NOPE + QK_ROPE)
    zq = jnp.zeros((Q_LORA, MLA_HEADS, QK_PAD - QK_NOPE - QK_ROPE), F32)
    w_q = jnp.concatenate([wq, zq], axis=-1).reshape(Q_LORA, MLA_HEADS * QK_PAD).astype(BF16)
    zr = jnp.zeros((Q_LORA, MLA_HEADS, ROPE_PAD - QK_ROPE), F32)
    w_q_sw = jnp.concatenate([_swap_halves(wq[..., QK_NOPE:]), zr], axis=-1)
    w_q_sw = w_q_sw.reshape(Q_LORA, MLA_HEADS * ROPE_PAD).astype(BF16)

    wkv = w_kv_b[0].reshape(KV_LORA, MLA_HEADS, QK_NOPE + V_HEAD)
    w_k = wkv[..., :QK_NOPE].reshape(KV_LORA, MLA_HEADS * QK_NOPE).astype(BF16)
    w_vt = wkv[..., QK_NOPE:].reshape(KV_LORA, D_MLA).T.astype(BF16)
    w_uk_t = wkv[..., :QK_NOPE].transpose(1, 2, 0).astype(BF16)
    w_uv = wkv[..., QK_NOPE:].transpose(1, 0, 2).astype(BF16)
    return dict(w_h=w_h, w_m=w_m, w_q=w_q, w_q_sw=w_q_sw, w_k=w_k, w_vt=w_vt, w_uk_t=w_uk_t, w_uv=w_uv,
                w_out=w_out[0].astype(BF16), w_up=w_ffn_up[0].astype(BF16),
                w_gate=w_ffn_gate[0].astype(BF16), w_down=w_ffn_down[0].astype(BF16))


def _row(a):
    return a.reshape(1, -1)


def _tile_plan(batch, seq, n_keys_pad):
    m = batch * seq
    tm = min(ROW_TILE, m)
    tm_q = min(QKV_ROW_TILE, m)
    tm_ffn = min(FFN_ROW_TILE, m)
    assert m % tm == 0 and m % tm_q == 0 and m % tm_ffn == 0 and (tm_ffn % seq == 0 or seq % tm_ffn == 0)
    return dict(
        tm=tm,
        tm_q=tm_q,
        tm_ffn=tm_ffn,
        rows_hgrn=min(HGRN_STEP_ROWS, seq),
        tq=min(ATTN_TILE, seq),
        tk=ATTN_TILE if n_keys_pad % ATTN_TILE == 0 else n_keys_pad,
        tm_kv=QKV_ROW_TILE if n_keys_pad % QKV_ROW_TILE == 0 else n_keys_pad,
    )


def _encoder(x, s0, lat_past, kr_past, conv_buf, p, w, *, blk):
    batch, seq, _ = x.shape
    past = 0 if lat_past is None else lat_past.shape[1]
    m = batch * seq
    n_keys = past + seq
    key_pad = -n_keys % KEY_ALIGN
    n_keys_pad = n_keys + key_pad
    t = _tile_plan(batch, seq, n_keys_pad)
    x2 = x.reshape(m, D_MODEL)
    cos_k, sin_k = _rope_tables(past, seq, max(seq, t["tm_q"]))

    hg, cqn, lat_new, kr_new, h = _in_proj(x2, p["ln_in_g"], p["ln_in_b"], w["w_h"], w["w_m"],
                                           p["q_a_g"], p["kv_a_g"], cos_k, sin_k, t["tm"])
    o_h, s_new = _hgrn(hg, p["lb"], p["hgrn_norm_g"], s0, batch, seq, blk, t["rows_hgrn"])

    q = _q_proj(cqn, w["w_q"], w["w_q_sw"], cos_k, sin_k, t["tm_q"])
    lat3 = lat_new.reshape(batch, seq, KV_LORA)
    kr3 = kr_new.reshape(batch, seq, ROPE_PAD)
    if past and seq * MLA_HEADS <= ABSORB_MAX_ROWS:
        kr_past_pad = jnp.pad(kr_past, ((0, 0), (0, 0), (0, ROPE_PAD - QK_ROPE)))
        o_a = _absorbed_attention(q, lat_past, kr_past_pad, lat_new, kr_new, w["w_uk_t"], w["w_uv"],
                                  batch, seq, n_keys_pad)
    else:
        parts_l, parts_r = [lat3], [kr3]
        if past:
            parts_l.insert(0, lat_past)
            parts_r.insert(0, jnp.pad(kr_past, ((0, 0), (0, 0), (0, ROPE_PAD - QK_ROPE))))
        if key_pad:
            parts_l.append(jnp.zeros((batch, key_pad, KV_LORA), F32))
            parts_r.append(jnp.zeros((batch, key_pad, ROPE_PAD), F32))
        lat_all = jnp.concatenate(parts_l, axis=1) if len(parts_l) > 1 else lat3
        kr_all = jnp.concatenate(parts_r, axis=1) if len(parts_r) > 1 else kr3
        k_cat, vt = _kv_proj(lat_all.reshape(batch * n_keys_pad, KV_LORA),
                             kr_all.reshape(batch * n_keys_pad, ROPE_PAD), w["w_k"], w["w_vt"], batch, t["tm_kv"])
        o_a = _attention(q.reshape(batch, seq, MLA_HEADS * QK_PAD),
                         k_cat.reshape(batch, n_keys_pad, MLA_HEADS * QK_PAD), vt,
                         batch, seq, n_keys_pad, n_keys, past, t["tq"], t["tk"]).reshape(m, D_MLA)

    h1, h1_bf16 = _out_proj(o_h, o_a, h, w["w_out"], p["ln1_g"], p["ln1_b"], t["tm"])
    hid, conv_new = _ffn_up(h1_bf16, w["w_up"], w["w_gate"], p["conv_w"], p["conv_b"], conv_buf, seq, t["tm_ffn"],
                            FFN_COL_TILE)
    y = _ffn_down(hid, h1, w["w_down"], p["ln2_g"], p["ln2_b"], t["tm"])
    return (y.reshape(batch, seq, D_MODEL), s_new[None], lat3[None],
            kr3[:, :, :QK_ROPE][None], conv_new[None])


def kernel(x_prompt, x_sample, cache_kv_latent, cache_k_rope, state_hgrn, cache_ffn_conv, lb_param, ln_in_g, ln_in_b, w_in, hgrn_norm_g, q_a_g, w_q_b, kv_a_g, w_kv_b, w_out, ln1_g, ln1_b, w_ffn_up, w_ffn_gate, conv_w, conv_b, w_ffn_down, ln2_g, ln2_b):
    w = _prep_weights(w_in, w_q_b, w_kv_b, w_out, w_ffn_up, w_ffn_gate, w_ffn_down)
    lbs = jnp.cumsum(jax.nn.softmax(lb_param.astype(F32), axis=0), axis=0)
    p = dict(ln_in_g=_row(ln_in_g), ln_in_b=_row(ln_in_b), lb=_row(lbs[0]),
             hgrn_norm_g=_row(hgrn_norm_g[0]), q_a_g=_row(q_a_g[0]), kv_a_g=_row(kv_a_g[0]),
             ln1_g=_row(ln1_g[0]), ln1_b=_row(ln1_b[0]), conv_w=conv_w[0], conv_b=_row(conv_b[0]),
             ln2_g=_row(ln2_g[0]), ln2_b=_row(ln2_b[0]))

    n_prompt = x_prompt.shape[0]
    conv0 = jnp.zeros((n_prompt, CONV_W - 1, D_FF), F32)
    y_p, s_p, lat_p, kr_p, conv_p = _encoder(x_prompt, None, None, None, conv0, p, w, blk=CHUNK)
    y_s, s_s, lat_s, kr_s, conv_s = _encoder(
        x_sample, state_hgrn[0], cache_kv_latent[0], cache_k_rope[0], cache_ffn_conv[0], p, w,
        blk=x_sample.shape[1])
    return (y_p, y_s, lat_p, kr_p, s_p, conv_p, lat_s, kr_s, s_s, conv_s)
```

```python
import functools

import numpy as np
import jax
import jax.numpy as jnp
from jax import lax
from jax.experimental import pallas as pl
from jax.experimental.pallas import tpu as pltpu

F32 = jnp.float32
BF16 = jnp.bfloat16

D_MODEL = 2048
CHUNK = 64
HGRN_HEADS = 8
HGRN_DK = 128
HGRN_DV = 128
D_HGRN = HGRN_HEADS * HGRN_DK
MLA_HEADS = 8
Q_LORA = 512
KV_LORA = 256
QK_NOPE = 128
QK_ROPE = 64
V_HEAD = 128
D_MLA = MLA_HEADS * V_HEAD
D_FF = 5632
CONV_W = 3
ROPE_THETA = 10000.0
LN_EPS = 1e-5
RMS_EPS = 1e-6
DEPTH = 1
ALPHA = (2.0 * DEPTH) ** 0.25

V7X_LANES = 128
V7X_MXU_WIDTH = 256
V7X_BF16_SUBLANES = 16
V7X_VMEM_LIMIT = 56 * 1024 * 1024

QK_PAD = V7X_MXU_WIDTH
ROPE_PAD = V7X_LANES
KEY_ALIGN = V7X_LANES
ONES_ROWS = V7X_BF16_SUBLANES
FFN_SUB = V7X_MXU_WIDTH

ROW_TILE = 512
QKV_ROW_TILE = 1024
FFN_ROW_TILE = 1024
HGRN_STEP_ROWS = 512
ABSORB_MAX_ROWS = 512
ATTN_TILE = 512
FFN_COL_TILE = 1408
ROW_SUB = 256
ATTN_UNROLL = 8
NEG_BIG = -1e30
LOG2_E = 1.4426950408889634

NT_DIMS = (((1,), (1,)), ((), ()))
TN_DIMS = (((0,), (0,)), ((), ()))


def _cparams(sem):
    return pltpu.CompilerParams(dimension_semantics=sem, vmem_limit_bytes=V7X_VMEM_LIMIT)


def _layer_norm(x, g, b):
    mu = jnp.mean(x, axis=-1, keepdims=True)
    xc = x - mu
    var = jnp.mean(xc * xc, axis=-1, keepdims=True)
    return xc * lax.rsqrt(var + LN_EPS) * g + b


def _rms_norm(x, g):
    return x * lax.rsqrt(jnp.mean(x * x, axis=-1, keepdims=True) + RMS_EPS) * g


HG_TILE = 2 * D_HGRN
N_HG_TILES = 4 * D_HGRN // HG_TILE
W_MLA_COLS = 2 * ROPE_PAD


def _in_proj_kernel(x_ref, g_ref, b_ref, wh_ref, wm_ref, qg_ref, kvg_ref, cos_ref, sin_ref,
                    hg_ref, cq_ref, lat_ref, kr_ref, h_ref, hb_ref):
    j = pl.program_id(1)

    @pl.when(j == 0)
    def _():
        for r0 in range(0, x_ref.shape[0], ROW_SUB):
            rows = slice(r0, r0 + ROW_SUB)
            h = _layer_norm(x_ref[rows, :], g_ref[...], b_ref[...])
            h_ref[rows, :] = h
            hb = h.astype(BF16)
            hb_ref[rows, :] = hb
            hg_ref[rows, :] = jnp.dot(hb, wh_ref[:, :HG_TILE], preferred_element_type=F32)

    @pl.when(jnp.logical_and(j > 0, j < N_HG_TILES))
    def _():
        cols = pl.ds(pl.multiple_of(j * HG_TILE, HG_TILE), HG_TILE)
        hg_ref[...] = jnp.dot(hb_ref[...], wh_ref[:, cols], preferred_element_type=F32)

    @pl.when(j == N_HG_TILES)
    def _():
        hb = hb_ref[...]
        lat_cols = pl.ds(N_HG_TILES * HG_TILE, Q_LORA + KV_LORA)
        p = jnp.dot(hb, wh_ref[:, lat_cols], preferred_element_type=F32)
        r = jnp.dot(hb, wm_ref[...], preferred_element_type=F32)
        cq = p[:, :Q_LORA]
        ckv = p[:, Q_LORA:]
        kr = r[:, :ROPE_PAD]
        kr_sw = r[:, ROPE_PAD:]
        cq_ref[...] = _rms_norm(cq, qg_ref[...]).astype(BF16)
        lat_ref[...] = _rms_norm(ckv, kvg_ref[...])
        kr_ref[...] = kr * cos_ref[...] + kr_sw * sin_ref[...]


def _in_proj(x, ln_g, ln_b, w_h, w_m, q_a_g, kv_a_g, cos_k, sin_k, tm):
    m = x.shape[0]
    n_pos_tiles = cos_k.shape[0] // tm
    grid = (m // tm, N_HG_TILES + 1)
    row = lambda i, j: (i, 0)
    const = lambda i, j: (0, 0)
    pos = lambda i, j: (i % n_pos_tiles, 0)
    hg_col = lambda i, j: (i, jnp.minimum(j, N_HG_TILES - 1))
    return pl.pallas_call(
        _in_proj_kernel,
        grid=grid,
        in_specs=[
            pl.BlockSpec((tm, D_MODEL), row),
            pl.BlockSpec((1, D_MODEL), const),
            pl.BlockSpec((1, D_MODEL), const),
            pl.BlockSpec(w_h.shape, const, pipeline_mode=pl.Buffered(1)),
            pl.BlockSpec((D_MODEL, W_MLA_COLS), const, pipeline_mode=pl.Buffered(1)),
            pl.BlockSpec((1, Q_LORA), const),
            pl.BlockSpec((1, KV_LORA), const),
            pl.BlockSpec((tm, ROPE_PAD), pos),
            pl.BlockSpec((tm, ROPE_PAD), pos),
        ],
        out_specs=[
            pl.BlockSpec((tm, HG_TILE), hg_col),
            pl.BlockSpec((tm, Q_LORA), row),
            pl.BlockSpec((tm, KV_LORA), row),
            pl.BlockSpec((tm, ROPE_PAD), row),
            pl.BlockSpec((tm, D_MODEL), row),
        ],
        out_shape=[
            jax.ShapeDtypeStruct((m, N_HG_TILES * HG_TILE), F32),
            jax.ShapeDtypeStruct((m, Q_LORA), BF16),
            jax.ShapeDtypeStruct((m, KV_LORA), F32),
            jax.ShapeDtypeStruct((m, ROPE_PAD), F32),
            jax.ShapeDtypeStruct((m, D_MODEL), F32),
        ],
        scratch_shapes=[pltpu.VMEM((tm, D_MODEL), BF16)],
        compiler_params=_cparams(("arbitrary", "arbitrary")),
        name="in_proj",
    )(x, ln_g, ln_b, w_h, w_m, q_a_g, kv_a_g, cos_k, sin_k)


def _cumsum_rows(x):
    c = x.shape[0]
    row = lax.broadcasted_iota(jnp.int32, (c, 1), 0)
    shift = 1
    while shift < c:
        x = x + jnp.where(row >= shift, pltpu.roll(x, shift, 0), 0.0)
        shift *= 2
    return x


def _split_level(c):
    t = lax.broadcasted_iota(jnp.int32, (c, c), 0)
    s = lax.broadcasted_iota(jnp.int32, (c, c), 1)
    x = jnp.bitwise_xor(t, s)
    lvl = jnp.full((c, c), -1, jnp.int32)
    for bit in range(c.bit_length() - 1):
        lvl = jnp.where(x >= (1 << bit), bit, lvl)
    return jnp.where(t < s, -2, lvl)


def _hgrn_head_block(q, fpre, v, lb, st, lvl):
    c = q.shape[0]
    f = lb + (1.0 - lb) * jax.nn.sigmoid(fpre)
    g = jnp.log(f)
    k = 1.0 - f
    b = _cumsum_rows(g)
    row = lax.broadcasted_iota(jnp.int32, (c, 1), 0)

    a = jnp.where(lvl == -1, jnp.sum(q * k, axis=-1, keepdims=True), 0.0)
    for level in range(c.bit_length() - 1):
        half = 1 << level
        pos = row % (2 * half)
        is_query = pos >= half
        if level == 0:
            decay = jnp.where(is_query, f, 1.0)
        else:
            if 2 * half < 8:
                ref = b
                for p in range(2 * half):
                    d = half - 1 - p
                    if d != 0:
                        ref = jnp.where(pos == p, pltpu.roll(b, (-d) % c, 0), ref)
            else:
                ref = jnp.concatenate(
                    [jnp.broadcast_to(b[j + half - 1:j + half], (2 * half, b.shape[1]))
                     for j in range(0, c, 2 * half)], axis=0)
            decay = jnp.exp(-jnp.abs(b - ref))
        z = (jnp.where(is_query, q, k) * decay).astype(BF16)
        gram = lax.dot_general(z, z, NT_DIMS, preferred_element_type=F32)
        a = jnp.where(lvl == level, gram, a)

    vb = v.astype(BF16)
    o = jnp.dot(a.astype(BF16), vb, preferred_element_type=F32)
    o = o + lax.dot_general((q * jnp.exp(b)).astype(BF16), st.astype(BF16), NT_DIMS,
                            preferred_element_type=F32)
    b_last = b[c - 1:c]
    k_dec = (k * jnp.exp(b_last - b)).astype(BF16)
    st_new = jnp.exp(b_last) * st + lax.dot_general(vb, k_dec, TN_DIMS, preferred_element_type=F32)
    return o, st_new


def _hgrn_kernel(*refs, c, has_s0):
    if has_s0:
        q_ref, f_ref, i_ref, g_ref, lb_ref, ng_ref, s0_ref, o_ref, s_ref, st_ref = refs
    else:
        q_ref, f_ref, i_ref, g_ref, lb_ref, ng_ref, o_ref, s_ref, st_ref = refs
        s0_ref = None
    t = pl.program_id(1)

    @pl.when(t == 0)
    def _():
        for h in range(HGRN_HEADS):
            if has_s0:
                st_ref[h] = s0_ref[0, h].T
            else:
                st_ref[h] = jnp.zeros((HGRN_DV, HGRN_DK), F32)

    lvl = _split_level(c)

    def block(i, _):
        rows = pl.ds(pl.multiple_of(i * c, c), c)
        for h in range(HGRN_HEADS):
            cols = slice(h * HGRN_DK, (h + 1) * HGRN_DK)
            o, st_new = _hgrn_head_block(q_ref[rows, cols], f_ref[rows, cols], i_ref[rows, cols],
                                         lb_ref[:, cols], st_ref[h], lvl)
            st_ref[h] = st_new
            o = _rms_norm(o, ng_ref[:, cols]) * jax.nn.silu(g_ref[rows, cols])
            o_ref[rows, cols] = o.astype(BF16)
        return 0

    lax.fori_loop(0, q_ref.shape[0] // c, block, 0)

    @pl.when(t == pl.num_programs(1) - 1)
    def _():
        for h in range(HGRN_HEADS):
            s_ref[0, h] = st_ref[h].T


def _hgrn(hg, lb, norm_g, s0, batch, seq, c, rows):
    nt = seq // rows
    has_s0 = s0 is not None
    col = lambda n: (lambda b, t: (b * nt + t, n))
    const = lambda b, t: (0, 0)
    state = lambda b, t: (b, 0, 0, 0)
    in_specs = [pl.BlockSpec((rows, D_HGRN), col(n)) for n in range(4)]
    in_specs += [pl.BlockSpec((1, D_HGRN), const), pl.BlockSpec((1, D_HGRN), const)]
    args = [hg, hg, hg, hg, lb, norm_g]
    if has_s0:
        in_specs.append(pl.BlockSpec((1, HGRN_HEADS, HGRN_DK, HGRN_DV), state))
        args.append(s0)
    return pl.pallas_call(
        functools.partial(_hgrn_kernel, c=c, has_s0=has_s0),
        grid=(batch, nt),
        in_specs=in_specs,
        out_specs=[
            pl.BlockSpec((rows, D_HGRN), lambda b, t: (b * nt + t, 0)),
            pl.BlockSpec((1, HGRN_HEADS, HGRN_DK, HGRN_DV), state),
        ],
        out_shape=[
            jax.ShapeDtypeStruct((batch * seq, D_HGRN), BF16),
            jax.ShapeDtypeStruct((batch, HGRN_HEADS, HGRN_DK, HGRN_DV), F32),
        ],
        scratch_shapes=[pltpu.VMEM((HGRN_HEADS, HGRN_DV, HGRN_DK), F32)],
        compiler_params=_cparams(("arbitrary", "arbitrary")),
        name="hgrn",
    )(*args)


def _q_proj_kernel(cq_ref, wq_ref, wsw_ref, cos_ref, sin_ref, q_ref, *, scale):
    cq = cq_ref[...]
    q_lin = jnp.dot(cq, wq_ref[...], preferred_element_type=F32)
    q_sw = jnp.dot(cq, wsw_ref[...], preferred_element_type=F32)
    cos = cos_ref[...]
    sin = sin_ref[...]
    for h in range(MLA_HEADS):
        base = h * QK_PAD
        q_ref[:, base:base + QK_NOPE] = (q_lin[:, base:base + QK_NOPE] * scale).astype(BF16)
        rot = (q_lin[:, base + QK_NOPE:base + QK_PAD] * cos
               + q_sw[:, h * ROPE_PAD:(h + 1) * ROPE_PAD] * sin)
        q_ref[:, base + QK_NOPE:base + QK_PAD] = (rot * scale).astype(BF16)


def _q_proj(cq, w_q, w_q_sw, cos_k, sin_k, tm):
    m = cq.shape[0]
    n_pos_tiles = cos_k.shape[0] // tm
    scale = LOG2_E * (QK_NOPE + QK_ROPE) ** -0.5
    row = lambda i: (i, 0)
    const = lambda i: (0, 0)
    pos = lambda i: (i % n_pos_tiles, 0)
    return pl.pallas_call(
        functools.partial(_q_proj_kernel, scale=scale),
        grid=(m // tm,),
        in_specs=[
            pl.BlockSpec((tm, Q_LORA), row),
            pl.BlockSpec((Q_LORA, MLA_HEADS * QK_PAD), const),
            pl.BlockSpec((Q_LORA, MLA_HEADS * ROPE_PAD), const),
            pl.BlockSpec((tm, ROPE_PAD), pos),
            pl.BlockSpec((tm, ROPE_PAD), pos),
        ],
        out_specs=pl.BlockSpec((tm, MLA_HEADS * QK_PAD), row),
        out_shape=jax.ShapeDtypeStruct((m, MLA_HEADS * QK_PAD), BF16),
        compiler_params=_cparams(("arbitrary",)),
        name="q_proj",
    )(cq, w_q, w_q_sw, cos_k, sin_k)


def _kv_proj_kernel(lat_ref, kr_ref, wk_ref, wvt_ref, k_ref, vt_ref):
    lat = lat_ref[...].astype(BF16)
    kn = jnp.dot(lat, wk_ref[...], preferred_element_type=F32)
    vt_ref[0] = lax.dot_general(wvt_ref[...], lat, NT_DIMS, preferred_element_type=F32).astype(BF16)
    kr = kr_ref[...].astype(BF16)
    for h in range(MLA_HEADS):
        base = h * QK_PAD
        k_ref[:, base:base + QK_NOPE] = kn[:, h * QK_NOPE:(h + 1) * QK_NOPE].astype(BF16)
        k_ref[:, base + QK_NOPE:base + QK_PAD] = kr


def _kv_proj(lat, kr_pad, w_k, w_vt, batch, tm):
    m = lat.shape[0]
    tiles_per_batch = m // batch // tm
    row = lambda i: (i, 0)
    const = lambda i: (0, 0)
    return pl.pallas_call(
        _kv_proj_kernel,
        grid=(m // tm,),
        in_specs=[
            pl.BlockSpec((tm, KV_LORA), row),
            pl.BlockSpec((tm, ROPE_PAD), row),
            pl.BlockSpec((KV_LORA, MLA_HEADS * QK_NOPE), const),
            pl.BlockSpec((D_MLA, KV_LORA), const),
        ],
        out_specs=[
            pl.BlockSpec((tm, MLA_HEADS * QK_PAD), row),
            pl.BlockSpec((1, D_MLA, tm), lambda i: (i // tiles_per_batch, 0, i % tiles_per_batch)),
        ],
        out_shape=[
            jax.ShapeDtypeStruct((m, MLA_HEADS * QK_PAD), BF16),
            jax.ShapeDtypeStruct((batch, D_MLA, m // batch), BF16),
        ],
        compiler_params=_cparams(("arbitrary",)),
        name="kv_proj",
    )(lat, kr_pad, w_k, w_vt)


def _attn_kernel(tile_q_ref, tile_k_ref, q_ref, k_ref, vt_ref, o_ref,
                 vx_ref, rel_ref, sa_ref, sb_ref, m_ref, acc_ref, *,
                 tq, tk, past, n_keys, n_plain, n_tiles, skip_quarter):
    @pl.when(jnp.logical_and(pl.program_id(0) == 0, pl.program_id(1) == 0))
    def _():
        rel_ref[...] = (lax.broadcasted_iota(jnp.int32, (tk, tq), 0) // CHUNK
                        - lax.broadcasted_iota(jnp.int32, (tk, tq), 1) // CHUNK)

    vx_ref[:V_HEAD, :] = vt_ref[0]
    vx_ref[V_HEAD:, :] = jnp.ones((ONES_ROWS, vx_ref.shape[1]), BF16)
    m_ref[...] = jnp.full(m_ref.shape, NEG_BIG, F32)
    acc_ref[...] = jnp.zeros(acc_ref.shape, F32)

    def origin(t):
        return pl.multiple_of(tile_q_ref[t] * tq, tq), pl.multiple_of(tile_k_ref[t] * tk, tk)

    full = ((0, tk, 0, tq),)
    masked_parts = ((0, tk // 2, 0, tq), (tk // 2, tk // 2, tq // 2, tq // 2)) if skip_quarter else full

    def scores(t, s_ref, parts):
        qs, ks = origin(t)
        for r0, rk, c0, cq in parts:
            s_ref[r0:r0 + rk, c0:c0 + cq] = lax.dot_general(
                k_ref[0, pl.ds(pl.multiple_of(ks + r0, rk), rk), :],
                q_ref[0, pl.ds(pl.multiple_of(qs + c0, cq), cq), :], NT_DIMS, preferred_element_type=F32)

    def softmax_pv(t, s_ref, masked):
        qs, ks = origin(t)
        for r0, rk, c0, cq in (masked_parts if masked else full):
            s = s_ref[r0:r0 + rk, c0:c0 + cq]
            if masked:
                k_row = r0 + lax.broadcasted_iota(jnp.int32, (rk, cq), 0)
                pad = jnp.where(k_row < n_keys - ks, 0, 1 << 20)
                visible = rel_ref[r0:r0 + rk, c0:c0 + cq] + pad <= (past + qs - ks) // CHUNK
                s = jnp.where(visible, s, NEG_BIG)
            cols = pl.ds(pl.multiple_of(qs + c0, cq), cq)
            m = m_ref[:, cols]
            m_new = jnp.maximum(m, jnp.max(s, axis=0, keepdims=True))
            p = jnp.exp2(s - m_new).astype(BF16)
            pv = jnp.dot(vx_ref[:, pl.ds(pl.multiple_of(ks + r0, rk), rk)], p,
                         preferred_element_type=F32)
            acc_ref[:, cols] = jnp.exp2(m - m_new) * acc_ref[:, cols] + pv
            m_ref[:, cols] = m_new

    bufs = (sa_ref, sb_ref)

    def step(t, parity, masked, next_masked):
        scores(t + 1, bufs[1 - parity], masked_parts if next_masked else full)
        softmax_pv(t, bufs[parity], masked)

    def run(t0, t1, masked, next_masked):
        t = t0
        if t < t1 and t % 2 == 1:
            step(t, 1, masked, next_masked)
            t += 1
        trips = (t1 - t) // ATTN_UNROLL
        if trips > 0:
            def body(i, _, base=t):
                tt = base + ATTN_UNROLL * i
                for u in range(ATTN_UNROLL):
                    step(tt + u, u % 2, masked, next_masked)
                return 0
            lax.fori_loop(0, trips, body, 0)
            t += ATTN_UNROLL * trips
        while t < t1:
            step(t, t % 2, masked, next_masked)
            t += 1

    last = n_tiles - 1
    plain_end = min(n_plain, last)
    scores(0, sa_ref, full if n_plain else masked_parts)
    run(0, plain_end - 1, False, False)
    if plain_end >= 1:
        step(plain_end - 1, (plain_end - 1) % 2, False, plain_end >= n_plain)
    run(plain_end, last, True, True)
    softmax_pv(last, bufs[last % 2], last >= n_plain)
    acc = acc_ref[...]
    o_ref[0] = (acc[:V_HEAD] / acc[V_HEAD:V_HEAD + 1]).T.astype(BF16)


def _attention_tiles(seq, n_keys_pad, n_keys, past, tq, tk):
    plain, masked = [], []
    for j in range(n_keys_pad // tk):
        for i in range(seq // tq):
            q_lo, k_lo = past + i * tq, j * tk
            q_hi, k_hi = q_lo + tq - 1, k_lo + tk - 1
            if k_lo // CHUNK > q_hi // CHUNK or k_lo >= n_keys:
                continue
            if k_hi // CHUNK <= q_lo // CHUNK and k_hi < n_keys:
                plain.append((i, j))
            else:
                masked.append((i, j))
    return plain, masked


def _attention(q, k, vt, batch, seq, n_keys_pad, n_keys, past, tq, tk):
    assert past % CHUNK == 0 and tk % CHUNK == 0 and (tq % CHUNK == 0 or tq == seq)
    plain, masked = _attention_tiles(seq, n_keys_pad, n_keys, past, tq, tk)
    tiles = plain + masked
    skip_quarter = (tq // 2) % CHUNK == 0 and (tk // 2) % CHUNK == 0 and n_keys == n_keys_pad and all(
        (j * tk + tk // 2) // CHUNK > (past + i * tq + tq // 2 - 1) // CHUNK for i, j in masked)
    tile_q = jnp.asarray([t[0] for t in tiles], jnp.int32)
    tile_k = jnp.asarray([t[1] for t in tiles], jnp.int32)
    smem = pl.BlockSpec(memory_space=pltpu.SMEM)
    return pl.pallas_call(
        functools.partial(_attn_kernel, tq=tq, tk=tk, past=past, n_keys=n_keys,
                          n_plain=len(plain), n_tiles=len(tiles), skip_quarter=skip_quarter),
        grid=(batch, MLA_HEADS),
        in_specs=[
            smem, smem,
            pl.BlockSpec((1, seq, QK_PAD), lambda b, h: (b, 0, h)),
            pl.BlockSpec((1, n_keys_pad, QK_PAD), lambda b, h: (b, 0, h)),
            pl.BlockSpec((1, V_HEAD, n_keys_pad), lambda b, h: (b, h, 0)),
        ],
        out_specs=pl.BlockSpec((1, seq, V_HEAD), lambda b, h: (b, 0, h)),
        out_shape=jax.ShapeDtypeStruct((batch, seq, D_MLA), BF16),
        scratch_shapes=[
            pltpu.VMEM((V_HEAD + ONES_ROWS, n_keys_pad), BF16),
            pltpu.VMEM((tk, tq), jnp.int32),
            pltpu.VMEM((tk, tq), F32),
            pltpu.VMEM((tk, tq), F32),
            pltpu.VMEM((1, seq), F32),
            pltpu.VMEM((V_HEAD + ONES_ROWS, seq), F32),
        ],
        compiler_params=_cparams(("arbitrary", "arbitrary")),
        name="attn",
    )(tile_q, tile_k, q, k, vt)


LAT_KEY = KV_LORA + ROPE_PAD


def _absorbed_attn_kernel(q_ref, latp_ref, krp_ref, latn_ref, krn_ref, wuk_ref, wuv_ref, o_ref,
                          qx_ref, kx_ref, vx_ref, *, seq, past, n_keys):
    for h in range(MLA_HEADS):
        rows = slice(h * seq, (h + 1) * seq)
        q_lat = jnp.dot(q_ref[:, h * QK_PAD:h * QK_PAD + QK_NOPE], wuk_ref[h], preferred_element_type=F32)
        qx_ref[rows, :KV_LORA] = q_lat.astype(BF16)
        qx_ref[rows, KV_LORA:] = q_ref[:, h * QK_PAD + QK_NOPE:(h + 1) * QK_PAD]
    kx_ref[:past, :KV_LORA] = latp_ref[0].astype(BF16)
    kx_ref[:past, KV_LORA:] = krp_ref[0].astype(BF16)
    kx_ref[past:n_keys, :KV_LORA] = latn_ref[...].astype(BF16)
    kx_ref[past:n_keys, KV_LORA:] = krn_ref[...].astype(BF16)
    kx_ref[n_keys:, :] = jnp.zeros((kx_ref.shape[0] - n_keys, LAT_KEY), BF16)
    kx = kx_ref[...]
    s = lax.dot_general(qx_ref[...], kx, NT_DIMS, preferred_element_type=F32)
    q_chunk = (past + lax.broadcasted_iota(jnp.int32, (s.shape[0], 1), 0) % seq) // CHUNK
    k_idx = lax.broadcasted_iota(jnp.int32, (1, s.shape[1]), 1)
    visible = jnp.logical_and(k_idx // CHUNK <= q_chunk, k_idx < n_keys)
    s = jnp.where(visible, s, NEG_BIG)
    p = jnp.exp2(s - jnp.max(s, axis=-1, keepdims=True)).astype(BF16)
    vx_ref[:, :KV_LORA] = kx[:, :KV_LORA]
    vx_ref[:, KV_LORA:] = jnp.ones((vx_ref.shape[0], ROPE_PAD), BF16)
    o_ext = jnp.dot(p, vx_ref[...], preferred_element_type=F32)
    denom = o_ext[:, KV_LORA:]
    o_lat = (o_ext[:, :KV_LORA] / jnp.concatenate([denom] * (KV_LORA // ROPE_PAD), axis=1)).astype(BF16)
    for h in range(MLA_HEADS):
        o_ref[:, h * V_HEAD:(h + 1) * V_HEAD] = jnp.dot(
            o_lat[h * seq:(h + 1) * seq], wuv_ref[h], preferred_element_type=F32).astype(BF16)


def _absorbed_attention(q, lat_past, kr_past, lat_new, kr_new, w_uk_t, w_uv, batch, seq, n_keys_pad):
    past = lat_past.shape[1]
    n_keys = past + seq
    const3 = lambda b: (0, 0, 0)
    return pl.pallas_call(
        functools.partial(_absorbed_attn_kernel, seq=seq, past=past, n_keys=n_keys),
        grid=(batch,),
        in_specs=[
            pl.BlockSpec((seq, MLA_HEADS * QK_PAD), lambda b: (b, 0)),
            pl.BlockSpec((1, past, KV_LORA), lambda b: (b, 0, 0)),
            pl.BlockSpec((1, past, ROPE_PAD), lambda b: (b, 0, 0)),
            pl.BlockSpec((seq, KV_LORA), lambda b: (b, 0)),
            pl.BlockSpec((seq, ROPE_PAD), lambda b: (b, 0)),
            pl.BlockSpec((MLA_HEADS, QK_NOPE, KV_LORA), const3),
            pl.BlockSpec((MLA_HEADS, KV_LORA, V_HEAD), const3),
        ],
        out_specs=pl.BlockSpec((seq, D_MLA), lambda b: (b, 0)),
        out_shape=jax.ShapeDtypeStruct((batch * seq, D_MLA), BF16),
        scratch_shapes=[
            pltpu.VMEM((MLA_HEADS * seq, LAT_KEY), BF16),
            pltpu.VMEM((n_keys_pad, LAT_KEY), BF16),
            pltpu.VMEM((n_keys_pad, LAT_KEY), BF16),
        ],
        compiler_params=_cparams(("arbitrary",)),
        name="absorbed_attn",
    )(q, lat_past, kr_past, lat_new, kr_new, w_uk_t, w_uv)


def _out_proj_kernel(oh_ref, oa_ref, h_ref, w_ref, g1_ref, b1_ref, y_ref, yb_ref):
    for r0 in range(0, h_ref.shape[0], ROW_SUB):
        rows = slice(r0, r0 + ROW_SUB)
        mix = jnp.dot(oh_ref[rows, :], w_ref[:D_HGRN, :], preferred_element_type=F32)
        mix = mix + jnp.dot(oa_ref[rows, :], w_ref[D_HGRN:, :], preferred_element_type=F32)
        y = _layer_norm(ALPHA * h_ref[rows, :] + mix, g1_ref[...], b1_ref[...])
        y_ref[rows, :] = y
        yb_ref[rows, :] = y.astype(BF16)


def _out_proj(oh, oa, h, w_out, ln1_g, ln1_b, tm):
    m = h.shape[0]
    row = lambda i: (i, 0)
    const = lambda i: (0, 0)
    vec = pl.BlockSpec((1, D_MODEL), const)
    return pl.pallas_call(
        _out_proj_kernel,
        grid=(m // tm,),
        in_specs=[
            pl.BlockSpec((tm, D_HGRN), row),
            pl.BlockSpec((tm, D_MLA), row),
            pl.BlockSpec((tm, D_MODEL), row),
            pl.BlockSpec((D_HGRN + D_MLA, D_MODEL), const),
            vec, vec,
        ],
        out_specs=[pl.BlockSpec((tm, D_MODEL), row), pl.BlockSpec((tm, D_MODEL), row)],
        out_shape=[jax.ShapeDtypeStruct((m, D_MODEL), F32), jax.ShapeDtypeStruct((m, D_MODEL), BF16)],
        compiler_params=_cparams(("arbitrary",)),
        name="out_proj",
    )(oh, oa, h, w_out, ln1_g, ln1_b)


def _ffn_up_kernel(x_ref, wu_ref, wg_ref, cw_ref, cb_ref, buf_ref, h_ref, cnew_ref, carry_ref, *,
                   spt, tiles_per_seq):
    i = pl.program_id(1)
    tm = x_ref.shape[0]
    tf = wu_ref.shape[1]
    rows = tm // spt

    if tiles_per_seq > 1:
        @pl.when(i % tiles_per_seq == 0)
        def _():
            carry_ref[...] = buf_ref[...]
        halo_ref = carry_ref
    else:
        halo_ref = buf_ref

    xb = x_ref[...]
    for c0 in range(0, tf, FFN_SUB):
        width = min(FFN_SUB, tf - c0)
        cs = slice(c0, c0 + width)
        r = lax.broadcasted_iota(jnp.int32, (spt, rows, width), 1)
        if 2 * width <= FFN_SUB:
            both = jnp.dot(xb, jnp.concatenate([wu_ref[:, cs], wg_ref[:, cs]], axis=1),
                           preferred_element_type=F32)
            u, gate = both[:, :width], both[:, width:]
        else:
            u = jnp.dot(xb, wu_ref[:, cs], preferred_element_type=F32)
            gate = jnp.dot(xb, wg_ref[:, cs], preferred_element_type=F32)
        u = u.reshape(spt, rows, width)
        gate = gate.reshape(spt, rows, width)
        halo = halo_ref[:, :, cs]
        prev1 = jnp.where(r == 0, halo[:, 1:2, :], pltpu.roll(u, 1, 1))
        prev2 = jnp.where(r == 0, halo[:, 0:1, :],
                          jnp.where(r == 1, halo[:, 1:2, :], pltpu.roll(u, 2, 1)))
        cw = cw_ref[:, cs]
        a = cb_ref[:, cs] + (prev2 * cw[0:1] + prev1 * cw[1:2] + u * cw[2:3])
        h_ref[:, cs] = (jax.nn.silu(a) * gate).astype(BF16).reshape(tm, width)
        tail = u[:, rows - (CONV_W - 1):, :]
        cnew_ref[:, :, cs] = tail
        if tiles_per_seq > 1:
            carry_ref[:, :, cs] = tail


def _ffn_up(x, w_up, w_gate, conv_w, conv_b, conv_buf, seq, tm, tf):
    m = x.shape[0]
    n_seq = m // seq
    if tm >= seq:
        spt, tiles_per_seq = tm // seq, 1
        seq_of = lambda i: i
    else:
        spt, tiles_per_seq = 1, seq // tm
        seq_of = lambda i: i // tiles_per_seq
    n_f = D_FF // tf
    hid, tails = pl.pallas_call(
        functools.partial(_ffn_up_kernel, spt=spt, tiles_per_seq=tiles_per_seq),
        grid=(n_f, m // tm),
        in_specs=[
            pl.BlockSpec((tm, D_MODEL), lambda f, i: (i, 0)),
            pl.BlockSpec((D_MODEL, tf), lambda f, i: (0, f)),
            pl.BlockSpec((D_MODEL, tf), lambda f, i: (0, f)),
            pl.BlockSpec((CONV_W, tf), lambda f, i: (0, f)),
            pl.BlockSpec((1, tf), lambda f, i: (0, f)),
            pl.BlockSpec((spt, CONV_W - 1, tf), lambda f, i: (seq_of(i), 0, f)),
        ],
        out_specs=[
            pl.BlockSpec((tm, tf), lambda f, i: (i, f)),
            pl.BlockSpec((spt, CONV_W - 1, tf), lambda f, i: (i, 0, f)),
        ],
        out_shape=[
            jax.ShapeDtypeStruct((m, D_FF), BF16),
            jax.ShapeDtypeStruct((n_seq * tiles_per_seq, CONV_W - 1, D_FF), F32),
        ],
        scratch_shapes=[pltpu.VMEM((spt, CONV_W - 1, tf), F32)],
        compiler_params=_cparams(("arbitrary", "arbitrary")),
        name="ffn_up",
    )(x, w_up, w_gate, conv_w, conv_b, conv_buf)
    return hid, tails[tiles_per_seq - 1::tiles_per_seq]


def _ffn_down_kernel(h_ref, x_ref, w_ref, g_ref, b_ref, y_ref):
    for r0 in range(0, x_ref.shape[0], ROW_SUB):
        rows = slice(r0, r0 + ROW_SUB)
        ff = jnp.dot(h_ref[rows, :], w_ref[...], preferred_element_type=F32)
        y_ref[rows, :] = _layer_norm(ALPHA * x_ref[rows, :] + ff, g_ref[...], b_ref[...])


def _ffn_down(hid, x, w_down, ln2_g, ln2_b, tm):
    m = x.shape[0]
    row = lambda i: (i, 0)
    const = lambda i: (0, 0)
    return pl.pallas_call(
        _ffn_down_kernel,
        grid=(m // tm,),
        in_specs=[
            pl.BlockSpec((tm, D_FF), row),
            pl.BlockSpec((tm, D_MODEL), row),
            pl.BlockSpec((D_FF, D_MODEL), const, pipeline_mode=pl.Buffered(1)),
            pl.BlockSpec((1, D_MODEL), const),
            pl.BlockSpec((1, D_MODEL), const),
        ],
        out_specs=pl.BlockSpec((tm, D_MODEL), row),
        out_shape=jax.ShapeDtypeStruct((m, D_MODEL), F32),
        compiler_params=_cparams(("arbitrary",)),
        name="ffn_down",
    )(hid, x, w_down, ln2_g, ln2_b)


def _rope_tables(past, seq, rows):
    inv = ROPE_THETA ** (-np.arange(0, QK_ROPE, 2, dtype=np.float64) / QK_ROPE)
    ang = (past + np.arange(seq, dtype=np.float64))[:, None] * inv[None, :]
    cos, sin = np.cos(ang), np.sin(ang)
    zero = np.zeros((seq, ROPE_PAD - QK_ROPE))
    cos_t = np.concatenate([cos, cos, zero], axis=1).astype(np.float32)
    sin_t = np.concatenate([-sin, sin, zero], axis=1).astype(np.float32)
    reps = rows // seq
    return jnp.asarray(np.tile(cos_t, (reps, 1))), jnp.asarray(np.tile(sin_t, (reps, 1)))


def _swap_halves(w):
    half = QK_ROPE // 2
    return jnp.concatenate([w[..., half:], w[..., :half]], axis=-1)


def _prep_weights(w_in, w_q_b, w_kv_b, w_out, w_ffn_up, w_ffn_gate, w_ffn_down):
    w_in = w_in[0]
    c0 = 4 * D_HGRN
    w_h = w_in.astype(BF16)
    kr_cols = w_in[:, c0 + Q_LORA + KV_LORA:]
    zpad = jnp.zeros((D_MODEL, ROPE_PAD - QK_ROPE), F32)
    w_m = jnp.concatenate([kr_cols, zpad, _swap_halves(kr_cols), zpad], axis=1).astype(BF16)

    wq = w_q_b[0].reshape(Q_LORA, MLA_HEADS, QK_NOPE + QK_ROPE)
    zq = jnp.zeros((Q_LORA, MLA_HEADS, QK_PAD - QK_NOPE - QK_ROPE), F32)
    w_q = jnp.concatenate([wq, zq], axis=-1).reshape(Q_LORA, MLA_HEADS * QK_PAD).astype(BF16)
    zr = jnp.zeros((Q_LORA, MLA_HEADS, ROPE_PAD - QK_ROPE), F32)
    w_q_sw = jnp.concatenate([_swap_halves(wq[..., QK_NOPE:]), zr], axis=-1)
    w_q_sw = w_q_sw.reshape(Q_LORA, MLA_HEADS * ROPE_PAD).astype(BF16)

    wkv = w_kv_b[0].reshape(KV_LORA, MLA_HEADS, QK_NOPE + V_HEAD)
    w_k = wkv[..., :QK_NOPE].reshape(KV_LORA, MLA_HEADS * QK_NOPE).astype(BF16)
    w_vt = wkv[..., QK_NOPE:].reshape(KV_LORA, D_MLA).T.astype(BF16)
    w_uk_t = wkv[..., :QK_NOPE].transpose(1, 2, 0).astype(BF16)
    w_uv = wkv[..., QK_NOPE:].transpose(1, 0, 2).astype(BF16)
    return dict(w_h=w_h, w_m=w_m, w_q=w_q, w_q_sw=w_q_sw, w_k=w_k, w_vt=w_vt, w_uk_t=w_uk_t, w_uv=w_uv,
                w_out=w_out[0].astype(BF16), w_up=w_ffn_up[0].astype(BF16),
                w_gate=w_ffn_gate[0].astype(BF16), w_down=w_ffn_down[0].astype(BF16))


def _row(a):
    return a.reshape(1, -1)


def _tile_plan(batch, seq, n_keys_pad):
    m = batch * seq
    tm = min(ROW_TILE, m)
    tm_q = min(QKV_ROW_TILE, m)
    tm_ffn = min(FFN_ROW_TILE, m)
    assert m % tm == 0 and m % tm_q == 0 and m % tm_ffn == 0 and (tm_ffn % seq == 0 or seq % tm_ffn == 0)
    return dict(
        tm=tm,
        tm_q=tm_q,
        tm_ffn=tm_ffn,
        rows_hgrn=min(HGRN_STEP_ROWS, seq),
        tq=min(ATTN_TILE, seq),
        tk=ATTN_TILE if n_keys_pad % ATTN_TILE == 0 else n_keys_pad,
        tm_kv=QKV_ROW_TILE if n_keys_pad % QKV_ROW_TILE == 0 else n_keys_pad,
    )


def _encoder(x, s0, lat_past, kr_past, conv_buf, p, w, *, blk):
    batch, seq, _ = x.shape
    past = 0 if lat_past is None else lat_past.shape[1]
    m = batch * seq
    n_keys = past + seq
    key_pad = -n_keys % KEY_ALIGN
    n_keys_pad = n_keys + key_pad
    t = _tile_plan(batch, seq, n_keys_pad)
    x2 = x.reshape(m, D_MODEL)
    cos_k, sin_k = _rope_tables(past, seq, max(seq, t["tm_q"]))

    hg, cqn, lat_new, kr_new, h = _in_proj(x2, p["ln_in_g"], p["ln_in_b"], w["w_h"], w["w_m"],
                                           p["q_a_g"], p["kv_a_g"], cos_k, sin_k, t["tm"])
    o_h, s_new = _hgrn(hg, p["lb"], p["hgrn_norm_g"], s0, batch, seq, blk, t["rows_hgrn"])

    q = _q_proj(cqn, w["w_q"], w["w_q_sw"], cos_k, sin_k, t["tm_q"])
    lat3 = lat_new.reshape(batch, seq, KV_LORA)
    kr3 = kr_new.reshape(batch, seq, ROPE_PAD)
    if past and seq * MLA_HEADS <= ABSORB_MAX_ROWS:
        kr_past_pad = jnp.pad(kr_past, ((0, 0), (0, 0), (0, ROPE_PAD - QK_ROPE)))
        o_a = _absorbed_attention(q, lat_past, kr_past_pad, lat_new, kr_new, w["w_uk_t"], w["w_uv"],
                                  batch, seq, n_keys_pad)
    else:
        parts_l, parts_r = [lat3], [kr3]
        if past:
            parts_l.insert(0, lat_past)
            parts_r.insert(0, jnp.pad(kr_past, ((0, 0), (0, 0), (0, ROPE_PAD - QK_ROPE))))
        if key_pad:
            parts_l.append(jnp.zeros((batch, key_pad, KV_LORA), F32))
            parts_r.append(jnp.zeros((batch, key_pad, ROPE_PAD), F32))
        lat_all = jnp.concatenate(parts_l, axis=1) if len(parts_l) > 1 else lat3
        kr_all = jnp.concatenate(parts_r, axis=1) if len(parts_r) > 1 else kr3
        k_cat, vt = _kv_proj(lat_all.reshape(batch * n_keys_pad, KV_LORA),
                             kr_all.reshape(batch * n_keys_pad, ROPE_PAD), w["w_k"], w["w_vt"], batch, t["tm_kv"])
        o_a = _attention(q.reshape(batch, seq, MLA_HEADS * QK_PAD),
                         k_cat.reshape(batch, n_keys_pad, MLA_HEADS * QK_PAD), vt,
                         batch, seq, n_keys_pad, n_keys, past, t["tq"], t["tk"]).reshape(m, D_MLA)

    h1, h1_bf16 = _out_proj(o_h, o_a, h, w["w_out"], p["ln1_g"], p["ln1_b"], t["tm"])
    hid, conv_new = _ffn_up(h1_bf16, w["w_up"], w["w_gate"], p["conv_w"], p["conv_b"], conv_buf, seq, t["tm_ffn"],
                            FFN_COL_TILE)
    y = _ffn_down(hid, h1, w["w_down"], p["ln2_g"], p["ln2_b"], t["tm"])
    return (y.reshape(batch, seq, D_MODEL), s_new[None], lat3[None],
            kr3[:, :, :QK_ROPE][None], conv_new[None])


def kernel(x_prompt, x_sample, cache_kv_latent, cache_k_rope, state_hgrn, cache_ffn_conv, lb_param, ln_in_g, ln_in_b, w_in, hgrn_norm_g, q_a_g, w_q_b, kv_a_g, w_kv_b, w_out, ln1_g, ln1_b, w_ffn_up, w_ffn_gate, conv_w, conv_b, w_ffn_down, ln2_g, ln2_b):
    w = _prep_weights(w_in, w_q_b, w_kv_b, w_out, w_ffn_up, w_ffn_gate, w_ffn_down)
    lbs = jnp.cumsum(jax.nn.softmax(lb_param.astype(F32), axis=0), axis=0)
    p = dict(ln_in_g=_row(ln_in_g), ln_in_b=_row(ln_in_b), lb=_row(lbs[0]),
             hgrn_norm_g=_row(hgrn_norm_g[0]), q_a_g=_row(q_a_g[0]), kv_a_g=_row(kv_a_g[0]),
             ln1_g=_row(ln1_g[0]), ln1_b=_row(ln1_b[0]), conv_w=conv_w[0], conv_b=_row(conv_b[0]),
             ln2_g=_row(ln2_g[0]), ln2_b=_row(ln2_b[0]))

    n_prompt = x_prompt.shape[0]
    conv0 = jnp.zeros((n_prompt, CONV_W - 1, D_FF), F32)
    y_p, s_p, lat_p, kr_p, conv_p = _encoder(x_prompt, None, None, None, conv0, p, w, blk=CHUNK)
    y_s, s_s, lat_s, kr_s, conv_s = _encoder(
        x_sample, state_hgrn[0], cache_kv_latent[0], cache_k_rope[0], cache_ffn_conv[0], p, w,
        blk=x_sample.shape[1])
    return (y_p, y_s, lat_p, kr_p, s_p, conv_p, lat_s, kr_s, s_s, conv_s)
```
